```python
import math
import jax, jax.numpy as jnp
from jax import lax
import numpy as np

D_MODEL = 2048
BATCH = 4
SEQ = 2048
DEPTH = 1
DEC_BATCH = 128
DEC_SEQ = 1
PAST_LEN = 16384
PAGE_SIZE = 128

N_META = 16
RMS_EPS = 1e-6
SSM_D_INNER = D_MODEL
SSM_HEAD_DIM = 64
SSM_N_HEADS = SSM_D_INNER // SSM_HEAD_DIM
SSM_N_GROUPS = 4
SSM_HPG = SSM_N_HEADS // SSM_N_GROUPS
SSM_D_STATE = 128
SSM_CONV = 4
SSM_CHUNK = 128
SSM_CONV_DIM = SSM_D_INNER + 2 * SSM_N_GROUPS * SSM_D_STATE
SSM_NORM_EPS = 1e-5
RWKV_DIM = D_MODEL
RWKV_HEAD = 64
RWKV_N_HEADS = RWKV_DIM // RWKV_HEAD
RWKV_W_RANK = 64
RWKV_A_RANK = 64
RWKV_GN_EPS = 64e-5
RWKV_SHIFT_DIM = 3 * RWKV_DIM + RWKV_W_RANK + RWKV_A_RANK
N_EXPERTS = 32
TOP_K = 4
D_FF = D_MODEL
SWIGLU_LIMIT = 7.0
SWIGLU_ALPHA = 1.702
MOE_BLOCK = 128
PROJ_WIDTHS = (SSM_D_INNER, SSM_CONV_DIM, SSM_N_HEADS, RWKV_SHIFT_DIM, 2 * D_MODEL)
PROJ_DIM = sum(PROJ_WIDTHS)

kernel_name = 'meta_ssd_rwkv7_gated_moe_step'


def rmsnorm(x, g, eps=RMS_EPS):
    xf = x.astype(jnp.float32)
    y = xf * lax.rsqrt(jnp.mean(xf * xf, axis=-1, keepdims=True) + eps)
    return (y * g.astype(jnp.float32)).astype(x.dtype)


def split_cols(t, widths):
    out, o = [], 0
    for w in widths:
        out.append(t[..., o:o + w])
        o += w
    return out


def causal_dwconv(buf, w, b):
    T = buf.shape[1] - (SSM_CONV - 1)
    out = b
    for k in range(SSM_CONV):
        out = out + buf[:, k:k + T] * w[k]
    return out


def ssd_chunked(x, dt, A, Bm, Cm):
    b, L = x.shape[:2]
    pad = (-L) % SSM_CHUNK
    padf = lambda t: jnp.pad(t, [(0, 0), (pad, 0)] + [(0, 0)] * (t.ndim - 2))
    xp, dtp, Bp, Cp = padf(x), padf(dt), padf(Bm), padf(Cm)
    nc = (L + pad) // SSM_CHUNK
    G, J, P, N, Q = SSM_N_GROUPS, SSM_HPG, SSM_HEAD_DIM, SSM_D_STATE, SSM_CHUNK
    xdt = (xp * dtp[..., None]).reshape(b, nc, Q, G, J, P)
    dA = (dtp * A).reshape(b, nc, Q, G, J)
    Bc = Bp.reshape(b, nc, Q, G, N)
    Cc = Cp.reshape(b, nc, Q, G, N)
    cs = jnp.cumsum(dA, axis=2)
    seg = cs[:, :, :, None] - cs[:, :, None, :]
    causal = jnp.tril(jnp.ones((Q, Q), dtype=bool))[None, None, :, :, None, None]
    Lmat = jnp.where(causal, jnp.exp(jnp.where(causal, seg, 0.0)), 0.0)
    CB = jnp.einsum('bclgn,bcsgn->bclsg', Cc, Bc)
    y_diag = jnp.einsum('bclsg,bclsgj,bcsgjp->bclgjp', CB, Lmat, xdt)
    decay_to_end = jnp.exp(cs[:, :, -1:] - cs)
    chunk_states = jnp.einsum('bclgn,bclgj,bclgjp->bcgjpn', Bc, decay_to_end, xdt)
    chunk_decay = jnp.exp(cs[:, :, -1])

    def step(S, inp):
        st, dec = inp
        return S * dec[..., None, None] + st, S

    S0 = jnp.zeros((b, G, J, P, N), xdt.dtype)
    S_fin, S_in = lax.scan(step, S0, (jnp.moveaxis(chunk_states, 1, 0), jnp.moveaxis(chunk_decay, 1, 0)))
    S_in = jnp.moveaxis(S_in, 0, 1)
    y_off = jnp.einsum('bclgn,bcgjpn,bclgj->bclgjp', Cc, S_in, jnp.exp(cs))
    y = (y_diag + y_off).reshape(b, nc * Q, SSM_N_HEADS, P)[:, pad:]
    return y, S_fin.reshape(b, SSM_N_HEADS, P, N)


def ssd_recurrent(x, dt, A, Bm, Cm, S0):
    Bh = jnp.repeat(Bm, SSM_HPG, axis=2)
    Ch = jnp.repeat(Cm, SSM_HPG, axis=2)

    def step(S, inp):
        x_t, dt_t, B_t, C_t = inp
        S = S * jnp.exp(dt_t * A)[..., None, None] + jnp.einsum('bhp,bhn->bhpn', x_t * dt_t[..., None], B_t)
        return S, jnp.einsum('bhpn,bhn->bhp', S, C_t)

    tm = lambda t: jnp.moveaxis(t, 1, 0)
    S, y = lax.scan(step, S0, (tm(x), tm(dt), tm(Bh), tm(Ch)))
    return jnp.moveaxis(y, 0, 1), S


def ssm_branch(z, xbc, dt_raw, conv_state, ssm_state, conv_w, conv_b, dt_bias, A_log, D_skip, norm_g):
    b, T = z.shape[:2]
    buf = jnp.concatenate([conv_state.astype(xbc.dtype), xbc], axis=1)
    new_conv = buf[:, -(SSM_CONV - 1):]
    xbc = jax.nn.silu(causal_dwconv(buf, conv_w, conv_b))
    GN = SSM_N_GROUPS * SSM_D_STATE
    xs, Bm, Cm = split_cols(xbc, (SSM_D_INNER, GN, GN))
    xs = xs.reshape(b, T, SSM_N_HEADS, SSM_HEAD_DIM)
    Bm = Bm.reshape(b, T, SSM_N_GROUPS, SSM_D_STATE)
    Cm = Cm.reshape(b, T, SSM_N_GROUPS, SSM_D_STATE)
    dt = jax.nn.softplus(dt_raw + dt_bias)
    A = -jnp.exp(A_log.astype(jnp.float32))
    if ssm_state is None:
        y, S = ssd_chunked(xs, dt, A, Bm, Cm)
    else:
        y, S = ssd_recurrent(xs, dt, A, Bm, Cm, ssm_state.astype(jnp.float32))
    y = y + D_skip[:, None] * xs
    y = y.reshape(b, T, SSM_D_INNER) * jax.nn.silu(z)
    yg = y.reshape(b, T, SSM_N_GROUPS, -1)
    yg = yg * lax.rsqrt(jnp.mean(yg * yg, axis=-1, keepdims=True) + SSM_NORM_EPS)
    return yg.reshape(b, T, SSM_D_INNER) * norm_g, new_conv, S


def rwkv_branch(rw, shift_state, wkv_state, mu, w0, w2, a0, a2, k_k, k_a, r_k, gn_g, gn_b):
    b, T = rw.shape[:2]
    prev = jnp.concatenate([shift_state[:, None].astype(rw.dtype), rw[:, :-1]], axis=1)
    new_shift = rw[:, -1]
    mixed = rw + (prev - rw) * mu
    r, k, v, wl, al = split_cols(mixed, (RWKV_DIM, RWKV_DIM, RWKV_DIM, RWKV_W_RANK, RWKV_A_RANK))
    w_log = -jax.nn.softplus(-(w0 + jnp.tanh(wl) @ w2)) - 0.5
    decay = jnp.exp(-jnp.exp(w_log))
    a = jax.nn.sigmoid(a0 + al @ a2)
    heads = lambda t: t.reshape(b, T, RWKV_N_HEADS, RWKV_HEAD)
    kk = heads(k * k_k)
    kk = kk / jnp.maximum(jnp.linalg.norm(kk, axis=-1, keepdims=True), 1e-12)
    k = k * (1.0 + (a - 1.0) * k_a)
    r_h, k_h, v_h, w_h, a_h = heads(r), heads(k), heads(v), heads(decay), heads(a)

    def step(S, inp):
        r_t, w_t, k_t, v_t, kk_t, a_t = inp
        sa = jnp.einsum('bhij,bhj->bhi', S, -kk_t)
        S = (S * w_t[:, :, None, :] + sa[..., None] * (kk_t * a_t)[:, :, None, :]
             + v_t[..., None] * k_t[:, :, None, :])
        return S, jnp.einsum('bhij,bhj->bhi', S, r_t)

    tm = lambda t: jnp.moveaxis(t, 1, 0)
    S, y = lax.scan(step, wkv_state.astype(jnp.float32),
                    (tm(r_h), tm(w_h), tm(k_h), tm(v_h), tm(kk), tm(a_h)))
    y = jnp.moveaxis(y, 0, 1)
    mean = jnp.mean(y, axis=-1, keepdims=True)
    var = jnp.mean(jnp.square(y - mean), axis=-1, keepdims=True)
    y = ((y - mean) * lax.rsqrt(var + RWKV_GN_EPS)).reshape(b, T, RWKV_DIM) * gn_g + gn_b
    bonus = jnp.sum(r_h * k_h * r_k, axis=-1, keepdims=True) * v_h
    return y + bonus.reshape(b, T, RWKV_DIM), new_shift, S


def moe_ffn(x, router_w, router_b, w_gate_up, b_gate_up, w_down, b_down):
    shp = x.shape
    xt = x.reshape(-1, D_MODEL)
    T = xt.shape[0]
    logits = (xt @ router_w + router_b).astype(jnp.float32)
    top_v, top_e = lax.top_k(logits, TOP_K)
    gates = jax.nn.softmax(top_v, axis=-1)
    TK = T * TOP_K
    e_flat = top_e.reshape(-1)
    g_flat = gates.reshape(-1)
    tok_flat = jnp.repeat(jnp.arange(T, dtype=jnp.int32), TOP_K)
    order = jnp.argsort(e_flat)
    se = e_flat[order]
    counts = jnp.zeros((N_EXPERTS,), jnp.int32).at[e_flat].add(1)
    padded = (counts + MOE_BLOCK - 1) // MOE_BLOCK * MOE_BLOCK
    ends = jnp.cumsum(padded)
    pstart = ends - padded
    start = jnp.cumsum(counts) - counts
    dest = pstart[se] + jnp.arange(TK, dtype=jnp.int32) - start[se]
    NB = -(-TK // MOE_BLOCK) + N_EXPERTS
    slot_tok = jnp.full((NB * MOE_BLOCK,), T, jnp.int32).at[dest].set(tok_flat[order])
    slot_gate = jnp.zeros((NB * MOE_BLOCK,), jnp.float32).at[dest].set(g_flat[order])
    block_e = jnp.minimum(jnp.searchsorted(ends, jnp.arange(NB, dtype=jnp.int32) * MOE_BLOCK, side='right'),
                          N_EXPERTS - 1)
    x_pad = jnp.concatenate([xt, jnp.zeros((1, D_MODEL), xt.dtype)], axis=0)
    xb = x_pad[slot_tok].reshape(NB, MOE_BLOCK, D_MODEL)

    def expert_block(args):
        xblk, e = args
        hgu = xblk @ w_gate_up[e] + b_gate_up[e]
        g, u = hgu[:, :D_FF], hgu[:, D_FF:]
        g = jnp.minimum(g, SWIGLU_LIMIT)
        u = jnp.clip(u, -SWIGLU_LIMIT, SWIGLU_LIMIT)
        act = (u + 1.0) * (g * jax.nn.sigmoid(g * SWIGLU_ALPHA))
        return act @ w_down[e] + b_down[e]

    yb = lax.map(expert_block, (xb, block_e)).reshape(NB * MOE_BLOCK, D_MODEL)
    y = jnp.zeros((T + 1, D_MODEL), x.dtype).at[slot_tok].add((yb * slot_gate[:, None]).astype(x.dtype))
    return y[:T].reshape(shp)


def trunk_layer(h, conv_state, ssm_state, shift_state, wkv_state, lp):
    xn = rmsnorm(h, lp['ln1_g'])
    proj = (xn @ lp['w_in']).astype(jnp.float32)
    z, xbc, dt_raw, rw, gates = split_cols(proj, PROJ_WIDTHS)
    y_a, new_conv, new_ssm = ssm_branch(z, xbc, dt_raw, conv_state, ssm_state, lp['ssm_conv_w'], lp['ssm_conv_b'],
                                        lp['ssm_dt_bias'], lp['ssm_A_log'], lp['ssm_D'], lp['ssm_norm_g'])
    y_b, new_shift, new_wkv = rwkv_branch(rw, shift_state, wkv_state, lp['rwkv_mu'], lp['rwkv_w0'], lp['rwkv_w2'],
                                          lp['rwkv_a0'], lp['rwkv_a2'], lp['rwkv_k_k'], lp['rwkv_k_a'],
                                          lp['rwkv_r_k'], lp['rwkv_gn_g'], lp['rwkv_gn_b'])
    g_a, g_b = split_cols(jax.nn.sigmoid(gates), (D_MODEL, D_MODEL))
    merged = (g_a * y_a + g_b * y_b).astype(h.dtype)
    h = h + merged @ lp['w_out']
    h = h + moe_ffn(rmsnorm(h, lp['ln2_g']), lp['router_w'], lp['router_b'], lp['w_gate_up'],
                    lp['b_gate_up'], lp['w_down'], lp['b_down'])
    dt_ = h.dtype
    return h, new_conv.astype(dt_), new_ssm.astype(dt_), new_shift.astype(dt_), new_wkv.astype(dt_)


def setup_inputs(seed: int = 0) -> dict:
    key = jax.random.key(seed)
    ks = iter(jax.random.split(key, 48))
    f32 = jnp.float32
    nrm = lambda shape, s: jax.random.normal(next(ks), shape, f32) * s
    uni = lambda shape, lo, hi: jax.random.uniform(next(ks), shape, f32, minval=lo, maxval=hi)
    L, H, P, N = DEPTH, SSM_N_HEADS, SSM_HEAD_DIM, SSM_D_STATE
    dt0 = jnp.exp(uni((L, H), math.log(1e-3), math.log(1e-1)))
    return {
        'x_prompt': nrm((BATCH, SEQ, D_MODEL), 1.0),
        'x_sample': nrm((DEC_BATCH, DEC_SEQ, D_MODEL), 1.0),
        'state_ssm_conv': nrm((L, DEC_BATCH, SSM_CONV - 1, SSM_CONV_DIM), 1.0),
        'state_ssm': nrm((L, DEC_BATCH, H, P, N), 0.1),
        'state_rwkv_shift': nrm((L, DEC_BATCH, RWKV_SHIFT_DIM), 1.0),
        'state_rwkv_wkv': nrm((L, DEC_BATCH, RWKV_N_HEADS, RWKV_HEAD, RWKV_HEAD), 0.1),
        'meta_tokens': nrm((N_META, D_MODEL), 1.0),
        'ln1_g': 1.0 + nrm((L, D_MODEL), 0.02),
        'w_in': nrm((L, D_MODEL, PROJ_DIM), D_MODEL ** -0.5),
        'ssm_conv_w': nrm((L, SSM_CONV, SSM_CONV_DIM), SSM_CONV ** -0.5),
        'ssm_conv_b': nrm((L, SSM_CONV_DIM), 0.02),
        'ssm_dt_bias': dt0 + jnp.log(-jnp.expm1(-dt0)),
        'ssm_A_log': jnp.log(uni((L, H), 1.0, 16.0)),
        'ssm_D': 1.0 + nrm((L, H), 0.02),
        'ssm_norm_g': 1.0 + nrm((L, SSM_D_INNER), 0.02),
        'rwkv_mu': uni((L, RWKV_SHIFT_DIM), 0.0, 1.0),
        'rwkv_w0': uni((L, RWKV_DIM), -6.5, -1.5),
        'rwkv_w2': nrm((L, RWKV_W_RANK, RWKV_DIM), 0.1),
        'rwkv_a0': nrm((L, RWKV_DIM), 0.1),
        'rwkv_a2': nrm((L, RWKV_A_RANK, RWKV_DIM), 0.1),
        'rwkv_k_k': 0.85 + nrm((L, RWKV_DIM), 0.02),
        'rwkv_k_a': 1.0 + nrm((L, RWKV_DIM), 0.02),
        'rwkv_r_k': nrm((L, RWKV_N_HEADS, RWKV_HEAD), 0.1),
        'rwkv_gn_g': 1.0 + nrm((L, RWKV_DIM), 0.02),
        'rwkv_gn_b': nrm((L, RWKV_DIM), 0.02),
        'w_out': nrm((L, D_MODEL, D_MODEL), D_MODEL ** -0.5),
        'ln2_g': 1.0 + nrm((L, D_MODEL), 0.02),
        'router_w': nrm((L, D_MODEL, N_EXPERTS), D_MODEL ** -0.5),
        'router_b': nrm((L, N_EXPERTS), 0.01),
        'w_gate_up': nrm((L, N_EXPERTS, D_MODEL, 2 * D_FF), D_MODEL ** -0.5),
        'b_gate_up': nrm((L, N_EXPERTS, 2 * D_FF), 0.01),
        'w_down': nrm((L, N_EXPERTS, D_FF, D_MODEL), D_FF ** -0.5),
        'b_down': nrm((L, N_EXPERTS, D_MODEL), 0.01),
        'lnf_g': 1.0 + nrm((D_MODEL,), 0.02),
    }


def reference(x_prompt, x_sample, state_ssm_conv, state_ssm, state_rwkv_shift, state_rwkv_wkv, meta_tokens,
              ln1_g, w_in, ssm_conv_w, ssm_conv_b, ssm_dt_bias, ssm_A_log, ssm_D, ssm_norm_g, rwkv_mu, rwkv_w0,
              rwkv_w2, rwkv_a0, rwkv_a2, rwkv_k_k, rwkv_k_a, rwkv_r_k, rwkv_gn_g, rwkv_gn_b, w_out, ln2_g,
              router_w, router_b, w_gate_up, b_gate_up, w_down, b_down, lnf_g):
    bp = x_prompt.shape[0]
    meta = jnp.broadcast_to(meta_tokens.astype(x_prompt.dtype)[None], (bp, N_META, D_MODEL))
    hp = jnp.concatenate([meta, x_prompt], axis=1)
    hs = x_sample
    zero_conv = jnp.zeros((bp, SSM_CONV - 1, SSM_CONV_DIM), jnp.float32)
    zero_shift = jnp.zeros((bp, RWKV_SHIFT_DIM), jnp.float32)
    zero_wkv = jnp.zeros((bp, RWKV_N_HEADS, RWKV_HEAD, RWKV_HEAD), jnp.float32)
    pc, ps, psh, pw, sc, ss, ssh, sw = [], [], [], [], [], [], [], []
    for l in range(DEPTH):
        lp = {
            'ln1_g': ln1_g[l], 'w_in': w_in[l], 'ssm_conv_w': ssm_conv_w[l], 'ssm_conv_b': ssm_conv_b[l],
            'ssm_dt_bias': ssm_dt_bias[l], 'ssm_A_log': ssm_A_log[l], 'ssm_D': ssm_D[l],
            'ssm_norm_g': ssm_norm_g[l], 'rwkv_mu': rwkv_mu[l], 'rwkv_w0': rwkv_w0[l], 'rwkv_w2': rwkv_w2[l],
            'rwkv_a0': rwkv_a0[l], 'rwkv_a2': rwkv_a2[l], 'rwkv_k_k': rwkv_k_k[l], 'rwkv_k_a': rwkv_k_a[l],
            'rwkv_r_k': rwkv_r_k[l], 'rwkv_gn_g': rwkv_gn_g[l], 'rwkv_gn_b': rwkv_gn_b[l], 'w_out': w_out[l],
            'ln2_g': ln2_g[l], 'router_w': router_w[l], 'router_b': router_b[l], 'w_gate_up': w_gate_up[l],
            'b_gate_up': b_gate_up[l], 'w_down': w_down[l], 'b_down': b_down[l],
        }
        hp, c1, s1, sh1, w1 = trunk_layer(hp, zero_conv, None, zero_shift, zero_wkv, lp)
        hs, c2, s2, sh2, w2 = trunk_layer(hs, state_ssm_conv[l], state_ssm[l], state_rwkv_shift[l],
                                          state_rwkv_wkv[l], lp)
        pc.append(c1); ps.append(s1); psh.append(sh1); pw.append(w1)
        sc.append(c2); ss.append(s2); ssh.append(sh2); sw.append(w2)
    y_prompt = rmsnorm(hp, lnf_g)[:, N_META:]
    y_sample = rmsnorm(hs, lnf_g)
    return (y_prompt, y_sample, jnp.stack(pc), jnp.stack(ps), jnp.stack(psh), jnp.stack(pw),
            jnp.stack(sc), jnp.stack(ss), jnp.stack(ssh), jnp.stack(sw))
```

```python
import functools
import math

import jax
import jax.numpy as jnp
from jax import lax
from jax.experimental import pallas as pl
from jax.experimental.pallas import tpu as pltpu

F32 = jnp.float32
BF16 = jnp.bfloat16

V7X_VMEM_LIMIT_BYTES = 56 * 1024 * 1024
LANES = 128
SUBLANES = 8

D_MODEL = 2048
N_META = 16
RMS_EPS = 1e-6
SSM_D_INNER = 2048
SSM_HEAD_DIM = 64
SSM_N_HEADS = 32
SSM_N_GROUPS = 4
SSM_HPG = 8
SSM_D_STATE = 128
SSM_CONV = 4
SSM_CHUNK = 128
SSM_CONV_DIM = SSM_D_INNER + 2 * SSM_N_GROUPS * SSM_D_STATE
SSM_NORM_EPS = 1e-5
RWKV_DIM = 2048
RWKV_HEAD = 64
RWKV_N_HEADS = 32
RWKV_LORA = 64
RWKV_GN_EPS = 64e-5
RWKV_SHIFT_DIM = 3 * RWKV_DIM + 2 * RWKV_LORA
RWKV_CHUNK = 64
N_EXPERTS = 32
TOP_K = 4
D_FF = 2048
SWIGLU_LIMIT = 7.0
SWIGLU_ALPHA = 1.702
MOE_BLOCK = 128


def _cparams(*sem):
    return pltpu.CompilerParams(dimension_semantics=sem, vmem_limit_bytes=V7X_VMEM_LIMIT_BYTES)


def _split3(x):
    hi = x.astype(BF16)
    r1 = x - hi.astype(F32)
    mid = r1.astype(BF16)
    lo = (r1 - mid.astype(F32)).astype(BF16)
    return hi, mid, lo


def _dot(a, b, dims=None):
    a = a.astype(BF16)
    b = b.astype(BF16)
    if dims is None:
        return jnp.dot(a, b, preferred_element_type=F32)
    return lax.dot_general(a, b, (dims, ((), ())), preferred_element_type=F32)


def _dot_exact_lhs(x, m01):
    hi, mid, lo = _split3(x)
    m = m01.astype(BF16)
    return (jnp.dot(hi, m, preferred_element_type=F32) + jnp.dot(mid, m, preferred_element_type=F32)
            + jnp.dot(lo, m, preferred_element_type=F32))


def _dot_f32(a, b):
    ah, am, al = _split3(a)
    bh, bm, bl = _split3(b)
    lhs = jnp.concatenate([ah, ah, am, ah, al, am], axis=1)
    rhs = jnp.concatenate([bh, bm, bh, bl, bh, bm], axis=0)
    return jnp.dot(lhs, rhs, preferred_element_type=F32)


def _softplus(x):
    return jnp.maximum(x, 0.0) + jnp.log(1.0 + jnp.exp(-jnp.abs(x)))


def _sigmoid(x):
    return 1.0 / (1.0 + jnp.exp(-x))


def _silu(x):
    return x * _sigmoid(x)


def _rmsnorm_kernel(x_ref, g_ref, o_ref, *, eps):
    x = x_ref[...]
    ms = jnp.mean(x * x, axis=-1, keepdims=True)
    o_ref[...] = (x * lax.rsqrt(ms + eps) * g_ref[...]).astype(o_ref.dtype)


def rmsnorm_rows(x, g, out_dtype, tm):
    rows, d = x.shape
    return pl.pallas_call(
        functools.partial(_rmsnorm_kernel, eps=RMS_EPS),
        out_shape=jax.ShapeDtypeStruct((rows, d), out_dtype),
        grid=(rows // tm,),
        in_specs=[pl.BlockSpec((tm, d), lambda i: (i, 0)), pl.BlockSpec((1, d), lambda i: (0, 0))],
        out_specs=pl.BlockSpec((tm, d), lambda i: (i, 0)),
        compiler_params=_cparams("parallel"),
        name="rmsnorm_rows",
    )(x, g.reshape(1, d))


def _matmul_kernel(x_ref, w_ref, o_ref):
    o_ref[...] = jnp.dot(x_ref[...], w_ref[...], preferred_element_type=F32)


def matmul_bf16(x, w, tm, tn, name):
    rows, k = x.shape
    n = w.shape[1]
    return pl.pallas_call(
        _matmul_kernel,
        out_shape=jax.ShapeDtypeStruct((rows, n), F32),
        grid=(n // tn, rows // tm),
        in_specs=[pl.BlockSpec((tm, k), lambda j, i: (i, 0)), pl.BlockSpec((k, tn), lambda j, i: (0, j))],
        out_specs=pl.BlockSpec((tm, tn), lambda j, i: (i, j)),
        compiler_params=_cparams("parallel", "parallel"),
        name=name,
    )(x, w)


def _ssd_kernel(xbc_ref, z_ref, dtr_ref, conv0_ref, s0_ref, convw_ref, convb_ref, dtb_ref, alog_ref,
                dskip_ref, ng_ref, expand_ref, y_ref, s_ref, buf_ref, *, q, t_valid, exact_state):
    c = pl.program_id(1)
    n_pad = buf_ref.shape[0] - q

    @pl.when(c == 0)
    def _():
        buf_ref[0:n_pad, :] = conv0_ref[0]
        s_ref[0] = s0_ref[0]

    buf_ref[n_pad:n_pad + q, :] = xbc_ref[0]
    acc = convb_ref[...] + convw_ref[SSM_CONV - 1:SSM_CONV, :] * buf_ref[n_pad:n_pad + q, :]
    for k in range(SSM_CONV - 1):
        off = n_pad - (SSM_CONV - 1) + k
        acc = acc + convw_ref[k:k + 1, :] * buf_ref[off:off + q, :]
    buf_ref[0:n_pad, :] = buf_ref[q:q + n_pad, :]
    xbc = _silu(acc)
    gn = SSM_N_GROUPS * SSM_D_STATE
    xs = xbc[:, :SSM_D_INNER]
    bm = xbc[:, SSM_D_INNER:SSM_D_INNER + gn]
    cm = xbc[:, SSM_D_INNER + gn:]

    row = lax.broadcasted_iota(jnp.int32, (q, LANES), 0) + c * q
    lane = lax.broadcasted_iota(jnp.int32, (q, LANES), 1)
    live = jnp.logical_and(row < t_valid, lane < SSM_N_HEADS)
    dt = jnp.where(live, _softplus(dtr_ref[0] + dtb_ref[...]), 0.0)
    da = dt * (-jnp.exp(alog_ref[...]))

    ti = lax.broadcasted_iota(jnp.int32, (q, q), 0)
    si = lax.broadcasted_iota(jnp.int32, (q, q), 1)
    causal = si <= ti
    tril = causal.astype(BF16)
    dh, dm, dl = _split3(da)
    cs = (jnp.dot(tril, dh, preferred_element_type=F32) + jnp.dot(tril, dm, preferred_element_type=F32)
          + jnp.dot(tril, dl, preferred_element_type=F32))
    eye = (lax.broadcasted_iota(jnp.int32, (LANES, LANES), 0)
           == lax.broadcasted_iota(jnp.int32, (LANES, LANES), 1)).astype(BF16)
    ch, cmid, cl = _split3(cs)
    nt = (((1,), (1,)), ((), ()))
    cs_t = (lax.dot_general(eye, ch, nt, preferred_element_type=F32)
            + lax.dot_general(eye, cmid, nt, preferred_element_type=F32)
            + lax.dot_general(eye, cl, nt, preferred_element_type=F32))

    expand = expand_ref[...]
    dt_x = _dot_exact_lhs(dt, expand)
    cs_x = _dot_exact_lhs(cs, expand)
    xdt = xs * dt_x
    ecs = jnp.exp(cs_x)
    dte = jnp.exp(cs_x[q - 1:q, :] - cs_x)

    lane_in_pair = lax.broadcasted_iota(jnp.int32, (q, LANES), 1)
    lo_half = lane_in_pair < SSM_HEAD_DIM

    y_groups = []
    for g in range(SSM_N_GROUPS):
        b_g = bm[:, g * SSM_D_STATE:(g + 1) * SSM_D_STATE]
        c_g = cm[:, g * SSM_D_STATE:(g + 1) * SSM_D_STATE]
        cb = _dot(c_g, b_g, ((1,), (1,)))
        gsl = slice(g * SSM_HPG * SSM_HEAD_DIM, (g + 1) * SSM_HPG * SSM_HEAD_DIM)
        s_g = s_ref[0, g * SSM_HPG:(g + 1) * SSM_HPG].reshape(SSM_HPG * SSM_HEAD_DIM, SSM_D_STATE)
        y_off = _dot(c_g, s_g, ((1,), (1,))) * ecs[:, gsl]
        slabs = []
        for m in range(SSM_HPG // 2):
            lsl = slice(gsl.start + m * LANES, gsl.start + (m + 1) * LANES)
            x_pair = xdt[:, lsl]
            acc_pair = None
            for half in range(2):
                h = g * SSM_HPG + 2 * m + half
                seg = cs[:, h:h + 1] - cs_t[h:h + 1, :]
                lmat = jnp.where(causal, jnp.exp(jnp.where(causal, seg, 0.0)), 0.0)
                x_half = jnp.where(lo_half if half == 0 else jnp.logical_not(lo_half), x_pair, 0.0)
                contrib = _dot(cb * lmat, x_half)
                acc_pair = contrib if acc_pair is None else acc_pair + contrib
            slabs.append(acc_pair)
        y_groups.append(jnp.concatenate(slabs, axis=1) + y_off)

        xw = xdt[:, gsl] * dte[:, gsl]
        tn = ((0,), (0,))
        if exact_state:
            xh, xm, xl = _split3(xw)
            bh, bmid, bl = _split3(b_g)
            upd = _dot(jnp.concatenate([xh, xh, xm, xh, xl, xm], axis=0),
                       jnp.concatenate([bh, bmid, bh, bl, bh, bmid], axis=0), tn)
        else:
            upd = _dot(xw, b_g, tn)
        for j in range(SSM_HPG):
            h = g * SSM_HPG + j
            dec = jnp.exp(cs_t[h:h + 1, q - 1:q])
            s_ref[0, h] = s_ref[0, h] * dec + upd[j * SSM_HEAD_DIM:(j + 1) * SSM_HEAD_DIM, :]

    y = jnp.concatenate(y_groups, axis=1) + dskip_ref[...] * xs
    y = y * _silu(z_ref[0])
    outs = []
    gw = SSM_D_INNER // SSM_N_GROUPS
    for g in range(SSM_N_GROUPS):
        yg = y[:, g * gw:(g + 1) * gw]
        outs.append(yg * lax.rsqrt(jnp.mean(yg * yg, axis=-1, keepdims=True) + SSM_NORM_EPS))
    y_ref[0] = jnp.concatenate(outs, axis=1) * ng_ref[...]


def ssd_branch(xbc, z, dtr, conv0, s0, conv_w, conv_b, dt_bias, a_log, d_skip, norm_g, *, q, t_valid,
               exact_state, name):
    nb, t, _ = xbc.shape
    n_pad = conv0.shape[1]
    pad_l = lambda v: jnp.pad(v.astype(F32), (0, LANES - v.shape[0])).reshape(1, LANES)
    expand = (jnp.arange(LANES)[:, None] == (jnp.arange(SSM_D_INNER) // SSM_HEAD_DIM)[None, :]).astype(BF16)
    row2 = lambda v: v.reshape(1, -1)
    full = lambda a: pl.BlockSpec(a.shape, lambda b, c: (0,) * a.ndim)
    consts = [conv_w, row2(conv_b), pad_l(dt_bias), pad_l(a_log), row2(jnp.repeat(d_skip, SSM_HEAD_DIM)),
              row2(norm_g), expand]
    kern = functools.partial(_ssd_kernel, q=q, t_valid=t_valid, exact_state=exact_state)
    return pl.pallas_call(
        kern,
        out_shape=(jax.ShapeDtypeStruct((nb, t, SSM_D_INNER), F32),
                   jax.ShapeDtypeStruct(s0.shape, F32)),
        grid=(nb, t // q),
        in_specs=[pl.BlockSpec((1, q, SSM_CONV_DIM), lambda b, c: (b, c, 0)),
                  pl.BlockSpec((1, q, SSM_D_INNER), lambda b, c: (b, c, 0)),
                  pl.BlockSpec((1, q, LANES), lambda b, c: (b, c, 0)),
                  pl.BlockSpec((1, n_pad, SSM_CONV_DIM), lambda b, c: (b, 0, 0)),
                  pl.BlockSpec((1,) + s0.shape[1:], lambda b, c: (b, 0, 0, 0))] + [full(a) for a in consts],
        out_specs=(pl.BlockSpec((1, q, SSM_D_INNER), lambda b, c: (b, c, 0)),
                   pl.BlockSpec((1,) + s0.shape[1:], lambda b, c: (b, 0, 0, 0))),
        scratch_shapes=[pltpu.VMEM((q + n_pad, SSM_CONV_DIM), F32)],
        compiler_params=_cparams("parallel", "arbitrary"),
        name=name,
    )(xbc, z, dtr, conv0, s0, *consts)


RWKV_GROUP_HEADS = 4
RWKV_GROUP_LANES = RWKV_GROUP_HEADS * RWKV_HEAD


def _rep_rows(x, n):
    return jnp.concatenate([x] * n, axis=0)


def _rwkv_kernel(rw_ref, shift0_ref, wkv0_ref, mu_ref, w0_ref, w2_ref, a0_ref, a2_ref, kk_ref, ka_ref, rk_ref,
                 gng_ref, gnb_ref, hred_ref, hexp_ref, tile4_ref, tile4t_ref,
                 y_ref, wkv_ref, buf_ref, s_ref, *, cq, t_valid, single_token, exact_state):
    c = pl.program_id(1)
    n_chunks = pl.num_programs(1)
    n_pad = buf_ref.shape[0] - cq
    gl = RWKV_GROUP_LANES
    gh = RWKV_GROUP_HEADS
    n_groups = RWKV_DIM // gl
    nt = ((1,), (1,))
    tn = ((0,), (0,))

    def head_mask(rows_per_head, cols_per_head, n_rows, n_cols):
        rr = lax.broadcasted_iota(jnp.int32, (n_rows, n_cols), 0) // rows_per_head
        cc = lax.broadcasted_iota(jnp.int32, (n_rows, n_cols), 1) // cols_per_head
        return rr == cc

    m_state = head_mask(RWKV_HEAD, RWKV_HEAD, gl, gl)
    m_exp = head_mask(cq, RWKV_HEAD, gh * cq, gl)
    m_nbd = head_mask(cq, cq, gh * cq, gh * cq)

    @pl.when(c == 0)
    def _():
        buf_ref[0:n_pad, :] = shift0_ref[0]
        for g in range(n_groups):
            stack = wkv0_ref[0, g * gl:(g + 1) * gl, :]
            s_ref[g] = jnp.where(m_state, _dot_exact_lhs(stack, tile4_ref[...]), 0.0)

    buf_ref[n_pad:n_pad + cq, :] = rw_ref[0]
    cur = buf_ref[n_pad:n_pad + cq, :]
    prev = buf_ref[n_pad - 1:n_pad - 1 + cq, :]
    buf_ref[0:n_pad, :] = buf_ref[cq:cq + n_pad, :]
    mixed = cur + (prev - cur) * mu_ref[...]
    r = mixed[:, 0:RWKV_DIM]
    k = mixed[:, RWKV_DIM:2 * RWKV_DIM]
    v = mixed[:, 2 * RWKV_DIM:3 * RWKV_DIM]
    lora = mixed[:, 3 * RWKV_DIM:]
    lane = lax.broadcasted_iota(jnp.int32, (cq, LANES), 1)
    wl = jnp.where(lane < RWKV_LORA, jnp.tanh(lora), 0.0)
    al = jnp.where(lane >= RWKV_LORA, lora, 0.0)
    w_log = -_softplus(-(w0_ref[...] + _dot_f32(wl, w2_ref[...]))) - 0.5
    a = _sigmoid(a0_ref[...] + _dot_f32(al, a2_ref[...]))

    row = lax.broadcasted_iota(jnp.int32, (cq, 1), 0) + c * cq
    live = row < t_valid
    logw = jnp.where(live, -jnp.exp(w_log), 0.0)
    kk_raw = k * kk_ref[...]
    ss = _dot_exact_lhs(kk_raw * kk_raw, hred_ref[...])
    inv_n = 1.0 / jnp.maximum(jnp.sqrt(ss), 1e-12)
    kk = jnp.where(live, kk_raw * _dot_exact_lhs(inv_n, hexp_ref[...]), 0.0)
    k2 = jnp.where(live, k * (1.0 + (a - 1.0) * ka_ref[...]), 0.0)
    v = jnp.where(live, v, 0.0)

    ti = lax.broadcasted_iota(jnp.int32, (cq, cq), 0)
    si = lax.broadcasted_iota(jnp.int32, (cq, cq), 1)
    tril = (si <= ti).astype(BF16)
    lh, lm, ll = _split3(logw)
    cl = (jnp.dot(tril, lh, preferred_element_type=F32) + jnp.dot(tril, lm, preferred_element_type=F32)
          + jnp.dot(tril, ll, preferred_element_type=F32))
    p_incl = jnp.exp(cl)
    p_inv = jnp.exp(-cl)
    a_t = jnp.exp(cl - logw) * (-kk)
    r_t = p_incl * r
    b_t = p_inv * (kk * a)
    k_t = p_inv * k2
    p_end = p_incl[cq - 1:cq, :]

    tt = lax.broadcasted_iota(jnp.int32, (cq, gh * cq), 0)
    ss_i = lax.broadcasted_iota(jnp.int32, (cq, gh * cq), 1) % cq
    strict = ss_i < tt
    incl = ss_i <= tt

    def expand_rows(x):
        return jnp.where(m_exp, _rep_rows(x, gh), 0.0)

    y_groups = []
    for g in range(n_groups):
        sl = slice(g * gl, (g + 1) * gl)
        lhs_ar = jnp.concatenate([a_t[:, sl], r_t[:, sl]], axis=0)
        b_exp = expand_rows(b_t[:, sl])
        k_exp = expand_rows(k_t[:, sl])
        v_g = v[:, sl]
        v_exp = expand_rows(v_g)
        s_bd = s_ref[g]
        ab = _dot(lhs_ar, b_exp, nt)
        ak = _dot(lhs_ar, k_exp, nt)
        w_s = _dot(lhs_ar, s_bd, nt)
        x = w_s[:cq] + _dot(jnp.where(strict, ak[:cq], 0.0), v_exp)
        if not single_token:
            n_cat = jnp.where(strict, ab[:cq], 0.0)
            n_steps = int(math.log2(cq))
            for step in range(n_steps):
                x = x + _dot(n_cat, expand_rows(x))
                if step + 1 < n_steps:
                    n_bd = jnp.where(m_nbd, _rep_rows(n_cat, gh), 0.0)
                    n_cat = _dot(n_cat, n_bd)
        u = x
        y_g = (w_s[cq:] + _dot(jnp.where(incl, ab[cq:], 0.0), expand_rows(u))
               + _dot(jnp.where(incl, ak[cq:], 0.0), v_exp))
        y_groups.append(y_g)

        uv = jnp.concatenate([u, v_g], axis=0)
        bk = jnp.concatenate([b_t[:, sl], k_t[:, sl]], axis=0)
        if exact_state:
            uh, um, ul = _split3(uv)
            bh, bmid, bl = _split3(bk)
            inc = _dot(jnp.concatenate([uh, uh, um, uh, ul, um], axis=0),
                       jnp.concatenate([bh, bmid, bh, bl, bh, bmid], axis=0), tn)
        else:
            inc = _dot(uv, bk, tn)
        s_ref[g] = jnp.where(m_state, (s_bd + inc) * p_end[:, sl], 0.0)

    y = jnp.concatenate(y_groups, axis=1)
    inv_h = 1.0 / RWKV_HEAD
    mean = _dot_exact_lhs(_dot_exact_lhs(y, hred_ref[...]) * inv_h, hexp_ref[...])
    yc = y - mean
    var = _dot_exact_lhs(yc * yc, hred_ref[...]) * inv_h
    rstd = _dot_exact_lhs(lax.rsqrt(var + RWKV_GN_EPS), hexp_ref[...])
    bonus = _dot_exact_lhs(_dot_exact_lhs(r * k2 * rk_ref[...], hred_ref[...]), hexp_ref[...]) * v
    y_ref[0] = yc * rstd * gng_ref[...] + gnb_ref[...] + bonus

    @pl.when(c == n_chunks - 1)
    def _():
        for g in range(n_groups):
            wkv_ref[0, g * gl:(g + 1) * gl, :] = _dot_exact_lhs(s_ref[g], tile4t_ref[...])


def rwkv_branch(rw, shift0, wkv0, mu, w0, w2, a0, a2, k_k, k_a, r_k, gn_g, gn_b, *, cq, t_valid, single_token,
                exact_state, name):
    nb, t, _ = rw.shape
    n_pad = shift0.shape[1]
    row2 = lambda a: a.reshape(1, -1).astype(F32)
    zeros_l = jnp.zeros((RWKV_LORA, RWKV_DIM), F32)
    head_of_lane = jnp.arange(RWKV_DIM) // RWKV_HEAD
    hexp = (jnp.arange(LANES)[:, None] == head_of_lane[None, :]).astype(BF16)
    hred = hexp.T
    tile4 = (jnp.arange(RWKV_HEAD)[:, None] == (jnp.arange(RWKV_GROUP_LANES) % RWKV_HEAD)[None, :]).astype(BF16)
    consts = [row2(mu), row2(w0), jnp.concatenate([w2, zeros_l], axis=0), row2(a0),
              jnp.concatenate([zeros_l, a2], axis=0), row2(k_k), row2(k_a), row2(r_k), row2(gn_g), row2(gn_b),
              hred, hexp, tile4, tile4.T]
    full = lambda a: pl.BlockSpec(a.shape, lambda b, c: (0,) * a.ndim)
    wkv_rows = wkv0.reshape(nb, RWKV_DIM, RWKV_HEAD)
    kern = functools.partial(_rwkv_kernel, cq=cq, t_valid=t_valid, single_token=single_token,
                             exact_state=exact_state)
    y, wkv = pl.pallas_call(
        kern,
        out_shape=(jax.ShapeDtypeStruct((nb, t, RWKV_DIM), F32),
                   jax.ShapeDtypeStruct(wkv_rows.shape, F32)),
        grid=(nb, t // cq),
        in_specs=[pl.BlockSpec((1, cq, RWKV_SHIFT_DIM), lambda b, c: (b, c, 0)),
                  pl.BlockSpec((1, n_pad, RWKV_SHIFT_DIM), lambda b, c: (b, 0, 0)),
                  pl.BlockSpec((1, RWKV_DIM, RWKV_HEAD), lambda b, c: (b, 0, 0))] + [full(a) for a in consts],
        out_specs=(pl.BlockSpec((1, cq, RWKV_DIM), lambda b, c: (b, c, 0)),
                   pl.BlockSpec((1, RWKV_DIM, RWKV_HEAD), lambda b, c: (b, 0, 0))),
        scratch_shapes=[pltpu.VMEM((cq + n_pad, RWKV_SHIFT_DIM), F32),
                        pltpu.VMEM((RWKV_DIM // RWKV_GROUP_LANES, RWKV_GROUP_LANES, RWKV_GROUP_LANES), F32)],
        compiler_params=_cparams("parallel", "arbitrary"),
        name=name,
    )(rw, shift0, wkv_rows, *consts)
    return y, wkv.reshape(wkv0.shape)


NEG_BIG = -1e30


def _mix_kernel(g_ref, yap_ref, ybp_ref, yas_ref, ybs_ref, h_ref, wout_ref, ln2_ref, rw_ref, rb_ref,
                h1_ref, xn_ref, idx_ref, gate_ref):
    i = pl.program_id(0)
    last = i == pl.num_programs(0) - 1
    ya = jnp.where(last, yas_ref[...], yap_ref[...])
    yb = jnp.where(last, ybs_ref[...], ybp_ref[...])
    g = g_ref[...]
    merged = _sigmoid(g[:, :D_MODEL]) * ya + _sigmoid(g[:, D_MODEL:]) * yb
    h1 = h_ref[...] + jnp.dot(merged.astype(BF16), wout_ref[...], preferred_element_type=F32)
    h1_ref[...] = h1
    ms = jnp.mean(h1 * h1, axis=-1, keepdims=True)
    xn = h1 * lax.rsqrt(ms + RMS_EPS) * ln2_ref[...]
    xn_ref[...] = xn
    logits = _dot_f32(xn, rw_ref[...]) + rb_ref[...]
    lane = lax.broadcasted_iota(jnp.int32, logits.shape, 1).astype(F32)
    idx_out = jnp.zeros(logits.shape, F32)
    val_out = jnp.zeros(logits.shape, F32)
    top0 = None
    for kth in range(TOP_K):
        m = jnp.max(logits, axis=-1, keepdims=True)
        sel = jnp.min(jnp.where(logits == m, lane, float(LANES)), axis=-1, keepdims=True)
        if kth == 0:
            top0 = m
        idx_out = jnp.where(lane == kth, sel, idx_out)
        val_out = jnp.where(lane == kth, jnp.exp(m - top0), val_out)
        logits = jnp.where(lane == sel, -jnp.inf, logits)
    idx_ref[...] = idx_out.astype(jnp.int32)
    gate_ref[...] = val_out / jnp.sum(val_out, axis=-1, keepdims=True)


def mix_and_route(gates, ya_p, yb_p, ya_s, yb_s, h_rows, w_out_bf, ln2_g, router_w, router_b, tm):
    rows = h_rows.shape[0]
    n_tiles = rows // tm
    last_p = ya_p.shape[0] // tm - 1
    rw_pad = jnp.pad(router_w, ((0, 0), (0, LANES - N_EXPERTS)))
    rb_pad = jnp.pad(router_b, (0, LANES - N_EXPERTS), constant_values=NEG_BIG).reshape(1, LANES)
    row_spec = lambda n: pl.BlockSpec((tm, n), lambda i: (i, 0))
    prompt_spec = pl.BlockSpec((tm, D_MODEL), lambda i: (jnp.minimum(i, last_p), 0))
    fixed = lambda a: pl.BlockSpec(a.shape, lambda i: (0,) * a.ndim)
    return pl.pallas_call(
        _mix_kernel,
        out_shape=(jax.ShapeDtypeStruct((rows, D_MODEL), F32), jax.ShapeDtypeStruct((rows, D_MODEL), F32),
                   jax.ShapeDtypeStruct((rows, LANES), jnp.int32), jax.ShapeDtypeStruct((rows, LANES), F32)),
        grid=(n_tiles,),
        in_specs=[row_spec(2 * D_MODEL), prompt_spec, prompt_spec, fixed(ya_s), fixed(yb_s), row_spec(D_MODEL),
                  fixed(w_out_bf), pl.BlockSpec((1, D_MODEL), lambda i: (0, 0)), fixed(rw_pad), fixed(rb_pad)],
        out_specs=(row_spec(D_MODEL), row_spec(D_MODEL), row_spec(LANES), row_spec(LANES)),
        compiler_params=_cparams("parallel"),
        name="mix_and_route",
    )(gates, ya_p, yb_p, ya_s, yb_s, h_rows, w_out_bf, ln2_g.reshape(1, D_MODEL), rw_pad, rb_pad)


def _gather_rows_kernel(tok_ref, x_hbm, o_ref, rows_ref, sem):
    def copy(r):
        return pltpu.make_async_copy(x_hbm.at[pl.ds(tok_ref[0, 0, r], 1), :], rows_ref.at[pl.ds(r, 1), :], sem)

    def start(r, carry):
        copy(r).start()
        return carry

    def wait(r, carry):
        copy(r).wait()
        return carry

    lax.fori_loop(0, MOE_BLOCK, start, 0)
    lax.fori_loop(0, MOE_BLOCK, wait, 0)
    o_ref[...] = rows_ref[...].astype(o_ref.dtype)


def gather_rows(x, slot_tok, out_dtype):
    n_blocks = slot_tok.shape[0] // MOE_BLOCK
    d = x.shape[1]
    return pl.pallas_call(
        _gather_rows_kernel,
        out_shape=jax.ShapeDtypeStruct((n_blocks * MOE_BLOCK, d), out_dtype),
        grid=(n_blocks,),
        in_specs=[pl.BlockSpec((1, 1, MOE_BLOCK), lambda i: (i, 0, 0), memory_space=pltpu.SMEM),
                  pl.BlockSpec(memory_space=pl.ANY)],
        out_specs=pl.BlockSpec((MOE_BLOCK, d), lambda i: (i, 0)),
        scratch_shapes=[pltpu.VMEM((MOE_BLOCK, d), x.dtype), pltpu.SemaphoreType.DMA(())],
        compiler_params=_cparams("arbitrary"),
        name="moe_gather",
    )(slot_tok.reshape(n_blocks, 1, MOE_BLOCK), x)


def _expert_changed(be_ref, i):
    prev = be_ref[jnp.maximum(i, 1) - 1]
    return jnp.logical_or(i == 0, be_ref[i] != prev)


def _moe_up_kernel(be_ref, nb_ref, x_ref, wg_ref, wu_ref, bg_ref, bu_ref, o_ref, wg_bf, wu_bf):
    i = pl.program_id(1)

    @pl.when(_expert_changed(be_ref, i))
    def _():
        wg_bf[...] = wg_ref[0].astype(BF16)
        wu_bf[...] = wu_ref[0].astype(BF16)

    @pl.when(i < nb_ref[0])
    def _():
        x = x_ref[...]
        g = jnp.dot(x, wg_bf[...], preferred_element_type=F32) + bg_ref[0]
        u = jnp.dot(x, wu_bf[...], preferred_element_type=F32) + bu_ref[0]
        g = jnp.minimum(g, SWIGLU_LIMIT)
        u = jnp.clip(u, -SWIGLU_LIMIT, SWIGLU_LIMIT)
        o_ref[...] = ((u + 1.0) * (g * _sigmoid(g * SWIGLU_ALPHA))).astype(o_ref.dtype)

    @pl.when(i >= nb_ref[0])
    def _():
        o_ref[...] = jnp.zeros_like(o_ref)


def moe_up(xb, block_e, n_used, w_gate_up, b_gate_up, tn):
    n_blocks = xb.shape[0] // MOE_BLOCK
    n_ff_tiles = D_FF // tn
    b3 = b_gate_up.reshape(N_EXPERTS, 1, 2 * D_FF)
    grid_spec = pltpu.PrefetchScalarGridSpec(
        num_scalar_prefetch=2,
        grid=(n_ff_tiles, n_blocks),
        in_specs=[pl.BlockSpec((MOE_BLOCK, D_MODEL), lambda j, i, be, nb: (i, 0)),
                  pl.BlockSpec((1, D_MODEL, tn), lambda j, i, be, nb: (be[i], 0, j)),
                  pl.BlockSpec((1, D_MODEL, tn), lambda j, i, be, nb: (be[i], 0, j + n_ff_tiles)),
                  pl.BlockSpec((1, 1, tn), lambda j, i, be, nb: (be[i], 0, j)),
                  pl.BlockSpec((1, 1, tn), lambda j, i, be, nb: (be[i], 0, j + n_ff_tiles))],
        out_specs=pl.BlockSpec((MOE_BLOCK, tn), lambda j, i, be, nb: (i, j)),
        scratch_shapes=[pltpu.VMEM((D_MODEL, tn), BF16), pltpu.VMEM((D_MODEL, tn), BF16)],
    )
    return pl.pallas_call(
        _moe_up_kernel,
        out_shape=jax.ShapeDtypeStruct((n_blocks * MOE_BLOCK, D_FF), BF16),
        grid_spec=grid_spec,
        compiler_params=_cparams("arbitrary", "arbitrary"),
        name="moe_up",
    )(block_e, n_used, xb, w_gate_up, w_gate_up, b3, b3)


def _moe_down_kernel(be_ref, nb_ref, h_ref, wd_ref, bd_ref, o_ref, wd_bf):
    i = pl.program_id(1)

    @pl.when(_expert_changed(be_ref, i))
    def _():
        wd_bf[...] = wd_ref[0].astype(BF16)

    @pl.when(i < nb_ref[0])
    def _():
        o_ref[...] = jnp.dot(h_ref[...], wd_bf[...], preferred_element_type=F32) + bd_ref[0]

    @pl.when(i >= nb_ref[0])
    def _():
        o_ref[...] = jnp.zeros_like(o_ref)


def moe_down(hb, block_e, n_used, w_down, b_down, tn):
    n_blocks = hb.shape[0] // MOE_BLOCK
    b3 = b_down.reshape(N_EXPERTS, 1, D_MODEL)
    grid_spec = pltpu.PrefetchScalarGridSpec(
        num_scalar_prefetch=2,
        grid=(D_MODEL // tn, n_blocks),
        in_specs=[pl.BlockSpec((MOE_BLOCK, D_FF), lambda j, i, be, nb: (i, 0)),
                  pl.BlockSpec((1, D_FF, tn), lambda j, i, be, nb: (be[i], 0, j)),
                  pl.BlockSpec((1, 1, tn), lambda j, i, be, nb: (be[i], 0, j))],
        out_specs=pl.BlockSpec((MOE_BLOCK, tn), lambda j, i, be, nb: (i, j)),
        scratch_shapes=[pltpu.VMEM((D_FF, tn), BF16)],
    )
    return pl.pallas_call(
        _moe_down_kernel,
        out_shape=jax.ShapeDtypeStruct((n_blocks * MOE_BLOCK, D_MODEL), F32),
        grid_spec=grid_spec,
        compiler_params=_cparams("arbitrary", "arbitrary"),
        name="moe_down",
    )(block_e, n_used, hb, w_down, b_down.reshape(N_EXPERTS, 1, D_MODEL))


def _combine_kernel(slot_ref, row0_ref, gate_ref, lnf_ref, yb_hbm, h_hbm, o_ref, rows_ref, h_vmem, sem, hsem):
    i = pl.program_id(0)
    tm = o_ref.shape[-2]
    h_copy = pltpu.make_async_copy(h_hbm.at[pl.ds(pl.multiple_of(row0_ref[i], SUBLANES), tm), :], h_vmem, hsem)
    h_copy.start()

    def copy(n):
        k = lax.shift_right_logical(n, int(math.log2(tm)))
        r = jnp.bitwise_and(n, tm - 1)
        return pltpu.make_async_copy(yb_hbm.at[pl.ds(slot_ref[0, 0, n], 1), :],
                                     rows_ref.at[k, pl.ds(r, 1), :], sem)

    def start(n, carry):
        copy(n).start()
        return carry

    def wait(n, carry):
        copy(n).wait()
        return carry

    lax.fori_loop(0, TOP_K * tm, start, 0)
    lax.fori_loop(0, TOP_K * tm, wait, 0)
    h_copy.wait()
    gate = gate_ref[...]
    acc = h_vmem[...]
    for k in range(TOP_K):
        acc = acc + rows_ref[k] * gate[:, k:k + 1]
    ms = jnp.mean(acc * acc, axis=-1, keepdims=True)
    o_ref[...] = (acc * lax.rsqrt(ms + RMS_EPS) * lnf_ref[...]).reshape(o_ref.shape)


def moe_combine(yb, h1, slot_of_pair, gates, row0, lnf_g, out_shape, out_index_map, tm):
    n_tiles = row0.shape[0]
    grid_spec = pltpu.PrefetchScalarGridSpec(
        num_scalar_prefetch=0,
        grid=(n_tiles,),
        in_specs=[pl.BlockSpec((1, 1, TOP_K * tm), lambda i: (i, 0, 0), memory_space=pltpu.SMEM),
                  pl.BlockSpec(memory_space=pltpu.SMEM),
                  pl.BlockSpec((tm, LANES), lambda i: (i, 0)),
                  pl.BlockSpec((1, D_MODEL), lambda i: (0, 0)),
                  pl.BlockSpec(memory_space=pl.ANY),
                  pl.BlockSpec(memory_space=pl.ANY)],
        out_specs=pl.BlockSpec(out_shape[0], out_index_map),
        scratch_shapes=[pltpu.VMEM((TOP_K, tm, D_MODEL), F32), pltpu.VMEM((tm, D_MODEL), F32),
                        pltpu.SemaphoreType.DMA(()), pltpu.SemaphoreType.DMA(())],
    )
    return pl.pallas_call(
        _combine_kernel,
        out_shape=jax.ShapeDtypeStruct(out_shape[1], F32),
        grid_spec=grid_spec,
        compiler_params=_cparams("arbitrary"),
        name="moe_combine",
    )(slot_of_pair, row0, gates, lnf_g.reshape(1, D_MODEL), yb, h1)


PROMPT_PAD_T = 2176
SAMPLE_PAD_T = SUBLANES
ROW_TILE = 1104


def kernel(x_prompt, x_sample, state_ssm_conv, state_ssm, state_rwkv_shift, state_rwkv_wkv, meta_tokens, ln1_g, w_in, ssm_conv_w, ssm_conv_b, ssm_dt_bias, ssm_A_log, ssm_D, ssm_norm_g, rwkv_mu, rwkv_w0, rwkv_w2, rwkv_a0, rwkv_a2, rwkv_k_k, rwkv_k_a, rwkv_r_k, rwkv_gn_g, rwkv_gn_b, w_out, ln2_g, router_w, router_b, w_gate_up, b_gate_up, w_down, b_down, lnf_g):
    bp, seq, d = x_prompt.shape
    bs = x_sample.shape[0]
    t_prompt = N_META + seq
    meta = jnp.broadcast_to(meta_tokens[None], (bp, N_META, d))
    hp = jnp.concatenate([meta, x_prompt, jnp.zeros((bp, PROMPT_PAD_T - t_prompt, d), F32)], axis=1)
    h_rows = jnp.concatenate([hp.reshape(bp * PROMPT_PAD_T, d), x_sample.reshape(bs, d)], axis=0)
    n_prow = bp * PROMPT_PAD_T

    l = 0
    xn = rmsnorm_rows(h_rows, ln1_g[l], BF16, ROW_TILE)
    w = w_in[l]
    o_xbc = SSM_D_INNER
    o_dt = o_xbc + SSM_CONV_DIM
    o_rw = o_dt + SSM_N_HEADS
    o_g = o_rw + RWKV_SHIFT_DIM
    w_z = w[:, :o_xbc].astype(BF16)
    w_xbc = w[:, o_xbc:o_dt].astype(BF16)
    w_dt = jnp.pad(w[:, o_dt:o_rw], ((0, 0), (0, LANES - SSM_N_HEADS))).astype(BF16)
    w_rw = w[:, o_rw:o_g].astype(BF16)
    w_g = w[:, o_g:].astype(BF16)
    z = matmul_bf16(xn, w_z, ROW_TILE, 1024, "proj_z")
    xbc = matmul_bf16(xn, w_xbc, ROW_TILE, 1024, "proj_xbc")
    dtr = matmul_bf16(xn, w_dt, ROW_TILE, LANES, "proj_dt")

    def split_rows(a):
        n = a.shape[1]
        a_s = jnp.pad(a[n_prow:].reshape(bs, 1, n), ((0, 0), (0, SAMPLE_PAD_T - 1), (0, 0)))
        return a[:n_prow].reshape(bp, PROMPT_PAD_T, n), a_s

    z_p, z_s = split_rows(z)
    xbc_p, xbc_s = split_rows(xbc)
    dtr_p, dtr_s = split_rows(dtr)
    ssm_args = (ssm_conv_w[l], ssm_conv_b[l], ssm_dt_bias[l], ssm_A_log[l], ssm_D[l], ssm_norm_g[l])
    conv0_p = jnp.zeros((bp, SUBLANES, SSM_CONV_DIM), F32)
    s0_p = jnp.zeros((bp, SSM_N_HEADS, SSM_HEAD_DIM, SSM_D_STATE), F32)
    ya_p, ssm_p = ssd_branch(xbc_p, z_p, dtr_p, conv0_p, s0_p, *ssm_args, q=SSM_CHUNK, t_valid=t_prompt,
                             exact_state=False, name="ssd_prompt")
    conv0_s = jnp.pad(state_ssm_conv[l], ((0, 0), (SUBLANES - (SSM_CONV - 1), 0), (0, 0)))
    ya_s, ssm_s = ssd_branch(xbc_s, z_s, dtr_s, conv0_s, state_ssm[l], *ssm_args, q=SAMPLE_PAD_T, t_valid=1,
                             exact_state=True, name="ssd_sample")
    rw = matmul_bf16(xn, w_rw, ROW_TILE, 896, "proj_rw")
    rw_p, rw_s = split_rows(rw)
    rwkv_args = (rwkv_mu[l], rwkv_w0[l], rwkv_w2[l], rwkv_a0[l], rwkv_a2[l], rwkv_k_k[l], rwkv_k_a[l],
                 rwkv_r_k[l], rwkv_gn_g[l], rwkv_gn_b[l])
    shift0_p = jnp.zeros((bp, SUBLANES, RWKV_SHIFT_DIM), F32)
    wkv0_p = jnp.zeros((bp, RWKV_N_HEADS, RWKV_HEAD, RWKV_HEAD), F32)
    yb_p, wkv_p = rwkv_branch(rw_p, shift0_p, wkv0_p, *rwkv_args, cq=RWKV_CHUNK, t_valid=t_prompt,
                              single_token=False, exact_state=False, name="rwkv_prompt")
    shift0_s = jnp.pad(state_rwkv_shift[l][:, None], ((0, 0), (SUBLANES - 1, 0), (0, 0)))
    yb_s, wkv_s = rwkv_branch(rw_s, shift0_s, state_rwkv_wkv[l], *rwkv_args, cq=SAMPLE_PAD_T, t_valid=1,
                              single_token=True, exact_state=True, name="rwkv_sample")
    gates = matmul_bf16(xn, w_g, ROW_TILE, 1024, "proj_gates")
    h1, xn2, top_idx, top_gate = mix_and_route(
        gates, ya_p.reshape(n_prow, d), yb_p.reshape(n_prow, d), ya_s[:, 0], yb_s[:, 0], h_rows,
        w_out[l].astype(BF16), ln2_g[l], router_w[l], router_b[l], MOE_BLOCK)

    n_rows = h_rows.shape[0]
    row_id = jnp.arange(n_rows, dtype=jnp.int32)
    valid = jnp.logical_or(row_id >= n_prow, row_id % PROMPT_PAD_T < t_prompt)
    key = jnp.where(valid[:, None], top_idx[:, :TOP_K], N_EXPERTS).reshape(-1)
    n_pairs = key.shape[0]
    n_valid_pairs = (bp * t_prompt + bs) * TOP_K
    n_blocks = -(-n_valid_pairs // MOE_BLOCK) + N_EXPERTS
    order = jnp.argsort(key, stable=True).astype(jnp.int32)
    counts = jnp.sum((key[:, None] == jnp.arange(N_EXPERTS, dtype=jnp.int32)[None, :]).astype(jnp.int32), axis=0)
    padded = (counts + MOE_BLOCK - 1) // MOE_BLOCK * MOE_BLOCK
    ends = jnp.cumsum(padded)
    pstart = ends - padded
    start = jnp.cumsum(counts) - counts
    inv = jnp.zeros((n_pairs,), jnp.int32).at[order].set(jnp.arange(n_pairs, dtype=jnp.int32), unique_indices=True)
    e_clip = jnp.minimum(key, N_EXPERTS - 1)
    slot_of_pair = jnp.where(key < N_EXPERTS, pstart[e_clip] + inv - start[e_clip], 0).reshape(n_rows, TOP_K)
    block_e = jnp.minimum(jnp.searchsorted(ends, jnp.arange(n_blocks, dtype=jnp.int32) * MOE_BLOCK, side='right'),
                          N_EXPERTS - 1).astype(jnp.int32)
    slot_id = jnp.arange(n_blocks * MOE_BLOCK, dtype=jnp.int32)
    e_slot = block_e[slot_id // MOE_BLOCK]
    rank = slot_id - pstart[e_slot]
    in_use = jnp.logical_and(rank >= 0, rank < counts[e_slot])
    src = jnp.clip(start[e_slot] + rank, 0, n_pairs - 1)
    slot_tok = jnp.where(in_use, order[src] // TOP_K, 0).astype(jnp.int32)
    n_used = (ends[-1:] // MOE_BLOCK).astype(jnp.int32)

    xb = gather_rows(xn2, slot_tok, BF16)
    hb = moe_up(xb, block_e, n_used, w_gate_up[l], b_gate_up[l], 1024)
    yb = moe_down(hb, block_e, n_used, w_down[l], b_down[l], 1024)

    def combine(row0, out_block, out_full, out_map):
        rows = (row0[:, None] + jnp.arange(MOE_BLOCK, dtype=jnp.int32)[None, :])
        slots = jnp.transpose(slot_of_pair[rows], (0, 2, 1)).reshape(row0.shape[0], 1, TOP_K * MOE_BLOCK)
        return moe_combine(yb, h1, slots, top_gate[rows.reshape(-1)], row0, lnf_g, (out_block, out_full),
                           out_map, MOE_BLOCK)

    tiles_per_seq = seq // MOE_BLOCK
    tile_id = jnp.arange(bp * tiles_per_seq, dtype=jnp.int32)
    row0_p = (tile_id // tiles_per_seq) * PROMPT_PAD_T + N_META + (tile_id % tiles_per_seq) * MOE_BLOCK
    y_prompt = combine(row0_p, (1, MOE_BLOCK, d), (bp, seq, d),
                       lambda i: (i // tiles_per_seq, i % tiles_per_seq, 0))
    y_sample = combine(jnp.full((1,), n_prow, jnp.int32), (MOE_BLOCK, d), (bs, d), lambda i: (i, 0))

    t_last = t_prompt - 1
    prompt_conv = xbc_p[:, t_prompt - (SSM_CONV - 1):t_prompt]
    sample_conv = jnp.concatenate([state_ssm_conv[l][:, 1:], xbc_s[:, 0:1]], axis=1)
    return (y_prompt, y_sample.reshape(bs, 1, d), prompt_conv[None], ssm_p[None], rw_p[:, t_last][None], wkv_p[None],
            sample_conv[None], ssm_s[None], rw_s[:, 0][None], wkv_s[None])
```

```python
import functools
import math

import jax
import jax.numpy as jnp
from jax import lax
from jax.experimental import pallas as pl
from jax.experimental.pallas import tpu as pltpu

F32 = jnp.float32
BF16 = jnp.bfloat16

V7X_VMEM_LIMIT_BYTES = 56 * 1024 * 1024
LANES = 128
SUBLANES = 8

D_MODEL = 2048
N_META = 16
RMS_EPS = 1e-6
SSM_D_INNER = 2048
SSM_HEAD_DIM = 64
SSM_N_HEADS = 32
SSM_N_GROUPS = 4
SSM_HPG = 8
SSM_D_STATE = 128
SSM_CONV = 4
SSM_CHUNK = 128
SSM_CONV_DIM = SSM_D_INNER + 2 * SSM_N_GROUPS * SSM_D_STATE
SSM_NORM_EPS = 1e-5
RWKV_DIM = 2048
RWKV_HEAD = 64
RWKV_N_HEADS = 32
RWKV_LORA = 64
RWKV_GN_EPS = 64e-5
RWKV_SHIFT_DIM = 3 * RWKV_DIM + 2 * RWKV_LORA
RWKV_CHUNK = 64
N_EXPERTS = 32
TOP_K = 4
D_FF = 2048
SWIGLU_LIMIT = 7.0
SWIGLU_ALPHA = 1.702
MOE_BLOCK = 128


def _cparams(*sem):
    return pltpu.CompilerParams(dimension_semantics=sem, vmem_limit_bytes=V7X_VMEM_LIMIT_BYTES)


def _split3(x):
    hi = x.astype(BF16)
    r1 = x - hi.astype(F32)
    mid = r1.astype(BF16)
    lo = (r1 - mid.astype(F32)).astype(BF16)
    return hi, mid, lo


def _dot(a, b, dims=None):
    a = a.astype(BF16)
    b = b.astype(BF16)
    if dims is None:
        return jnp.dot(a, b, preferred_element_type=F32)
    return lax.dot_general(a, b, (dims, ((), ())), preferred_element_type=F32)


def _dot_exact_lhs(x, m01):
    hi, mid, lo = _split3(x)
    m = m01.astype(BF16)
    return (jnp.dot(hi, m, preferred_element_type=F32) + jnp.dot(mid, m, preferred_element_type=F32)
            + jnp.dot(lo, m, preferred_element_type=F32))


def _dot_f32(a, b):
    ah, am, al = _split3(a)
    bh, bm, bl = _split3(b)
    lhs = jnp.concatenate([ah, ah, am, ah, al, am], axis=1)
    rhs = jnp.concatenate([bh, bm, bh, bl, bh, bm], axis=0)
    return jnp.dot(lhs, rhs, preferred_element_type=F32)


def _softplus(x):
    return jnp.maximum(x, 0.0) + jnp.log(1.0 + jnp.exp(-jnp.abs(x)))


def _sigmoid(x):
    return 1.0 / (1.0 + jnp.exp(-x))


def _silu(x):
    return x * _sigmoid(x)


def _rmsnorm_kernel(x_ref, g_ref, o_ref, *, eps):
    x = x_ref[...]
    ms = jnp.mean(x * x, axis=-1, keepdims=True)
    o_ref[...] = (x * lax.rsqrt(ms + eps) * g_ref[...]).astype(o_ref.dtype)


def rmsnorm_rows(x, g, out_dtype, tm):
    rows, d = x.shape
    return pl.pallas_call(
        functools.partial(_rmsnorm_kernel, eps=RMS_EPS),
        out_shape=jax.ShapeDtypeStruct((rows, d), out_dtype),
        grid=(rows // tm,),
        in_specs=[pl.BlockSpec((tm, d), lambda i: (i, 0)), pl.BlockSpec((1, d), lambda i: (0, 0))],
        out_specs=pl.BlockSpec((tm, d), lambda i: (i, 0)),
        compiler_params=_cparams("parallel"),
        name="rmsnorm_rows",
    )(x, g.reshape(1, d))


def _matmul_kernel(x_ref, w_ref, o_ref):
    o_ref[...] = jnp.dot(x_ref[...], w_ref[...], preferred_element_type=F32)


def matmul_bf16(x, w, tm, tn, name):
    rows, k = x.shape
    n = w.shape[1]
    return pl.pallas_call(
        _matmul_kernel,
        out_shape=jax.ShapeDtypeStruct((rows, n), F32),
        grid=(n // tn, rows // tm),
        in_specs=[pl.BlockSpec((tm, k), lambda j, i: (i, 0)), pl.BlockSpec((k, tn), lambda j, i: (0, j))],
        out_specs=pl.BlockSpec((tm, tn), lambda j, i: (i, j)),
        compiler_params=_cparams("parallel", "parallel"),
        name=name,
    )(x, w)


def _ssd_kernel(xbc_ref, z_ref, dtr_ref, conv0_ref, s0_ref, convw_ref, convb_ref, dtb_ref, alog_ref,
                dskip_ref, ng_ref, expand_ref, y_ref, s_ref, buf_ref, *, q, t_valid, exact_state):
    c = pl.program_id(1)
    n_pad = buf_ref.shape[0] - q

    @pl.when(c == 0)
    def _():
        buf_ref[0:n_pad, :] = conv0_ref[0]
        s_ref[0] = s0_ref[0]

    buf_ref[n_pad:n_pad + q, :] = xbc_ref[0]
    acc = convb_ref[...] + convw_ref[SSM_CONV - 1:SSM_CONV, :] * buf_ref[n_pad:n_pad + q, :]
    for k in range(SSM_CONV - 1):
        off = n_pad - (SSM_CONV - 1) + k
        acc = acc + convw_ref[k:k + 1, :] * buf_ref[off:off + q, :]
    buf_ref[0:n_pad, :] = buf_ref[q:q + n_pad, :]
    xbc = _silu(acc)
    gn = SSM_N_GROUPS * SSM_D_STATE
    xs = xbc[:, :SSM_D_INNER]
    bm = xbc[:, SSM_D_INNER:SSM_D_INNER + gn]
    cm = xbc[:, SSM_D_INNER + gn:]

    row = lax.broadcasted_iota(jnp.int32, (q, LANES), 0) + c * q
    lane = lax.broadcasted_iota(jnp.int32, (q, LANES), 1)
    live = jnp.logical_and(row < t_valid, lane < SSM_N_HEADS)
    dt = jnp.where(live, _softplus(dtr_ref[0] + dtb_ref[...]), 0.0)
    da = dt * (-jnp.exp(alog_ref[...]))

    ti = lax.broadcasted_iota(jnp.int32, (q, q), 0)
    si = lax.broadcasted_iota(jnp.int32, (q, q), 1)
    causal = si <= ti
    tril = causal.astype(BF16)
    dh, dm, dl = _split3(da)
    cs = (jnp.dot(tril, dh, preferred_element_type=F32) + jnp.dot(tril, dm, preferred_element_type=F32)
          + jnp.dot(tril, dl, preferred_element_type=F32))
    eye = (lax.broadcasted_iota(jnp.int32, (LANES, LANES), 0)
           == lax.broadcasted_iota(jnp.int32, (LANES, LANES), 1)).astype(BF16)
    ch, cmid, cl = _split3(cs)
    nt = (((1,), (1,)), ((), ()))
    cs_t = (lax.dot_general(eye, ch, nt, preferred_element_type=F32)
            + lax.dot_general(eye, cmid, nt, preferred_element_type=F32)
            + lax.dot_general(eye, cl, nt, preferred_element_type=F32))

    expand = expand_ref[...]
    dt_x = _dot_exact_lhs(dt, expand)
    cs_x = _dot_exact_lhs(cs, expand)
    xdt = xs * dt_x
    ecs = jnp.exp(cs_x)
    dte = jnp.exp(cs_x[q - 1:q, :] - cs_x)

    lane_in_pair = lax.broadcasted_iota(jnp.int32, (q, LANES), 1)
    lo_half = lane_in_pair < SSM_HEAD_DIM

    y_groups = []
    for g in range(SSM_N_GROUPS):
        b_g = bm[:, g * SSM_D_STATE:(g + 1) * SSM_D_STATE]
        c_g = cm[:, g * SSM_D_STATE:(g + 1) * SSM_D_STATE]
        cb = _dot(c_g, b_g, ((1,), (1,)))
        gsl = slice(g * SSM_HPG * SSM_HEAD_DIM, (g + 1) * SSM_HPG * SSM_HEAD_DIM)
        s_g = s_ref[0, g * SSM_HPG:(g + 1) * SSM_HPG].reshape(SSM_HPG * SSM_HEAD_DIM, SSM_D_STATE)
        y_off = _dot(c_g, s_g, ((1,), (1,))) * ecs[:, gsl]
        slabs = []
        for m in range(SSM_HPG // 2):
            lsl = slice(gsl.start + m * LANES, gsl.start + (m + 1) * LANES)
            x_pair = xdt[:, lsl]
            acc_pair = None
            for half in range(2):
                h = g * SSM_HPG + 2 * m + half
                seg = cs[:, h:h + 1] - cs_t[h:h + 1, :]
                lmat = jnp.where(causal, jnp.exp(jnp.where(causal, seg, 0.0)), 0.0)
                x_half = jnp.where(lo_half if half == 0 else jnp.logical_not(lo_half), x_pair, 0.0)
                contrib = _dot(cb * lmat, x_half)
                acc_pair = contrib if acc_pair is None else acc_pair + contrib
            slabs.append(acc_pair)
        y_groups.append(jnp.concatenate(slabs, axis=1) + y_off)

        xw = xdt[:, gsl] * dte[:, gsl]
        tn = ((0,), (0,))
        if exact_state:
            xh, xm, xl = _split3(xw)
            bh, bmid, bl = _split3(b_g)
            upd = _dot(jnp.concatenate([xh, xh, xm, xh, xl, xm], axis=0),
                       jnp.concatenate([bh, bmid, bh, bl, bh, bmid], axis=0), tn)
        else:
            upd = _dot(xw, b_g, tn)
        for j in range(SSM_HPG):
            h = g * SSM_HPG + j
            dec = jnp.exp(cs_t[h:h + 1, q - 1:q])
            s_ref[0, h] = s_ref[0, h] * dec + upd[j * SSM_HEAD_DIM:(j + 1) * SSM_HEAD_DIM, :]

    y = jnp.concatenate(y_groups, axis=1) + dskip_ref[...] * xs
    y = y * _silu(z_ref[0])
    outs = []
    gw = SSM_D_INNER // SSM_N_GROUPS
    for g in range(SSM_N_GROUPS):
        yg = y[:, g * gw:(g + 1) * gw]
        outs.append(yg * lax.rsqrt(jnp.mean(yg * yg, axis=-1, keepdims=True) + SSM_NORM_EPS))
    y_ref[0] = jnp.concatenate(outs, axis=1) * ng_ref[...]


def ssd_branch(xbc, z, dtr, conv0, s0, conv_w, conv_b, dt_bias, a_log, d_skip, norm_g, *, q, t_valid,
               exact_state, name):
    nb, t, _ = xbc.shape
    n_pad = conv0.shape[1]
    pad_l = lambda v: jnp.pad(v.astype(F32), (0, LANES - v.shape[0])).reshape(1, LANES)
    expand = (jnp.arange(LANES)[:, None] == (jnp.arange(SSM_D_INNER) // SSM_HEAD_DIM)[None, :]).astype(BF16)
    row2 = lambda v: v.reshape(1, -1)
    full = lambda a: pl.BlockSpec(a.shape, lambda b, c: (0,) * a.ndim)
    consts = [conv_w, row2(conv_b), pad_l(dt_bias), pad_l(a_log), row2(jnp.repeat(d_skip, SSM_HEAD_DIM)),
              row2(norm_g), expand]
    kern = functools.partial(_ssd_kernel, q=q, t_valid=t_valid, exact_state=exact_state)
    return pl.pallas_call(
        kern,
        out_shape=(jax.ShapeDtypeStruct((nb, t, SSM_D_INNER), F32),
                   jax.ShapeDtypeStruct(s0.shape, F32)),
        grid=(nb, t // q),
        in_specs=[pl.BlockSpec((1, q, SSM_CONV_DIM), lambda b, c: (b, c, 0)),
                  pl.BlockSpec((1, q, SSM_D_INNER), lambda b, c: (b, c, 0)),
                  pl.BlockSpec((1, q, LANES), lambda b, c: (b, c, 0)),
                  pl.BlockSpec((1, n_pad, SSM_CONV_DIM), lambda b, c: (b, 0, 0)),
                  pl.BlockSpec((1,) + s0.shape[1:], lambda b, c: (b, 0, 0, 0))] + [full(a) for a in consts],
        out_specs=(pl.BlockSpec((1, q, SSM_D_INNER), lambda b, c: (b, c, 0)),
                   pl.BlockSpec((1,) + s0.shape[1:], lambda b, c: (b, 0, 0, 0))),
        scratch_shapes=[pltpu.VMEM((q + n_pad, SSM_CONV_DIM), F32)],
        compiler_params=_cparams("parallel", "arbitrary"),
        name=name,
    )(xbc, z, dtr, conv0, s0, *consts)


RWKV_GROUP_HEADS = 4
RWKV_GROUP_LANES = RWKV_GROUP_HEADS * RWKV_HEAD


def _rep_rows(x, n):
    return jnp.concatenate([x] * n, axis=0)


def _rwkv_kernel(rw_ref, shift0_ref, wkv0_ref, mu_ref, w0_ref, w2_ref, a0_ref, a2_ref, kk_ref, ka_ref, rk_ref,
                 gng_ref, gnb_ref, hred_ref, hexp_ref, tile4_ref, tile4t_ref,
                 y_ref, wkv_ref, buf_ref, s_ref, *, cq, t_valid, single_token, exact_state):
    c = pl.program_id(1)
    n_chunks = pl.num_programs(1)
    n_pad = buf_ref.shape[0] - cq
    gl = RWKV_GROUP_LANES
    gh = RWKV_GROUP_HEADS
    n_groups = RWKV_DIM // gl
    nt = ((1,), (1,))
    tn = ((0,), (0,))

    def head_mask(rows_per_head, cols_per_head, n_rows, n_cols):
        rr = lax.broadcasted_iota(jnp.int32, (n_rows, n_cols), 0) // rows_per_head
        cc = lax.broadcasted_iota(jnp.int32, (n_rows, n_cols), 1) // cols_per_head
        return rr == cc

    m_state = head_mask(RWKV_HEAD, RWKV_HEAD, gl, gl)
    m_exp = head_mask(cq, RWKV_HEAD, gh * cq, gl)
    m_nbd = head_mask(cq, cq, gh * cq, gh * cq)

    @pl.when(c == 0)
    def _():
        buf_ref[0:n_pad, :] = shift0_ref[0]
        for g in range(n_groups):
            stack = wkv0_ref[0, g * gl:(g + 1) * gl, :]
            s_ref[g] = jnp.where(m_state, jnp.concatenate([stack] * gh, axis=1), 0.0)

    buf_ref[n_pad:n_pad + cq, :] = rw_ref[0]
    cur = buf_ref[n_pad:n_pad + cq, :]
    prev = buf_ref[n_pad - 1:n_pad - 1 + cq, :]
    buf_ref[0:n_pad, :] = buf_ref[cq:cq + n_pad, :]
    mixed = cur + (prev - cur) * mu_ref[...]
    r = mixed[:, 0:RWKV_DIM]
    k = mixed[:, RWKV_DIM:2 * RWKV_DIM]
    v = mixed[:, 2 * RWKV_DIM:3 * RWKV_DIM]
    lora = mixed[:, 3 * RWKV_DIM:]
    lane = lax.broadcasted_iota(jnp.int32, (cq, LANES), 1)
    wl = jnp.where(lane < RWKV_LORA, jnp.tanh(lora), 0.0)
    al = jnp.where(lane >= RWKV_LORA, lora, 0.0)
    w_log = -_softplus(-(w0_ref[...] + _dot_f32(wl, w2_ref[...]))) - 0.5
    a = _sigmoid(a0_ref[...] + _dot_f32(al, a2_ref[...]))

    row = lax.broadcasted_iota(jnp.int32, (cq, 1), 0) + c * cq
    live = row < t_valid
    logw = jnp.where(live, -jnp.exp(w_log), 0.0)
    kk_raw = k * kk_ref[...]
    ss = _dot_exact_lhs(kk_raw * kk_raw, hred_ref[...])
    inv_n = 1.0 / jnp.maximum(jnp.sqrt(ss), 1e-12)
    kk = jnp.where(live, kk_raw * _dot_exact_lhs(inv_n, hexp_ref[...]), 0.0)
    k2 = jnp.where(live, k * (1.0 + (a - 1.0) * ka_ref[...]), 0.0)
    v = jnp.where(live, v, 0.0)

    ti = lax.broadcasted_iota(jnp.int32, (cq, cq), 0)
    si = lax.broadcasted_iota(jnp.int32, (cq, cq), 1)
    tril = (si <= ti).astype(BF16)
    lh, lm, ll = _split3(logw)
    cl = (jnp.dot(tril, lh, preferred_element_type=F32) + jnp.dot(tril, lm, preferred_element_type=F32)
          + jnp.dot(tril, ll, preferred_element_type=F32))
    p_incl = jnp.exp(cl)
    p_inv = jnp.exp(-cl)
    a_t = jnp.exp(cl - logw) * (-kk)
    r_t = p_incl * r
    b_t = p_inv * (kk * a)
    k_t = p_inv * k2
    p_end = p_incl[cq - 1:cq, :]

    tt = lax.broadcasted_iota(jnp.int32, (cq, gh * cq), 0)
    ss_i = lax.broadcasted_iota(jnp.int32, (cq, gh * cq), 1) % cq
    strict = ss_i < tt
    incl = ss_i <= tt

    def expand_rows(x):
        return jnp.where(m_exp, _rep_rows(x, gh), 0.0)

    groups = range(n_groups)
    sls = [slice(g * gl, (g + 1) * gl) for g in groups]
    lhs_ar = [jnp.concatenate([a_t[:, sl], r_t[:, sl]], axis=0) for sl in sls]
    v_exp = [expand_rows(v[:, sl]) for sl in sls]
    s_bd = [s_ref[g] for g in groups]
    ab = [_dot(lhs_ar[g], expand_rows(b_t[:, sls[g]]), nt) for g in groups]
    ak = [_dot(lhs_ar[g], expand_rows(k_t[:, sls[g]]), nt) for g in groups]
    w_s = [_dot(lhs_ar[g], s_bd[g], nt) for g in groups]
    x = [w_s[g][:cq] + _dot(jnp.where(strict, ak[g][:cq], 0.0), v_exp[g]) for g in groups]
    if not single_token:
        n_cat = [jnp.where(strict, ab[g][:cq], 0.0) for g in groups]
        n_steps = int(math.log2(cq))
        for step in range(n_steps):
            x = [x[g] + _dot(n_cat[g], expand_rows(x[g])) for g in groups]
            if step + 1 < n_steps:
                n_cat = [_dot(n_cat[g], jnp.where(m_nbd, _rep_rows(n_cat[g], gh), 0.0)) for g in groups]
    y_groups = [w_s[g][cq:] + _dot(jnp.where(incl, ab[g][cq:], 0.0), expand_rows(x[g]))
                + _dot(jnp.where(incl, ak[g][cq:], 0.0), v_exp[g]) for g in groups]
    for g in groups:
        uv = jnp.concatenate([x[g], v[:, sls[g]]], axis=0)
        bk = jnp.concatenate([b_t[:, sls[g]], k_t[:, sls[g]]], axis=0)
        if exact_state:
            uh, um, ul = _split3(uv)
            bh, bmid, bl = _split3(bk)
            inc = _dot(jnp.concatenate([uh, uh, um, uh, ul, um], axis=0),
                       jnp.concatenate([bh, bmid, bh, bl, bh, bmid], axis=0), tn)
        else:
            inc = _dot(uv, bk, tn)
        s_ref[g] = jnp.where(m_state, (s_bd[g] + inc) * p_end[:, sls[g]], 0.0)

    y = jnp.concatenate(y_groups, axis=1)
    inv_h = 1.0 / RWKV_HEAD
    mean = _dot_exact_lhs(_dot_exact_lhs(y, hred_ref[...]) * inv_h, hexp_ref[...])
    yc = y - mean
    var = _dot_exact_lhs(yc * yc, hred_ref[...]) * inv_h
    rstd = _dot_exact_lhs(lax.rsqrt(var + RWKV_GN_EPS), hexp_ref[...])
    bonus = _dot_exact_lhs(_dot_exact_lhs(r * k2 * rk_ref[...], hred_ref[...]), hexp_ref[...]) * v
    y_ref[0] = yc * rstd * gng_ref[...] + gnb_ref[...] + bonus

    @pl.when(c == n_chunks - 1)
    def _():
        for g in range(n_groups):
            s_g = s_ref[g]
            stack = s_g[:, 0:RWKV_HEAD]
            for hh in range(1, gh):
                stack = stack + s_g[:, hh * RWKV_HEAD:(hh + 1) * RWKV_HEAD]
            wkv_ref[0, g * gl:(g + 1) * gl, :] = stack


def rwkv_branch(rw, shift0, wkv0, mu, w0, w2, a0, a2, k_k, k_a, r_k, gn_g, gn_b, *, cq, t_valid, single_token,
                exact_state, name):
    nb, t, _ = rw.shape
    n_pad = shift0.shape[1]
    row2 = lambda a: a.reshape(1, -1).astype(F32)
    zeros_l = jnp.zeros((RWKV_LORA, RWKV_DIM), F32)
    head_of_lane = jnp.arange(RWKV_DIM) // RWKV_HEAD
    hexp = (jnp.arange(LANES)[:, None] == head_of_lane[None, :]).astype(BF16)
    hred = hexp.T
    tile4 = (jnp.arange(RWKV_HEAD)[:, None] == (jnp.arange(RWKV_GROUP_LANES) % RWKV_HEAD)[None, :]).astype(BF16)
    consts = [row2(mu), row2(w0), jnp.concatenate([w2, zeros_l], axis=0), row2(a0),
              jnp.concatenate([zeros_l, a2], axis=0), row2(k_k), row2(k_a), row2(r_k), row2(gn_g), row2(gn_b),
              hred, hexp, tile4, tile4.T]
    full = lambda a: pl.BlockSpec(a.shape, lambda b, c: (0,) * a.ndim)
    wkv_rows = wkv0.reshape(nb, RWKV_DIM, RWKV_HEAD)
    kern = functools.partial(_rwkv_kernel, cq=cq, t_valid=t_valid, single_token=single_token,
                             exact_state=exact_state)
    y, wkv = pl.pallas_call(
        kern,
        out_shape=(jax.ShapeDtypeStruct((nb, t, RWKV_DIM), F32),
                   jax.ShapeDtypeStruct(wkv_rows.shape, F32)),
        grid=(nb, t // cq),
        in_specs=[pl.BlockSpec((1, cq, RWKV_SHIFT_DIM), lambda b, c: (b, c, 0)),
                  pl.BlockSpec((1, n_pad, RWKV_SHIFT_DIM), lambda b, c: (b, 0, 0)),
                  pl.BlockSpec((1, RWKV_DIM, RWKV_HEAD), lambda b, c: (b, 0, 0))] + [full(a) for a in consts],
        out_specs=(pl.BlockSpec((1, cq, RWKV_DIM), lambda b, c: (b, c, 0)),
                   pl.BlockSpec((1, RWKV_DIM, RWKV_HEAD), lambda b, c: (b, 0, 0))),
        scratch_shapes=[pltpu.VMEM((cq + n_pad, RWKV_SHIFT_DIM), F32),
                        pltpu.VMEM((RWKV_DIM // RWKV_GROUP_LANES, RWKV_GROUP_LANES, RWKV_GROUP_LANES), F32)],
        compiler_params=_cparams("parallel", "arbitrary"),
        name=name,
    )(rw, shift0, wkv_rows, *consts)
    return y, wkv.reshape(wkv0.shape)


NEG_BIG = -1e30


def _mix_kernel(g_ref, yap_ref, ybp_ref, yas_ref, ybs_ref, h_ref, wout_ref, ln2_ref, rw_ref, rb_ref,
                h1_ref, xn_ref, idx_ref, gate_ref):
    i = pl.program_id(0)
    last = i == pl.num_programs(0) - 1
    ya = jnp.where(last, yas_ref[...], yap_ref[...])
    yb = jnp.where(last, ybs_ref[...], ybp_ref[...])
    g = g_ref[...]
    merged = _sigmoid(g[:, :D_MODEL]) * ya + _sigmoid(g[:, D_MODEL:]) * yb
    h1 = h_ref[...] + jnp.dot(merged.astype(BF16), wout_ref[...], preferred_element_type=F32)
    h1_ref[...] = h1
    ms = jnp.mean(h1 * h1, axis=-1, keepdims=True)
    xn = h1 * lax.rsqrt(ms + RMS_EPS) * ln2_ref[...]
    xn_ref[...] = xn
    logits = _dot_f32(xn, rw_ref[...]) + rb_ref[...]
    lane = lax.broadcasted_iota(jnp.int32, logits.shape, 1).astype(F32)
    idx_out = jnp.zeros(logits.shape, F32)
    val_out = jnp.zeros(logits.shape, F32)
    top0 = None
    for kth in range(TOP_K):
        m = jnp.max(logits, axis=-1, keepdims=True)
        sel = jnp.min(jnp.where(logits == m, lane, float(LANES)), axis=-1, keepdims=True)
        if kth == 0:
            top0 = m
        idx_out = jnp.where(lane == kth, sel, idx_out)
        val_out = jnp.where(lane == kth, jnp.exp(m - top0), val_out)
        logits = jnp.where(lane == sel, -jnp.inf, logits)
    idx_ref[...] = idx_out.astype(jnp.int32)
    gate_ref[...] = val_out / jnp.sum(val_out, axis=-1, keepdims=True)


def mix_and_route(gates, ya_p, yb_p, ya_s, yb_s, h_rows, w_out_bf, ln2_g, router_w, router_b, tm):
    rows = h_rows.shape[0]
    n_tiles = rows // tm
    last_p = ya_p.shape[0] // tm - 1
    rw_pad = jnp.pad(router_w, ((0, 0), (0, LANES - N_EXPERTS)))
    rb_pad = jnp.pad(router_b, (0, LANES - N_EXPERTS), constant_values=NEG_BIG).reshape(1, LANES)
    row_spec = lambda n: pl.BlockSpec((tm, n), lambda i: (i, 0))
    prompt_spec = pl.BlockSpec((tm, D_MODEL), lambda i: (jnp.minimum(i, last_p), 0))
    fixed = lambda a: pl.BlockSpec(a.shape, lambda i: (0,) * a.ndim)
    return pl.pallas_call(
        _mix_kernel,
        out_shape=(jax.ShapeDtypeStruct((rows, D_MODEL), F32), jax.ShapeDtypeStruct((rows, D_MODEL), F32),
                   jax.ShapeDtypeStruct((rows, LANES), jnp.int32), jax.ShapeDtypeStruct((rows, LANES), F32)),
        grid=(n_tiles,),
        in_specs=[row_spec(2 * D_MODEL), prompt_spec, prompt_spec, fixed(ya_s), fixed(yb_s), row_spec(D_MODEL),
                  fixed(w_out_bf), pl.BlockSpec((1, D_MODEL), lambda i: (0, 0)), fixed(rw_pad), fixed(rb_pad)],
        out_specs=(row_spec(D_MODEL), row_spec(D_MODEL), row_spec(LANES), row_spec(LANES)),
        compiler_params=_cparams("parallel"),
        name="mix_and_route",
    )(gates, ya_p, yb_p, ya_s, yb_s, h_rows, w_out_bf, ln2_g.reshape(1, D_MODEL), rw_pad, rb_pad)


def _gather_rows_kernel(tok_ref, x_hbm, o_ref, rows_ref, sem):
    def copy(r):
        return pltpu.make_async_copy(x_hbm.at[pl.ds(tok_ref[0, 0, r], 1), :], rows_ref.at[pl.ds(r, 1), :], sem)

    def start(r, carry):
        copy(r).start()
        return carry

    def wait(r, carry):
        copy(r).wait()
        return carry

    lax.fori_loop(0, MOE_BLOCK, start, 0)
    lax.fori_loop(0, MOE_BLOCK, wait, 0)
    o_ref[...] = rows_ref[...].astype(o_ref.dtype)


def gather_rows(x, slot_tok, out_dtype):
    n_blocks = slot_tok.shape[0] // MOE_BLOCK
    d = x.shape[1]
    return pl.pallas_call(
        _gather_rows_kernel,
        out_shape=jax.ShapeDtypeStruct((n_blocks * MOE_BLOCK, d), out_dtype),
        grid=(n_blocks,),
        in_specs=[pl.BlockSpec((1, 1, MOE_BLOCK), lambda i: (i, 0, 0), memory_space=pltpu.SMEM),
                  pl.BlockSpec(memory_space=pl.ANY)],
        out_specs=pl.BlockSpec((MOE_BLOCK, d), lambda i: (i, 0)),
        scratch_shapes=[pltpu.VMEM((MOE_BLOCK, d), x.dtype), pltpu.SemaphoreType.DMA(())],
        compiler_params=_cparams("arbitrary"),
        name="moe_gather",
    )(slot_tok.reshape(n_blocks, 1, MOE_BLOCK), x)


def _expert_changed(be_ref, i):
    prev = be_ref[jnp.maximum(i, 1) - 1]
    return jnp.logical_or(i == 0, be_ref[i] != prev)


def _moe_up_kernel(be_ref, nb_ref, x_ref, wg_ref, wu_ref, bg_ref, bu_ref, o_ref, wg_bf, wu_bf):
    i = pl.program_id(1)

    @pl.when(_expert_changed(be_ref, i))
    def _():
        wg_bf[...] = wg_ref[0].astype(BF16)
        wu_bf[...] = wu_ref[0].astype(BF16)

    @pl.when(i < nb_ref[0])
    def _():
        x = x_ref[...]
        g = jnp.dot(x, wg_bf[...], preferred_element_type=F32) + bg_ref[0]
        u = jnp.dot(x, wu_bf[...], preferred_element_type=F32) + bu_ref[0]
        g = jnp.minimum(g, SWIGLU_LIMIT)
        u = jnp.clip(u, -SWIGLU_LIMIT, SWIGLU_LIMIT)
        o_ref[...] = ((u + 1.0) * (g * _sigmoid(g * SWIGLU_ALPHA))).astype(o_ref.dtype)

    @pl.when(i >= nb_ref[0])
    def _():
        o_ref[...] = jnp.zeros_like(o_ref)


def moe_up(xb, block_e, n_used, w_gate_up, b_gate_up, tn):
    n_blocks = xb.shape[0] // MOE_BLOCK
    n_ff_tiles = D_FF // tn
    b3 = b_gate_up.reshape(N_EXPERTS, 1, 2 * D_FF)
    grid_spec = pltpu.PrefetchScalarGridSpec(
        num_scalar_prefetch=2,
        grid=(n_ff_tiles, n_blocks),
        in_specs=[pl.BlockSpec((MOE_BLOCK, D_MODEL), lambda j, i, be, nb: (i, 0)),
                  pl.BlockSpec((1, D_MODEL, tn), lambda j, i, be, nb: (be[i], 0, j)),
                  pl.BlockSpec((1, D_MODEL, tn), lambda j, i, be, nb: (be[i], 0, j + n_ff_tiles)),
                  pl.BlockSpec((1, 1, tn), lambda j, i, be, nb: (be[i], 0, j)),
                  pl.BlockSpec((1, 1, tn), lambda j, i, be, nb: (be[i], 0, j + n_ff_tiles))],
        out_specs=pl.BlockSpec((MOE_BLOCK, tn), lambda j, i, be, nb: (i, j)),
        scratch_shapes=[pltpu.VMEM((D_MODEL, tn), BF16), pltpu.VMEM((D_MODEL, tn), BF16)],
    )
    return pl.pallas_call(
        _moe_up_kernel,
        out_shape=jax.ShapeDtypeStruct((n_blocks * MOE_BLOCK, D_FF), BF16),
        grid_spec=grid_spec,
        compiler_params=_cparams("arbitrary", "arbitrary"),
        name="moe_up",
    )(block_e, n_used, xb, w_gate_up, w_gate_up, b3, b3)


def _moe_down_kernel(be_ref, nb_ref, h_ref, wd_ref, bd_ref, o_ref, wd_bf):
    i = pl.program_id(1)

    @pl.when(_expert_changed(be_ref, i))
    def _():
        wd_bf[...] = wd_ref[0].astype(BF16)

    @pl.when(i < nb_ref[0])
    def _():
        o_ref[...] = jnp.dot(h_ref[...], wd_bf[...], preferred_element_type=F32) + bd_ref[0]

    @pl.when(i >= nb_ref[0])
    def _():
        o_ref[...] = jnp.zeros_like(o_ref)


def moe_down(hb, block_e, n_used, w_down, b_down, tn):
    n_blocks = hb.shape[0] // MOE_BLOCK
    b3 = b_down.reshape(N_EXPERTS, 1, D_MODEL)
    grid_spec = pltpu.PrefetchScalarGridSpec(
        num_scalar_prefetch=2,
        grid=(D_MODEL // tn, n_blocks),
        in_specs=[pl.BlockSpec((MOE_BLOCK, D_FF), lambda j, i, be, nb: (i, 0)),
                  pl.BlockSpec((1, D_FF, tn), lambda j, i, be, nb: (be[i], 0, j)),
                  pl.BlockSpec((1, 1, tn), lambda j, i, be, nb: (be[i], 0, j))],
        out_specs=pl.BlockSpec((MOE_BLOCK, tn), lambda j, i, be, nb: (i, j)),
        scratch_shapes=[pltpu.VMEM((D_FF, tn), BF16)],
    )
    return pl.pallas_call(
        _moe_down_kernel,
        out_shape=jax.ShapeDtypeStruct((n_blocks * MOE_BLOCK, D_MODEL), F32),
        grid_spec=grid_spec,
        compiler_params=_cparams("arbitrary", "arbitrary"),
        name="moe_down",
    )(block_e, n_used, hb, w_down, b_down.reshape(N_EXPERTS, 1, D_MODEL))


def _combine_kernel(slot_ref, row0_ref, gate_ref, lnf_ref, yb_hbm, h_hbm, o_ref, rows_ref, h_vmem, sem, hsem):
    i = pl.program_id(0)
    tm = o_ref.shape[-2]
    h_copy = pltpu.make_async_copy(h_hbm.at[pl.ds(pl.multiple_of(row0_ref[i], SUBLANES), tm), :], h_vmem, hsem)
    h_copy.start()

    def copy(n):
        k = lax.shift_right_logical(n, int(math.log2(tm)))
        r = jnp.bitwise_and(n, tm - 1)
        return pltpu.make_async_copy(yb_hbm.at[pl.ds(slot_ref[0, 0, n], 1), :],
                                     rows_ref.at[k, pl.ds(r, 1), :], sem)

    def start(n, carry):
        copy(n).start()
        return carry

    def wait(n, carry):
        copy(n).wait()
        return carry

    lax.fori_loop(0, TOP_K * tm, start, 0)
    lax.fori_loop(0, TOP_K * tm, wait, 0)
    h_copy.wait()
    gate = gate_ref[...]
    acc = h_vmem[...]
    for k in range(TOP_K):
        acc = acc + rows_ref[k] * gate[:, k:k + 1]
    ms = jnp.mean(acc * acc, axis=-1, keepdims=True)
    o_ref[...] = (acc * lax.rsqrt(ms + RMS_EPS) * lnf_ref[...]).reshape(o_ref.shape)


def moe_combine(yb, h1, slot_of_pair, gates, row0, lnf_g, out_shape, out_index_map, tm):
    n_tiles = row0.shape[0]
    grid_spec = pltpu.PrefetchScalarGridSpec(
        num_scalar_prefetch=0,
        grid=(n_tiles,),
        in_specs=[pl.BlockSpec((1, 1, TOP_K * tm), lambda i: (i, 0, 0), memory_space=pltpu.SMEM),
                  pl.BlockSpec(memory_space=pltpu.SMEM),
                  pl.BlockSpec((tm, LANES), lambda i: (i, 0)),
                  pl.BlockSpec((1, D_MODEL), lambda i: (0, 0)),
                  pl.BlockSpec(memory_space=pl.ANY),
                  pl.BlockSpec(memory_space=pl.ANY)],
        out_specs=pl.BlockSpec(out_shape[0], out_index_map),
        scratch_shapes=[pltpu.VMEM((TOP_K, tm, D_MODEL), F32), pltpu.VMEM((tm, D_MODEL), F32),
                        pltpu.SemaphoreType.DMA(()), pltpu.SemaphoreType.DMA(())],
    )
    return pl.pallas_call(
        _combine_kernel,
        out_shape=jax.ShapeDtypeStruct(out_shape[1], F32),
        grid_spec=grid_spec,
        compiler_params=_cparams("arbitrary"),
        name="moe_combine",
    )(slot_of_pair, row0, gates, lnf_g.reshape(1, D_MODEL), yb, h1)


PROMPT_PAD_T = 2176
SAMPLE_PAD_T = SUBLANES
ROW_TILE = 1104


def kernel(x_prompt, x_sample, state_ssm_conv, state_ssm, state_rwkv_shift, state_rwkv_wkv, meta_tokens, ln1_g, w_in, ssm_conv_w, ssm_conv_b, ssm_dt_bias, ssm_A_log, ssm_D, ssm_norm_g, rwkv_mu, rwkv_w0, rwkv_w2, rwkv_a0, rwkv_a2, rwkv_k_k, rwkv_k_a, rwkv_r_k, rwkv_gn_g, rwkv_gn_b, w_out, ln2_g, router_w, router_b, w_gate_up, b_gate_up, w_down, b_down, lnf_g):
    bp, seq, d = x_prompt.shape
    bs = x_sample.shape[0]
    t_prompt = N_META + seq
    meta = jnp.broadcast_to(meta_tokens[None], (bp, N_META, d))
    hp = jnp.concatenate([meta, x_prompt, jnp.zeros((bp, PROMPT_PAD_T - t_prompt, d), F32)], axis=1)
    h_rows = jnp.concatenate([hp.reshape(bp * PROMPT_PAD_T, d), x_sample.reshape(bs, d)], axis=0)
    n_prow = bp * PROMPT_PAD_T

    l = 0
    xn = rmsnorm_rows(h_rows, ln1_g[l], BF16, ROW_TILE)
    w = w_in[l]
    o_xbc = SSM_D_INNER
    o_dt = o_xbc + SSM_CONV_DIM
    o_rw = o_dt + SSM_N_HEADS
    o_g = o_rw + RWKV_SHIFT_DIM
    w_z = w[:, :o_xbc].astype(BF16)
    w_xbc = w[:, o_xbc:o_dt].astype(BF16)
    w_dt = jnp.pad(w[:, o_dt:o_rw], ((0, 0), (0, LANES - SSM_N_HEADS))).astype(BF16)
    w_rw = w[:, o_rw:o_g].astype(BF16)
    w_g = w[:, o_g:].astype(BF16)
    z = matmul_bf16(xn, w_z, ROW_TILE, 1024, "proj_z")
    xbc = matmul_bf16(xn, w_xbc, ROW_TILE, 1024, "proj_xbc")
    dtr = matmul_bf16(xn, w_dt, ROW_TILE, LANES, "proj_dt")

    def split_rows(a):
        n = a.shape[1]
        a_s = jnp.pad(a[n_prow:].reshape(bs, 1, n), ((0, 0), (0, SAMPLE_PAD_T - 1), (0, 0)))
        return a[:n_prow].reshape(bp, PROMPT_PAD_T, n), a_s

    z_p, z_s = split_rows(z)
    xbc_p, xbc_s = split_rows(xbc)
    dtr_p, dtr_s = split_rows(dtr)
    ssm_args = (ssm_conv_w[l], ssm_conv_b[l], ssm_dt_bias[l], ssm_A_log[l], ssm_D[l], ssm_norm_g[l])
    conv0_p = jnp.zeros((bp, SUBLANES, SSM_CONV_DIM), F32)
    s0_p = jnp.zeros((bp, SSM_N_HEADS, SSM_HEAD_DIM, SSM_D_STATE), F32)
    ya_p, ssm_p = ssd_branch(xbc_p, z_p, dtr_p, conv0_p, s0_p, *ssm_args, q=SSM_CHUNK, t_valid=t_prompt,
                             exact_state=False, name="ssd_prompt")
    conv0_s = jnp.pad(state_ssm_conv[l], ((0, 0), (SUBLANES - (SSM_CONV - 1), 0), (0, 0)))
    ya_s, ssm_s = ssd_branch(xbc_s, z_s, dtr_s, conv0_s, state_ssm[l], *ssm_args, q=SAMPLE_PAD_T, t_valid=1,
                             exact_state=True, name="ssd_sample")
    rw = matmul_bf16(xn, w_rw, ROW_TILE, 896, "proj_rw")
    rw_p, rw_s = split_rows(rw)
    rwkv_args = (rwkv_mu[l], rwkv_w0[l], rwkv_w2[l], rwkv_a0[l], rwkv_a2[l], rwkv_k_k[l], rwkv_k_a[l],
                 rwkv_r_k[l], rwkv_gn_g[l], rwkv_gn_b[l])
    shift0_p = jnp.zeros((bp, SUBLANES, RWKV_SHIFT_DIM), F32)
    wkv0_p = jnp.zeros((bp, RWKV_N_HEADS, RWKV_HEAD, RWKV_HEAD), F32)
    yb_p, wkv_p = rwkv_branch(rw_p, shift0_p, wkv0_p, *rwkv_args, cq=RWKV_CHUNK, t_valid=t_prompt,
                              single_token=False, exact_state=False, name="rwkv_prompt")
    shift0_s = jnp.pad(state_rwkv_shift[l][:, None], ((0, 0), (SUBLANES - 1, 0), (0, 0)))
    yb_s, wkv_s = rwkv_branch(rw_s, shift0_s, state_rwkv_wkv[l], *rwkv_args, cq=SAMPLE_PAD_T, t_valid=1,
                              single_token=True, exact_state=True, name="rwkv_sample")
    gates = matmul_bf16(xn, w_g, ROW_TILE, 1024, "proj_gates")
    h1, xn2, top_idx, top_gate = mix_and_route(
        gates, ya_p.reshape(n_prow, d), yb_p.reshape(n_prow, d), ya_s[:, 0], yb_s[:, 0], h_rows,
        w_out[l].astype(BF16), ln2_g[l], router_w[l], router_b[l], MOE_BLOCK)

    n_rows = h_rows.shape[0]
    row_id = jnp.arange(n_rows, dtype=jnp.int32)
    valid = jnp.logical_or(row_id >= n_prow, row_id % PROMPT_PAD_T < t_prompt)
    key = jnp.where(valid[:, None], top_idx[:, :TOP_K], N_EXPERTS).reshape(-1)
    n_pairs = key.shape[0]
    n_valid_pairs = (bp * t_prompt + bs) * TOP_K
    n_blocks = -(-n_valid_pairs // MOE_BLOCK) + N_EXPERTS
    order = jnp.argsort(key, stable=True).astype(jnp.int32)
    counts = jnp.sum((key[:, None] == jnp.arange(N_EXPERTS, dtype=jnp.int32)[None, :]).astype(jnp.int32), axis=0)
    padded = (counts + MOE_BLOCK - 1) // MOE_BLOCK * MOE_BLOCK
    ends = jnp.cumsum(padded)
    pstart = ends - padded
    start = jnp.cumsum(counts) - counts
    inv = jnp.zeros((n_pairs,), jnp.int32).at[order].set(jnp.arange(n_pairs, dtype=jnp.int32), unique_indices=True)
    e_clip = jnp.minimum(key, N_EXPERTS - 1)
    slot_of_pair = jnp.where(key < N_EXPERTS, pstart[e_clip] + inv - start[e_clip], 0).reshape(n_rows, TOP_K)
    block_e = jnp.minimum(jnp.searchsorted(ends, jnp.arange(n_blocks, dtype=jnp.int32) * MOE_BLOCK, side='right'),
                          N_EXPERTS - 1).astype(jnp.int32)
    slot_id = jnp.arange(n_blocks * MOE_BLOCK, dtype=jnp.int32)
    e_slot = block_e[slot_id // MOE_BLOCK]
    rank = slot_id - pstart[e_slot]
    in_use = jnp.logical_and(rank >= 0, rank < counts[e_slot])
    src = jnp.clip(start[e_slot] + rank, 0, n_pairs - 1)
    slot_tok = jnp.where(in_use, order[src] // TOP_K, 0).astype(jnp.int32)
    n_used = (ends[-1:] // MOE_BLOCK).astype(jnp.int32)

    xb = gather_rows(xn2, slot_tok, BF16)
    hb = moe_up(xb, block_e, n_used, w_gate_up[l], b_gate_up[l], 1024)
    yb = moe_down(hb, block_e, n_used, w_down[l], b_down[l], 1024)

    def combine(row0, out_block, out_full, out_map):
        rows = (row0[:, None] + jnp.arange(MOE_BLOCK, dtype=jnp.int32)[None, :])
        slots = jnp.transpose(slot_of_pair[rows], (0, 2, 1)).reshape(row0.shape[0], 1, TOP_K * MOE_BLOCK)
        return moe_combine(yb, h1, slots, top_gate[rows.reshape(-1)], row0, lnf_g, (out_block, out_full),
                           out_map, MOE_BLOCK)

    tiles_per_seq = seq // MOE_BLOCK
    tile_id = jnp.arange(bp * tiles_per_seq, dtype=jnp.int32)
    row0_p = (tile_id // tiles_per_seq) * PROMPT_PAD_T + N_META + (tile_id % tiles_per_seq) * MOE_BLOCK
    y_prompt = combine(row0_p, (1, MOE_BLOCK, d), (bp, seq, d),
                       lambda i: (i // tiles_per_seq, i % tiles_per_seq, 0))
    y_sample = combine(jnp.full((1,), n_prow, jnp.int32), (MOE_BLOCK, d), (bs, d), lambda i: (i, 0))

    t_last = t_prompt - 1
    prompt_conv = xbc_p[:, t_prompt - (SSM_CONV - 1):t_prompt]
    sample_conv = jnp.concatenate([state_ssm_conv[l][:, 1:], xbc_s[:, 0:1]], axis=1)
    return (y_prompt, y_sample.reshape(bs, 1, d), prompt_conv[None], ssm_p[None], rw_p[:, t_last][None], wkv_p[None],
            sample_conv[None], ssm_s[None], rw_s[:, 0][None], wkv_s[None])
```

```python
import functools
import math

import jax
import jax.numpy as jnp
from jax import lax
from jax.experimental import pallas as pl
from jax.experimental.pallas import tpu as pltpu

F32 = jnp.float32
BF16 = jnp.bfloat16

V7X_VMEM_LIMIT_BYTES = 56 * 1024 * 1024
LANES = 128
SUBLANES = 8

D_MODEL = 2048
N_META = 16
RMS_EPS = 1e-6
SSM_D_INNER = 2048
SSM_HEAD_DIM = 64
SSM_N_HEADS = 32
SSM_N_GROUPS = 4
SSM_HPG = 8
SSM_D_STATE = 128
SSM_CONV = 4
SSM_CHUNK = 128
SSM_CONV_DIM = SSM_D_INNER + 2 * SSM_N_GROUPS * SSM_D_STATE
SSM_NORM_EPS = 1e-5
RWKV_DIM = 2048
RWKV_HEAD = 64
RWKV_N_HEADS = 32
RWKV_LORA = 64
RWKV_GN_EPS = 64e-5
RWKV_SHIFT_DIM = 3 * RWKV_DIM + 2 * RWKV_LORA
RWKV_CHUNK = 64
N_EXPERTS = 32
TOP_K = 4
D_FF = 2048
SWIGLU_LIMIT = 7.0
SWIGLU_ALPHA = 1.702
MOE_BLOCK = 128


def _cparams(*sem):
    return pltpu.CompilerParams(dimension_semantics=sem, vmem_limit_bytes=V7X_VMEM_LIMIT_BYTES)


def _split3(x):
    hi = x.astype(BF16)
    r1 = x - hi.astype(F32)
    mid = r1.astype(BF16)
    lo = (r1 - mid.astype(F32)).astype(BF16)
    return hi, mid, lo


def _dot(a, b, dims=None):
    a = a.astype(BF16)
    b = b.astype(BF16)
    if dims is None:
        return jnp.dot(a, b, preferred_element_type=F32)
    return lax.dot_general(a, b, (dims, ((), ())), preferred_element_type=F32)


def _dot_exact_lhs(x, m01):
    hi, mid, lo = _split3(x)
    m = m01.astype(BF16)
    return (jnp.dot(hi, m, preferred_element_type=F32) + jnp.dot(mid, m, preferred_element_type=F32)
            + jnp.dot(lo, m, preferred_element_type=F32))


def _dot_f32(a, b):
    ah, am, al = _split3(a)
    bh, bm, bl = _split3(b)
    lhs = jnp.concatenate([ah, ah, am, ah, al, am], axis=1)
    rhs = jnp.concatenate([bh, bm, bh, bl, bh, bm], axis=0)
    return jnp.dot(lhs, rhs, preferred_element_type=F32)


def _softplus(x):
    return jnp.maximum(x, 0.0) + jnp.log(1.0 + jnp.exp(-jnp.abs(x)))


def _sigmoid(x):
    return 1.0 / (1.0 + jnp.exp(-x))


def _silu(x):
    return x * _sigmoid(x)


def _rmsnorm_kernel(x_ref, g_ref, o_ref, *, eps):
    x = x_ref[...]
    ms = jnp.mean(x * x, axis=-1, keepdims=True)
    o_ref[...] = (x * lax.rsqrt(ms + eps) * g_ref[...]).astype(o_ref.dtype)


def rmsnorm_rows(x, g, out_dtype, tm):
    rows, d = x.shape
    return pl.pallas_call(
        functools.partial(_rmsnorm_kernel, eps=RMS_EPS),
        out_shape=jax.ShapeDtypeStruct((rows, d), out_dtype),
        grid=(rows // tm,),
        in_specs=[pl.BlockSpec((tm, d), lambda i: (i, 0)), pl.BlockSpec((1, d), lambda i: (0, 0))],
        out_specs=pl.BlockSpec((tm, d), lambda i: (i, 0)),
        compiler_params=_cparams("parallel"),
        name="rmsnorm_rows",
    )(x, g.reshape(1, d))


def _matmul_kernel(x_ref, w_ref, o_ref):
    o_ref[...] = jnp.dot(x_ref[...], w_ref[...], preferred_element_type=F32)


def matmul_bf16(x, w, tm, tn, name):
    rows, k = x.shape
    n = w.shape[1]
    return pl.pallas_call(
        _matmul_kernel,
        out_shape=jax.ShapeDtypeStruct((rows, n), F32),
        grid=(n // tn, rows // tm),
        in_specs=[pl.BlockSpec((tm, k), lambda j, i: (i, 0)), pl.BlockSpec((k, tn), lambda j, i: (0, j))],
        out_specs=pl.BlockSpec((tm, tn), lambda j, i: (i, j)),
        compiler_params=_cparams("parallel", "parallel"),
        name=name,
    )(x, w)


def _ssd_kernel(xbc_ref, z_ref, dtr_ref, conv0_ref, s0_ref, convw_ref, convb_ref, dtb_ref, alog_ref,
                dskip_ref, ng_ref, expand_ref, y_ref, s_ref, buf_ref, *, q, t_valid, exact_state):
    c = pl.program_id(1)
    n_pad = buf_ref.shape[0] - q

    @pl.when(c == 0)
    def _():
        buf_ref[0:n_pad, :] = conv0_ref[0]
        s_ref[0] = s0_ref[0]

    buf_ref[n_pad:n_pad + q, :] = xbc_ref[0]
    acc = convb_ref[...] + convw_ref[SSM_CONV - 1:SSM_CONV, :] * buf_ref[n_pad:n_pad + q, :]
    for k in range(SSM_CONV - 1):
        off = n_pad - (SSM_CONV - 1) + k
        acc = acc + convw_ref[k:k + 1, :] * buf_ref[off:off + q, :]
    buf_ref[0:n_pad, :] = buf_ref[q:q + n_pad, :]
    xbc = _silu(acc)
    gn = SSM_N_GROUPS * SSM_D_STATE
    xs = xbc[:, :SSM_D_INNER]
    bm = xbc[:, SSM_D_INNER:SSM_D_INNER + gn]
    cm = xbc[:, SSM_D_INNER + gn:]

    row = lax.broadcasted_iota(jnp.int32, (q, LANES), 0) + c * q
    lane = lax.broadcasted_iota(jnp.int32, (q, LANES), 1)
    live = jnp.logical_and(row < t_valid, lane < SSM_N_HEADS)
    dt = jnp.where(live, _softplus(dtr_ref[0] + dtb_ref[...]), 0.0)
    da = dt * (-jnp.exp(alog_ref[...]))

    ti = lax.broadcasted_iota(jnp.int32, (q, q), 0)
    si = lax.broadcasted_iota(jnp.int32, (q, q), 1)
    causal = si <= ti
    tril = causal.astype(BF16)
    dh, dm, dl = _split3(da)
    cs = (jnp.dot(tril, dh, preferred_element_type=F32) + jnp.dot(tril, dm, preferred_element_type=F32)
          + jnp.dot(tril, dl, preferred_element_type=F32))
    eye = (lax.broadcasted_iota(jnp.int32, (LANES, LANES), 0)
           == lax.broadcasted_iota(jnp.int32, (LANES, LANES), 1)).astype(BF16)
    ch, cmid, cl = _split3(cs)
    nt = (((1,), (1,)), ((), ()))
    cs_t = (lax.dot_general(eye, ch, nt, preferred_element_type=F32)
            + lax.dot_general(eye, cmid, nt, preferred_element_type=F32)
            + lax.dot_general(eye, cl, nt, preferred_element_type=F32))

    expand = expand_ref[...]
    dt_x = _dot_exact_lhs(dt, expand)
    cs_x = _dot_exact_lhs(cs, expand)
    xdt = xs * dt_x
    ecs = jnp.exp(cs_x)
    dte = jnp.exp(cs_x[q - 1:q, :] - cs_x)

    lane_in_pair = lax.broadcasted_iota(jnp.int32, (q, LANES), 1)
    lo_half = lane_in_pair < SSM_HEAD_DIM

    y_groups = []
    for g in range(SSM_N_GROUPS):
        b_g = bm[:, g * SSM_D_STATE:(g + 1) * SSM_D_STATE]
        c_g = cm[:, g * SSM_D_STATE:(g + 1) * SSM_D_STATE]
        cb = _dot(c_g, b_g, ((1,), (1,)))
        gsl = slice(g * SSM_HPG * SSM_HEAD_DIM, (g + 1) * SSM_HPG * SSM_HEAD_DIM)
        s_g = s_ref[0, g * SSM_HPG:(g + 1) * SSM_HPG].reshape(SSM_HPG * SSM_HEAD_DIM, SSM_D_STATE)
        y_off = _dot(c_g, s_g, ((1,), (1,))) * ecs[:, gsl]
        slabs = []
        for m in range(SSM_HPG // 2):
            lsl = slice(gsl.start + m * LANES, gsl.start + (m + 1) * LANES)
            x_pair = xdt[:, lsl]
            acc_pair = None
            for half in range(2):
                h = g * SSM_HPG + 2 * m + half
                seg = cs[:, h:h + 1] - cs_t[h:h + 1, :]
                lmat = jnp.where(causal, jnp.exp(jnp.where(causal, seg, 0.0)), 0.0)
                x_half = jnp.where(lo_half if half == 0 else jnp.logical_not(lo_half), x_pair, 0.0)
                contrib = _dot(cb * lmat, x_half)
                acc_pair = contrib if acc_pair is None else acc_pair + contrib
            slabs.append(acc_pair)
        y_groups.append(jnp.concatenate(slabs, axis=1) + y_off)

        xw = xdt[:, gsl] * dte[:, gsl]
        tn = ((0,), (0,))
        if exact_state:
            xh, xm, xl = _split3(xw)
            bh, bmid, bl = _split3(b_g)
            upd = _dot(jnp.concatenate([xh, xh, xm, xh, xl, xm], axis=0),
                       jnp.concatenate([bh, bmid, bh, bl, bh, bmid], axis=0), tn)
        else:
            upd = _dot(xw, b_g, tn)
        for j in range(SSM_HPG):
            h = g * SSM_HPG + j
            dec = jnp.exp(cs_t[h:h + 1, q - 1:q])
            s_ref[0, h] = s_ref[0, h] * dec + upd[j * SSM_HEAD_DIM:(j + 1) * SSM_HEAD_DIM, :]

    y = jnp.concatenate(y_groups, axis=1) + dskip_ref[...] * xs
    y = y * _silu(z_ref[0])
    outs = []
    gw = SSM_D_INNER // SSM_N_GROUPS
    for g in range(SSM_N_GROUPS):
        yg = y[:, g * gw:(g + 1) * gw]
        outs.append(yg * lax.rsqrt(jnp.mean(yg * yg, axis=-1, keepdims=True) + SSM_NORM_EPS))
    y_ref[0] = jnp.concatenate(outs, axis=1) * ng_ref[...]


def ssd_branch(xbc, z, dtr, conv0, s0, conv_w, conv_b, dt_bias, a_log, d_skip, norm_g, *, q, t_valid,
               exact_state, name):
    nb, t, _ = xbc.shape
    n_pad = conv0.shape[1]
    pad_l = lambda v: jnp.pad(v.astype(F32), (0, LANES - v.shape[0])).reshape(1, LANES)
    expand = (jnp.arange(LANES)[:, None] == (jnp.arange(SSM_D_INNER) // SSM_HEAD_DIM)[None, :]).astype(BF16)
    row2 = lambda v: v.reshape(1, -1)
    full = lambda a: pl.BlockSpec(a.shape, lambda b, c: (0,) * a.ndim)
    consts = [conv_w, row2(conv_b), pad_l(dt_bias), pad_l(a_log), row2(jnp.repeat(d_skip, SSM_HEAD_DIM)),
              row2(norm_g), expand]
    kern = functools.partial(_ssd_kernel, q=q, t_valid=t_valid, exact_state=exact_state)
    return pl.pallas_call(
        kern,
        out_shape=(jax.ShapeDtypeStruct((nb, t, SSM_D_INNER), F32),
                   jax.ShapeDtypeStruct(s0.shape, F32)),
        grid=(nb, t // q),
        in_specs=[pl.BlockSpec((1, q, SSM_CONV_DIM), lambda b, c: (b, c, 0)),
                  pl.BlockSpec((1, q, SSM_D_INNER), lambda b, c: (b, c, 0)),
                  pl.BlockSpec((1, q, LANES), lambda b, c: (b, c, 0)),
                  pl.BlockSpec((1, n_pad, SSM_CONV_DIM), lambda b, c: (b, 0, 0)),
                  pl.BlockSpec((1,) + s0.shape[1:], lambda b, c: (b, 0, 0, 0))] + [full(a) for a in consts],
        out_specs=(pl.BlockSpec((1, q, SSM_D_INNER), lambda b, c: (b, c, 0)),
                   pl.BlockSpec((1,) + s0.shape[1:], lambda b, c: (b, 0, 0, 0))),
        scratch_shapes=[pltpu.VMEM((q + n_pad, SSM_CONV_DIM), F32)],
        compiler_params=_cparams("parallel", "arbitrary"),
        name=name,
    )(xbc, z, dtr, conv0, s0, *consts)


RWKV_GROUP_HEADS = 4
RWKV_GROUP_LANES = RWKV_GROUP_HEADS * RWKV_HEAD


def _rep_rows(x, n):
    return jnp.concatenate([x] * n, axis=0)


def _head_mask(rows_per_head, cols_per_head, n_rows, n_cols):
    rr = lax.broadcasted_iota(jnp.int32, (n_rows, n_cols), 0) // rows_per_head
    cc = lax.broadcasted_iota(jnp.int32, (n_rows, n_cols), 1) // cols_per_head
    return rr == cc


def _rwkv_prep(cur, prev, live, mu_ref, w0_ref, w2_ref, a0_ref, a2_ref, kk_ref, ka_ref, hred_ref, hexp_ref):
    rows = cur.shape[0]
    mixed = cur + (prev - cur) * mu_ref[...]
    r = mixed[:, 0:RWKV_DIM]
    k = mixed[:, RWKV_DIM:2 * RWKV_DIM]
    v = mixed[:, 2 * RWKV_DIM:3 * RWKV_DIM]
    lora = mixed[:, 3 * RWKV_DIM:]
    lane = lax.broadcasted_iota(jnp.int32, (rows, LANES), 1)
    wl = jnp.where(lane < RWKV_LORA, jnp.tanh(lora), 0.0)
    al = jnp.where(lane >= RWKV_LORA, lora, 0.0)
    w_log = -_softplus(-(w0_ref[...] + _dot_f32(wl, w2_ref[...]))) - 0.5
    a = _sigmoid(a0_ref[...] + _dot_f32(al, a2_ref[...]))
    logw = -jnp.exp(w_log)
    kk_raw = k * kk_ref[...]
    ss = _dot_exact_lhs(kk_raw * kk_raw, hred_ref[...])
    inv_n = 1.0 / jnp.maximum(jnp.sqrt(ss), 1e-12)
    kk = kk_raw * _dot_exact_lhs(inv_n, hexp_ref[...])
    k2 = k * (1.0 + (a - 1.0) * ka_ref[...])
    if live is not None:
        logw = jnp.where(live, logw, 0.0)
        kk = jnp.where(live, kk, 0.0)
        k2 = jnp.where(live, k2, 0.0)
        v = jnp.where(live, v, 0.0)
    return r, k2, v, kk, a, logw


def _rwkv_post(y, r, k2, v, rk_ref, gng_ref, gnb_ref, hred_ref, hexp_ref):
    inv_h = 1.0 / RWKV_HEAD
    mean = _dot_exact_lhs(_dot_exact_lhs(y, hred_ref[...]) * inv_h, hexp_ref[...])
    yc = y - mean
    var = _dot_exact_lhs(yc * yc, hred_ref[...]) * inv_h
    rstd = _dot_exact_lhs(lax.rsqrt(var + RWKV_GN_EPS), hexp_ref[...])
    bonus = _dot_exact_lhs(_dot_exact_lhs(r * k2 * rk_ref[...], hred_ref[...]), hexp_ref[...]) * v
    return yc * rstd * gng_ref[...] + gnb_ref[...] + bonus


def _state_to_blockdiag(stack, m_state):
    return jnp.where(m_state, jnp.concatenate([stack] * RWKV_GROUP_HEADS, axis=1), 0.0)


def _blockdiag_to_state(s_bd):
    stack = s_bd[:, 0:RWKV_HEAD]
    for hh in range(1, RWKV_GROUP_HEADS):
        stack = stack + s_bd[:, hh * RWKV_HEAD:(hh + 1) * RWKV_HEAD]
    return stack


def _rwkv_kernel(rw_ref, shift0_ref, wkv0_ref, mu_ref, w0_ref, w2_ref, a0_ref, a2_ref, kk_ref, ka_ref, rk_ref,
                 gng_ref, gnb_ref, hred_ref, hexp_ref,
                 y_ref, wkv_ref, buf_ref, s_ref, *, cq, t_valid):
    c = pl.program_id(1)
    n_chunks = pl.num_programs(1)
    n_pad = buf_ref.shape[0] - cq
    gl = RWKV_GROUP_LANES
    gh = RWKV_GROUP_HEADS
    n_groups = RWKV_DIM // gl
    nt = ((1,), (1,))
    tn = ((0,), (0,))

    m_state = _head_mask(RWKV_HEAD, RWKV_HEAD, gl, gl)
    m_exp = _head_mask(cq, RWKV_HEAD, gh * cq, gl)
    m_nbd = _head_mask(cq, cq, gh * cq, gh * cq)

    @pl.when(c == 0)
    def _():
        buf_ref[0:n_pad, :] = shift0_ref[0]
        for g in range(n_groups):
            s_ref[g] = _state_to_blockdiag(wkv0_ref[0, g * gl:(g + 1) * gl, :], m_state)

    buf_ref[n_pad:n_pad + cq, :] = rw_ref[0]
    cur = buf_ref[n_pad:n_pad + cq, :]
    prev = buf_ref[n_pad - 1:n_pad - 1 + cq, :]
    buf_ref[0:n_pad, :] = buf_ref[cq:cq + n_pad, :]
    row = lax.broadcasted_iota(jnp.int32, (cq, 1), 0) + c * cq
    r, k2, v, kk, a, logw = _rwkv_prep(cur, prev, row < t_valid, mu_ref, w0_ref, w2_ref, a0_ref, a2_ref, kk_ref,
                                       ka_ref, hred_ref, hexp_ref)

    ti = lax.broadcasted_iota(jnp.int32, (cq, cq), 0)
    si = lax.broadcasted_iota(jnp.int32, (cq, cq), 1)
    tril = (si <= ti).astype(BF16)
    lh, lm, ll = _split3(logw)
    cl = (jnp.dot(tril, lh, preferred_element_type=F32) + jnp.dot(tril, lm, preferred_element_type=F32)
          + jnp.dot(tril, ll, preferred_element_type=F32))
    p_incl = jnp.exp(cl)
    p_inv = jnp.exp(-cl)
    a_t = jnp.exp(cl - logw) * (-kk)
    r_t = p_incl * r
    b_t = p_inv * (kk * a)
    k_t = p_inv * k2
    p_end = p_incl[cq - 1:cq, :]

    tt = lax.broadcasted_iota(jnp.int32, (cq, gh * cq), 0)
    ss_i = lax.broadcasted_iota(jnp.int32, (cq, gh * cq), 1) % cq
    strict = ss_i < tt
    incl = ss_i <= tt

    def expand_rows(x):
        return jnp.where(m_exp, _rep_rows(x, gh), 0.0)

    groups = range(n_groups)
    sls = [slice(g * gl, (g + 1) * gl) for g in groups]
    lhs_ar = [jnp.concatenate([a_t[:, sl], r_t[:, sl]], axis=0) for sl in sls]
    v_exp = [expand_rows(v[:, sl]) for sl in sls]
    s_bd = [s_ref[g] for g in groups]
    ab = [_dot(lhs_ar[g], expand_rows(b_t[:, sls[g]]), nt) for g in groups]
    ak = [_dot(lhs_ar[g], expand_rows(k_t[:, sls[g]]), nt) for g in groups]
    w_s = [_dot(lhs_ar[g], s_bd[g], nt) for g in groups]
    x = [w_s[g][:cq] + _dot(jnp.where(strict, ak[g][:cq], 0.0), v_exp[g]) for g in groups]
    n_cat = [jnp.where(strict, ab[g][:cq], 0.0) for g in groups]
    n_steps = int(math.log2(cq))
    for step in range(n_steps):
        x = [x[g] + _dot(n_cat[g], expand_rows(x[g])) for g in groups]
        if step + 1 < n_steps:
            n_cat = [_dot(n_cat[g], jnp.where(m_nbd, _rep_rows(n_cat[g], gh), 0.0)) for g in groups]
    y_groups = [w_s[g][cq:] + _dot(jnp.where(incl, ab[g][cq:], 0.0), expand_rows(x[g]))
                + _dot(jnp.where(incl, ak[g][cq:], 0.0), v_exp[g]) for g in groups]
    for g in groups:
        uv = jnp.concatenate([x[g], v[:, sls[g]]], axis=0)
        bk = jnp.concatenate([b_t[:, sls[g]], k_t[:, sls[g]]], axis=0)
        inc = _dot(uv, bk, tn)
        s_ref[g] = jnp.where(m_state, (s_bd[g] + inc) * p_end[:, sls[g]], 0.0)

    y = jnp.concatenate(y_groups, axis=1)
    y_ref[0] = _rwkv_post(y, r, k2, v, rk_ref, gng_ref, gnb_ref, hred_ref, hexp_ref)

    @pl.when(c == n_chunks - 1)
    def _():
        for g in range(n_groups):
            wkv_ref[0, g * gl:(g + 1) * gl, :] = _blockdiag_to_state(s_ref[g])


def _rwkv_consts(mu, w0, w2, a0, a2, k_k, k_a, r_k, gn_g, gn_b):
    row2 = lambda a: a.reshape(1, -1).astype(F32)
    zeros_l = jnp.zeros((RWKV_LORA, RWKV_DIM), F32)
    head_of_lane = jnp.arange(RWKV_DIM) // RWKV_HEAD
    hexp = (jnp.arange(LANES)[:, None] == head_of_lane[None, :]).astype(BF16)
    return [row2(mu), row2(w0), jnp.concatenate([w2, zeros_l], axis=0), row2(a0),
            jnp.concatenate([zeros_l, a2], axis=0), row2(k_k), row2(k_a), row2(r_k), row2(gn_g), row2(gn_b),
            hexp.T, hexp]


def _rwkv_step_kernel(rw_ref, prev_ref, wkv0_ref, mu_ref, w0_ref, w2_ref, a0_ref, a2_ref, kk_ref, ka_ref, rk_ref,
                      gng_ref, gnb_ref, hred_ref, hexp_ref, y_ref, wkv_ref):
    bt = rw_ref.shape[0]
    gl = RWKV_GROUP_LANES
    n_groups = RWKV_DIM // gl
    nt = ((1,), (1,))
    tn = ((0,), (0,))
    m_state = _head_mask(RWKV_HEAD, RWKV_HEAD, gl, gl)
    r, k2, v, kk, a, logw = _rwkv_prep(rw_ref[...], prev_ref[...], None, mu_ref, w0_ref, w2_ref, a0_ref, a2_ref,
                                       kk_ref, ka_ref, hred_ref, hexp_ref)
    w = jnp.exp(logw)
    a_hat = -kk
    b = kk * a
    rowid = lax.broadcasted_iota(jnp.int32, (bt, gl), 0)
    groups = range(n_groups)
    sls = [slice(g * gl, (g + 1) * gl) for g in groups]
    y_acc = [jnp.zeros((bt, gl), F32) for _ in groups]
    for bb in range(bt):
        mine = rowid == bb
        s_bd = [_state_to_blockdiag(wkv0_ref[bb, sl, :], m_state) for sl in sls]
        u = [_dot(a_hat[:, sl], s_bd[g], nt) for g, sl in enumerate(sls)]
        s_new = []
        for g, sl in enumerate(sls):
            uv = jnp.concatenate([jnp.where(mine, u[g], 0.0), jnp.where(mine, v[:, sl], 0.0)], axis=0)
            bk = jnp.concatenate([b[:, sl], k2[:, sl]], axis=0)
            uh, um, ul = _split3(uv)
            bh, bmid, bl = _split3(bk)
            inc = _dot(jnp.concatenate([uh, uh, um, uh, ul, um], axis=0),
                       jnp.concatenate([bh, bmid, bh, bl, bh, bmid], axis=0), tn)
            s_new.append(jnp.where(m_state, s_bd[g] * w[bb:bb + 1, sl] + inc, 0.0))
        for g, sl in enumerate(sls):
            y_acc[g] = jnp.where(mine, _dot(r[:, sl], s_new[g], nt), y_acc[g])
            wkv_ref[bb, sl, :] = _blockdiag_to_state(s_new[g])
    y = jnp.concatenate(y_acc, axis=1)
    y_ref[...] = _rwkv_post(y, r, k2, v, rk_ref, gng_ref, gnb_ref, hred_ref, hexp_ref)


def rwkv_step(rw, prev, wkv0, *params, bt, name):
    nb = rw.shape[0]
    consts = _rwkv_consts(*params)
    full = lambda a: pl.BlockSpec(a.shape, lambda i: (0,) * a.ndim)
    wkv_rows = wkv0.reshape(nb, RWKV_DIM, RWKV_HEAD)
    y, wkv = pl.pallas_call(
        _rwkv_step_kernel,
        out_shape=(jax.ShapeDtypeStruct((nb, RWKV_DIM), F32), jax.ShapeDtypeStruct(wkv_rows.shape, F32)),
        grid=(nb // bt,),
        in_specs=[pl.BlockSpec((bt, RWKV_SHIFT_DIM), lambda i: (i, 0)),
                  pl.BlockSpec((bt, RWKV_SHIFT_DIM), lambda i: (i, 0)),
                  pl.BlockSpec((bt, RWKV_DIM, RWKV_HEAD), lambda i: (i, 0, 0))] + [full(a) for a in consts],
        out_specs=(pl.BlockSpec((bt, RWKV_DIM), lambda i: (i, 0)),
                   pl.BlockSpec((bt, RWKV_DIM, RWKV_HEAD), lambda i: (i, 0, 0))),
        compiler_params=_cparams("parallel"),
        name=name,
    )(rw, prev, wkv_rows, *consts)
    return y, wkv.reshape(wkv0.shape)


def rwkv_branch(rw, shift0, wkv0, *params, cq, t_valid, name):
    nb, t, _ = rw.shape
    n_pad = shift0.shape[1]
    consts = _rwkv_consts(*params)
    full = lambda a: pl.BlockSpec(a.shape, lambda b, c: (0,) * a.ndim)
    wkv_rows = wkv0.reshape(nb, RWKV_DIM, RWKV_HEAD)
    kern = functools.partial(_rwkv_kernel, cq=cq, t_valid=t_valid)
    y, wkv = pl.pallas_call(
        kern,
        out_shape=(jax.ShapeDtypeStruct((nb, t, RWKV_DIM), F32),
                   jax.ShapeDtypeStruct(wkv_rows.shape, F32)),
        grid=(nb, t // cq),
        in_specs=[pl.BlockSpec((1, cq, RWKV_SHIFT_DIM), lambda b, c: (b, c, 0)),
                  pl.BlockSpec((1, n_pad, RWKV_SHIFT_DIM), lambda b, c: (b, 0, 0)),
                  pl.BlockSpec((1, RWKV_DIM, RWKV_HEAD), lambda b, c: (b, 0, 0))] + [full(a) for a in consts],
        out_specs=(pl.BlockSpec((1, cq, RWKV_DIM), lambda b, c: (b, c, 0)),
                   pl.BlockSpec((1, RWKV_DIM, RWKV_HEAD), lambda b, c: (b, 0, 0))),
        scratch_shapes=[pltpu.VMEM((cq + n_pad, RWKV_SHIFT_DIM), F32),
                        pltpu.VMEM((RWKV_DIM // RWKV_GROUP_LANES, RWKV_GROUP_LANES, RWKV_GROUP_LANES), F32)],
        compiler_params=_cparams("parallel", "arbitrary"),
        name=name,
    )(rw, shift0, wkv_rows, *consts)
    return y, wkv.reshape(wkv0.shape)


NEG_BIG = -1e30


def _mix_kernel(g_ref, yap_ref, ybp_ref, yas_ref, ybs_ref, h_ref, wout_ref, ln2_ref, rw_ref, rb_ref,
                h1_ref, xn_ref, idx_ref, gate_ref):
    i = pl.program_id(0)
    last = i == pl.num_programs(0) - 1
    ya = jnp.where(last, yas_ref[...], yap_ref[...])
    yb = jnp.where(last, ybs_ref[...], ybp_ref[...])
    g = g_ref[...]
    merged = _sigmoid(g[:, :D_MODEL]) * ya + _sigmoid(g[:, D_MODEL:]) * yb
    h1 = h_ref[...] + jnp.dot(merged.astype(BF16), wout_ref[...], preferred_element_type=F32)
    h1_ref[...] = h1
    ms = jnp.mean(h1 * h1, axis=-1, keepdims=True)
    xn = h1 * lax.rsqrt(ms + RMS_EPS) * ln2_ref[...]
    xn_ref[...] = xn
    logits = _dot_f32(xn, rw_ref[...]) + rb_ref[...]
    lane = lax.broadcasted_iota(jnp.int32, logits.shape, 1).astype(F32)
    idx_out = jnp.zeros(logits.shape, F32)
    val_out = jnp.zeros(logits.shape, F32)
    top0 = None
    for kth in range(TOP_K):
        m = jnp.max(logits, axis=-1, keepdims=True)
        sel = jnp.min(jnp.where(logits == m, lane, float(LANES)), axis=-1, keepdims=True)
        if kth == 0:
            top0 = m
        idx_out = jnp.where(lane == kth, sel, idx_out)
        val_out = jnp.where(lane == kth, jnp.exp(m - top0), val_out)
        logits = jnp.where(lane == sel, -jnp.inf, logits)
    idx_ref[...] = idx_out.astype(jnp.int32)
    gate_ref[...] = val_out / jnp.sum(val_out, axis=-1, keepdims=True)


def mix_and_route(gates, ya_p, yb_p, ya_s, yb_s, h_rows, w_out_bf, ln2_g, router_w, router_b, tm):
    rows = h_rows.shape[0]
    n_tiles = rows // tm
    last_p = ya_p.shape[0] // tm - 1
    rw_pad = jnp.pad(router_w, ((0, 0), (0, LANES - N_EXPERTS)))
    rb_pad = jnp.pad(router_b, (0, LANES - N_EXPERTS), constant_values=NEG_BIG).reshape(1, LANES)
    row_spec = lambda n: pl.BlockSpec((tm, n), lambda i: (i, 0))
    prompt_spec = pl.BlockSpec((tm, D_MODEL), lambda i: (jnp.minimum(i, last_p), 0))
    fixed = lambda a: pl.BlockSpec(a.shape, lambda i: (0,) * a.ndim)
    return pl.pallas_call(
        _mix_kernel,
        out_shape=(jax.ShapeDtypeStruct((rows, D_MODEL), F32), jax.ShapeDtypeStruct((rows, D_MODEL), F32),
                   jax.ShapeDtypeStruct((rows, LANES), jnp.int32), jax.ShapeDtypeStruct((rows, LANES), F32)),
        grid=(n_tiles,),
        in_specs=[row_spec(2 * D_MODEL), prompt_spec, prompt_spec, fixed(ya_s), fixed(yb_s), row_spec(D_MODEL),
                  fixed(w_out_bf), pl.BlockSpec((1, D_MODEL), lambda i: (0, 0)), fixed(rw_pad), fixed(rb_pad)],
        out_specs=(row_spec(D_MODEL), row_spec(D_MODEL), row_spec(LANES), row_spec(LANES)),
        compiler_params=_cparams("parallel"),
        name="mix_and_route",
    )(gates, ya_p, yb_p, ya_s, yb_s, h_rows, w_out_bf, ln2_g.reshape(1, D_MODEL), rw_pad, rb_pad)


def _gather_rows_kernel(nb_ref, tok_ref, tok_next_ref, x_hbm, o_ref, rows_ref, sem):
    i = pl.program_id(0)
    n_used = nb_ref[0]
    slot = lax.rem(i, 2)

    def copies(idx_ref, buf):
        return [pltpu.make_async_copy(x_hbm.at[pl.ds(idx_ref[0, 0, r], 1), :],
                                      rows_ref.at[buf, pl.ds(r, 1), :], sem.at[buf]) for r in range(MOE_BLOCK)]

    @pl.when(jnp.logical_and(i == 0, n_used > 0))
    def _():
        for cp in copies(tok_ref, 0):
            cp.start()

    @pl.when(i + 1 < n_used)
    def _():
        for cp in copies(tok_next_ref, 1 - slot):
            cp.start()

    @pl.when(i < n_used)
    def _():
        for cp in copies(tok_ref, slot):
            cp.wait()
        o_ref[...] = rows_ref[slot].astype(o_ref.dtype)

    @pl.when(i >= n_used)
    def _():
        o_ref[...] = jnp.zeros_like(o_ref)


def gather_rows(x, slot_tok, n_used, out_dtype):
    n_blocks = slot_tok.shape[0] // MOE_BLOCK
    d = x.shape[1]
    tok3 = slot_tok.reshape(n_blocks, 1, MOE_BLOCK)
    grid_spec = pltpu.PrefetchScalarGridSpec(
        num_scalar_prefetch=1,
        grid=(n_blocks,),
        in_specs=[pl.BlockSpec((1, 1, MOE_BLOCK), lambda i, nb: (i, 0, 0), memory_space=pltpu.SMEM),
                  pl.BlockSpec((1, 1, MOE_BLOCK), lambda i, nb: (jnp.minimum(i + 1, n_blocks - 1), 0, 0),
                               memory_space=pltpu.SMEM),
                  pl.BlockSpec(memory_space=pl.ANY)],
        out_specs=pl.BlockSpec((MOE_BLOCK, d), lambda i, nb: (i, 0)),
        scratch_shapes=[pltpu.VMEM((2, MOE_BLOCK, d), x.dtype), pltpu.SemaphoreType.DMA((2,))],
    )
    return pl.pallas_call(
        _gather_rows_kernel,
        out_shape=jax.ShapeDtypeStruct((n_blocks * MOE_BLOCK, d), out_dtype),
        grid_spec=grid_spec,
        compiler_params=_cparams("arbitrary"),
        name="moe_gather",
    )(n_used, tok3, tok3, x)


def expert_run_tables(block_e):
    n = block_e.shape[0]
    idx = jnp.arange(n, dtype=jnp.int32)
    change = jnp.concatenate([jnp.ones((1,), bool), block_e[1:] != block_e[:-1]])
    run_id = jnp.cumsum(change.astype(jnp.int32)) - 1
    later_change = jnp.concatenate([jnp.where(change, idx, n)[1:], jnp.full((1,), n, jnp.int32)])
    next_pos = lax.cummin(later_change, axis=0, reverse=True)
    next_e = jnp.where(next_pos < n, block_e[jnp.minimum(next_pos, n - 1)], -1).astype(jnp.int32)
    return run_id.astype(jnp.int32), next_e, run_id[-1:] + 1


def _stream_expert_weights(be_ref, nxt_ref, run_ref, meta_ref, make_copies, cast):
    j = pl.program_id(0)
    i = pl.program_id(1)
    n_tiles = pl.num_programs(0)
    n_runs = meta_ref[1]
    first = jnp.logical_or(i == 0, be_ref[i] != be_ref[jnp.maximum(i, 1) - 1])
    buf = lax.rem(j * n_runs + run_ref[i], 2)

    @pl.when(jnp.logical_and(first, jnp.logical_and(i == 0, j == 0)))
    def _():
        for cp in make_copies(be_ref[0], 0, 0):
            cp.start()

    @pl.when(first)
    def _():
        for cp in make_copies(be_ref[i], j, buf):
            cp.wait()
        cast(buf)

    more_here = nxt_ref[i] >= 0

    @pl.when(jnp.logical_and(first, more_here))
    def _():
        for cp in make_copies(nxt_ref[i], j, 1 - buf):
            cp.start()

    @pl.when(jnp.logical_and(first, jnp.logical_and(jnp.logical_not(more_here), j + 1 < n_tiles)))
    def _():
        for cp in make_copies(be_ref[0], j + 1, 1 - buf):
            cp.start()


def _moe_up_kernel(be_ref, nxt_ref, run_ref, meta_ref, x_ref, w_hbm, bg_ref, bu_ref, o_ref, w_f32, w_bf, sem):
    i = pl.program_id(1)
    tn = o_ref.shape[1]

    def make_copies(e, tile, buf):
        return [pltpu.make_async_copy(w_hbm.at[e, :, pl.ds(pl.multiple_of(half * D_FF + tile * tn, LANES), tn)],
                                      w_f32.at[buf, half], sem.at[buf, half]) for half in range(2)]

    def cast(buf):
        w_bf[...] = w_f32[buf].astype(BF16)

    _stream_expert_weights(be_ref, nxt_ref, run_ref, meta_ref, make_copies, cast)

    @pl.when(i < meta_ref[0])
    def _():
        x = x_ref[...]
        g = jnp.dot(x, w_bf[0], preferred_element_type=F32) + bg_ref[0]
        u = jnp.dot(x, w_bf[1], preferred_element_type=F32) + bu_ref[0]
        g = jnp.minimum(g, SWIGLU_LIMIT)
        u = jnp.clip(u, -SWIGLU_LIMIT, SWIGLU_LIMIT)
        o_ref[...] = ((u + 1.0) * (g * _sigmoid(g * SWIGLU_ALPHA))).astype(o_ref.dtype)

    @pl.when(i >= meta_ref[0])
    def _():
        o_ref[...] = jnp.zeros_like(o_ref)


def moe_up(xb, block_e, next_e, run_id, meta, w_gate_up, b_gate_up, tn):
    n_blocks = xb.shape[0] // MOE_BLOCK
    n_ff_tiles = D_FF // tn
    b3 = b_gate_up.reshape(N_EXPERTS, 1, 2 * D_FF)
    grid_spec = pltpu.PrefetchScalarGridSpec(
        num_scalar_prefetch=4,
        grid=(n_ff_tiles, n_blocks),
        in_specs=[pl.BlockSpec((MOE_BLOCK, D_MODEL), lambda j, i, be, nx, rn, mt: (i, 0)),
                  pl.BlockSpec(memory_space=pl.ANY),
                  pl.BlockSpec((1, 1, tn), lambda j, i, be, nx, rn, mt: (be[i], 0, j)),
                  pl.BlockSpec((1, 1, tn), lambda j, i, be, nx, rn, mt: (be[i], 0, j + n_ff_tiles))],
        out_specs=pl.BlockSpec((MOE_BLOCK, tn), lambda j, i, be, nx, rn, mt: (i, j)),
        scratch_shapes=[pltpu.VMEM((2, 2, D_MODEL, tn), F32), pltpu.VMEM((2, D_MODEL, tn), BF16),
                        pltpu.SemaphoreType.DMA((2, 2))],
    )
    return pl.pallas_call(
        _moe_up_kernel,
        out_shape=jax.ShapeDtypeStruct((n_blocks * MOE_BLOCK, D_FF), BF16),
        grid_spec=grid_spec,
        compiler_params=_cparams("arbitrary", "arbitrary"),
        name="moe_up",
    )(block_e, next_e, run_id, meta, xb, w_gate_up, b3, b3)


def _moe_down_kernel(be_ref, nxt_ref, run_ref, meta_ref, h_ref, w_hbm, bd_ref, o_ref, w_f32, w_bf, sem):
    i = pl.program_id(1)
    tn = o_ref.shape[1]

    def make_copies(e, tile, buf):
        return [pltpu.make_async_copy(w_hbm.at[e, :, pl.ds(pl.multiple_of(tile * tn, LANES), tn)],
                                      w_f32.at[buf], sem.at[buf])]

    def cast(buf):
        w_bf[...] = w_f32[buf].astype(BF16)

    _stream_expert_weights(be_ref, nxt_ref, run_ref, meta_ref, make_copies, cast)

    @pl.when(i < meta_ref[0])
    def _():
        o_ref[...] = jnp.dot(h_ref[...], w_bf[...], preferred_element_type=F32) + bd_ref[0]

    @pl.when(i >= meta_ref[0])
    def _():
        o_ref[...] = jnp.zeros_like(o_ref)


def moe_down(hb, block_e, next_e, run_id, meta, w_down, b_down, tn):
    n_blocks = hb.shape[0] // MOE_BLOCK
    grid_spec = pltpu.PrefetchScalarGridSpec(
        num_scalar_prefetch=4,
        grid=(D_MODEL // tn, n_blocks),
        in_specs=[pl.BlockSpec((MOE_BLOCK, D_FF), lambda j, i, be, nx, rn, mt: (i, 0)),
                  pl.BlockSpec(memory_space=pl.ANY),
                  pl.BlockSpec((1, 1, tn), lambda j, i, be, nx, rn, mt: (be[i], 0, j))],
        out_specs=pl.BlockSpec((MOE_BLOCK, tn), lambda j, i, be, nx, rn, mt: (i, j)),
        scratch_shapes=[pltpu.VMEM((2, D_FF, tn), F32), pltpu.VMEM((D_FF, tn), BF16),
                        pltpu.SemaphoreType.DMA((2,))],
    )
    return pl.pallas_call(
        _moe_down_kernel,
        out_shape=jax.ShapeDtypeStruct((n_blocks * MOE_BLOCK, D_MODEL), F32),
        grid_spec=grid_spec,
        compiler_params=_cparams("arbitrary", "arbitrary"),
        name="moe_down",
    )(block_e, next_e, run_id, meta, hb, w_down, b_down.reshape(N_EXPERTS, 1, D_MODEL))


def _combine_kernel(slot_ref, slot_next_ref, row0_ref, gate_ref, lnf_ref, yb_hbm, h_hbm, o_ref, rows_ref, h_vmem,
                    sem, hsem):
    i = pl.program_id(0)
    n_tiles = pl.num_programs(0)
    tm = o_ref.shape[-2]
    slot = lax.rem(i, 2)

    def copies(idx_ref, tile, buf):
        row0 = pl.multiple_of(row0_ref[tile], SUBLANES)
        cps = [pltpu.make_async_copy(h_hbm.at[pl.ds(row0, tm), :], h_vmem.at[buf], hsem.at[buf])]
        for n in range(TOP_K * tm):
            cps.append(pltpu.make_async_copy(yb_hbm.at[pl.ds(idx_ref[0, 0, n], 1), :],
                                             rows_ref.at[buf, n // tm, pl.ds(n % tm, 1), :], sem.at[buf]))
        return cps

    @pl.when(i == 0)
    def _():
        for cp in copies(slot_ref, 0, 0):
            cp.start()

    @pl.when(i + 1 < n_tiles)
    def _():
        for cp in copies(slot_next_ref, jnp.minimum(i + 1, n_tiles - 1), 1 - slot):
            cp.start()

    for cp in copies(slot_ref, i, slot):
        cp.wait()
    gate = gate_ref[...]
    acc = h_vmem[slot]
    for k in range(TOP_K):
        acc = acc + rows_ref[slot, k] * gate[:, k:k + 1]
    ms = jnp.mean(acc * acc, axis=-1, keepdims=True)
    o_ref[...] = (acc * lax.rsqrt(ms + RMS_EPS) * lnf_ref[...]).reshape(o_ref.shape)


def moe_combine(yb, h1, slot_of_pair, gates, row0, lnf_g, out_shape, out_index_map, tm):
    n_tiles = row0.shape[0]
    grid_spec = pltpu.PrefetchScalarGridSpec(
        num_scalar_prefetch=0,
        grid=(n_tiles,),
        in_specs=[pl.BlockSpec((1, 1, TOP_K * tm), lambda i: (i, 0, 0), memory_space=pltpu.SMEM),
                  pl.BlockSpec((1, 1, TOP_K * tm), lambda i: (jnp.minimum(i + 1, n_tiles - 1), 0, 0),
                               memory_space=pltpu.SMEM),
                  pl.BlockSpec(memory_space=pltpu.SMEM),
                  pl.BlockSpec((tm, LANES), lambda i: (i, 0)),
                  pl.BlockSpec((1, D_MODEL), lambda i: (0, 0)),
                  pl.BlockSpec(memory_space=pl.ANY),
                  pl.BlockSpec(memory_space=pl.ANY)],
        out_specs=pl.BlockSpec(out_shape[0], out_index_map),
        scratch_shapes=[pltpu.VMEM((2, TOP_K, tm, D_MODEL), F32), pltpu.VMEM((2, tm, D_MODEL), F32),
                        pltpu.SemaphoreType.DMA((2,)), pltpu.SemaphoreType.DMA((2,))],
    )
    return pl.pallas_call(
        _combine_kernel,
        out_shape=jax.ShapeDtypeStruct(out_shape[1], F32),
        grid_spec=grid_spec,
        compiler_params=_cparams("arbitrary"),
        name="moe_combine",
    )(slot_of_pair, slot_of_pair, row0, gates, lnf_g.reshape(1, D_MODEL), yb, h1)


PROMPT_PAD_T = 2176
SAMPLE_PAD_T = SUBLANES
ROW_TILE = 1104


def kernel(x_prompt, x_sample, state_ssm_conv, state_ssm, state_rwkv_shift, state_rwkv_wkv, meta_tokens, ln1_g, w_in, ssm_conv_w, ssm_conv_b, ssm_dt_bias, ssm_A_log, ssm_D, ssm_norm_g, rwkv_mu, rwkv_w0, rwkv_w2, rwkv_a0, rwkv_a2, rwkv_k_k, rwkv_k_a, rwkv_r_k, rwkv_gn_g, rwkv_gn_b, w_out, ln2_g, router_w, router_b, w_gate_up, b_gate_up, w_down, b_down, lnf_g):
    bp, seq, d = x_prompt.shape
    bs = x_sample.shape[0]
    t_prompt = N_META + seq
    meta = jnp.broadcast_to(meta_tokens[None], (bp, N_META, d))
    hp = jnp.concatenate([meta, x_prompt, jnp.zeros((bp, PROMPT_PAD_T - t_prompt, d), F32)], axis=1)
    h_rows = jnp.concatenate([hp.reshape(bp * PROMPT_PAD_T, d), x_sample.reshape(bs, d)], axis=0)
    n_prow = bp * PROMPT_PAD_T

    l = 0
    xn = rmsnorm_rows(h_rows, ln1_g[l], BF16, ROW_TILE)
    w = w_in[l]
    o_xbc = SSM_D_INNER
    o_dt = o_xbc + SSM_CONV_DIM
    o_rw = o_dt + SSM_N_HEADS
    o_g = o_rw + RWKV_SHIFT_DIM
    w_z = w[:, :o_xbc].astype(BF16)
    w_xbc = w[:, o_xbc:o_dt].astype(BF16)
    w_dt = jnp.pad(w[:, o_dt:o_rw], ((0, 0), (0, LANES - SSM_N_HEADS))).astype(BF16)
    w_rw = w[:, o_rw:o_g].astype(BF16)
    w_g = w[:, o_g:].astype(BF16)
    z = matmul_bf16(xn, w_z, ROW_TILE, 1024, "proj_z")
    xbc = matmul_bf16(xn, w_xbc, ROW_TILE, 1024, "proj_xbc")
    dtr = matmul_bf16(xn, w_dt, ROW_TILE, LANES, "proj_dt")

    def split_rows(a):
        n = a.shape[1]
        a_s = jnp.pad(a[n_prow:].reshape(bs, 1, n), ((0, 0), (0, SAMPLE_PAD_T - 1), (0, 0)))
        return a[:n_prow].reshape(bp, PROMPT_PAD_T, n), a_s

    z_p, z_s = split_rows(z)
    xbc_p, xbc_s = split_rows(xbc)
    dtr_p, dtr_s = split_rows(dtr)
    ssm_args = (ssm_conv_w[l], ssm_conv_b[l], ssm_dt_bias[l], ssm_A_log[l], ssm_D[l], ssm_norm_g[l])
    conv0_p = jnp.zeros((bp, SUBLANES, SSM_CONV_DIM), F32)
    s0_p = jnp.zeros((bp, SSM_N_HEADS, SSM_HEAD_DIM, SSM_D_STATE), F32)
    ya_p, ssm_p = ssd_branch(xbc_p, z_p, dtr_p, conv0_p, s0_p, *ssm_args, q=SSM_CHUNK, t_valid=t_prompt,
                             exact_state=False, name="ssd_prompt")
    conv0_s = jnp.pad(state_ssm_conv[l], ((0, 0), (SUBLANES - (SSM_CONV - 1), 0), (0, 0)))
    ya_s, ssm_s = ssd_branch(xbc_s, z_s, dtr_s, conv0_s, state_ssm[l], *ssm_args, q=SAMPLE_PAD_T, t_valid=1,
                             exact_state=True, name="ssd_sample")
    rw = matmul_bf16(xn, w_rw, ROW_TILE, 896, "proj_rw")
    rw_p = rw[:n_prow].reshape(bp, PROMPT_PAD_T, RWKV_SHIFT_DIM)
    rw_s = rw[n_prow:]
    rwkv_args = (rwkv_mu[l], rwkv_w0[l], rwkv_w2[l], rwkv_a0[l], rwkv_a2[l], rwkv_k_k[l], rwkv_k_a[l],
                 rwkv_r_k[l], rwkv_gn_g[l], rwkv_gn_b[l])
    shift0_p = jnp.zeros((bp, SUBLANES, RWKV_SHIFT_DIM), F32)
    wkv0_p = jnp.zeros((bp, RWKV_N_HEADS, RWKV_HEAD, RWKV_HEAD), F32)
    yb_p, wkv_p = rwkv_branch(rw_p, shift0_p, wkv0_p, *rwkv_args, cq=RWKV_CHUNK, t_valid=t_prompt,
                              name="rwkv_prompt")
    yb_s, wkv_s = rwkv_step(rw_s, state_rwkv_shift[l], state_rwkv_wkv[l], *rwkv_args, bt=SUBLANES,
                            name="rwkv_sample")
    gates = matmul_bf16(xn, w_g, ROW_TILE, 1024, "proj_gates")
    h1, xn2, top_idx, top_gate = mix_and_route(
        gates, ya_p.reshape(n_prow, d), yb_p.reshape(n_prow, d), ya_s[:, 0], yb_s, h_rows,
        w_out[l].astype(BF16), ln2_g[l], router_w[l], router_b[l], MOE_BLOCK)

    n_rows = h_rows.shape[0]
    row_id = jnp.arange(n_rows, dtype=jnp.int32)
    valid = jnp.logical_or(row_id >= n_prow, row_id % PROMPT_PAD_T < t_prompt)
    yb, slot_of_pair = moe_expert_rows(xn2, top_idx, valid, (bp * t_prompt + bs) * TOP_K, w_gate_up[l],
                                       b_gate_up[l], w_down[l], b_down[l])

    def combine(row0, out_block, out_full, out_map):
        rows = (row0[:, None] + jnp.arange(MOE_BLOCK, dtype=jnp.int32)[None, :])
        slots = jnp.transpose(slot_of_pair[rows], (0, 2, 1)).reshape(row0.shape[0], 1, TOP_K * MOE_BLOCK)
        return moe_combine(yb, h1, slots, top_gate[rows.reshape(-1)], row0, lnf_g, (out_block, out_full),
                           out_map, MOE_BLOCK)

    tiles_per_seq = seq // MOE_BLOCK
    tile_id = jnp.arange(bp * tiles_per_seq, dtype=jnp.int32)
    row0_p = (tile_id // tiles_per_seq) * PROMPT_PAD_T + N_META + (tile_id % tiles_per_seq) * MOE_BLOCK
    y_prompt = combine(row0_p, (1, MOE_BLOCK, d), (bp, seq, d),
                       lambda i: (i // tiles_per_seq, i % tiles_per_seq, 0))
    y_sample = combine(jnp.full((1,), n_prow, jnp.int32), (MOE_BLOCK, d), (bs, d), lambda i: (i, 0))

    t_last = t_prompt - 1
    prompt_conv = xbc_p[:, t_prompt - (SSM_CONV - 1):t_prompt]
    sample_conv = jnp.concatenate([state_ssm_conv[l][:, 1:], xbc_s[:, 0:1]], axis=1)
    return (y_prompt, y_sample.reshape(bs, 1, d), prompt_conv[None], ssm_p[None], rw_p[:, t_last][None], wkv_p[None],
            sample_conv[None], ssm_s[None], rw_s[None], wkv_s[None])


def moe_expert_rows(xn2, top_idx, valid, n_valid_pairs, w_gate_up, b_gate_up, w_down, b_down):
    n_rows = xn2.shape[0]
    key = jnp.where(valid[:, None], top_idx[:, :TOP_K], N_EXPERTS).reshape(-1)
    n_pairs = key.shape[0]
    n_blocks = -(-n_valid_pairs // MOE_BLOCK) + N_EXPERTS
    order = jnp.argsort(key, stable=True).astype(jnp.int32)
    counts = jnp.sum((key[:, None] == jnp.arange(N_EXPERTS, dtype=jnp.int32)[None, :]).astype(jnp.int32), axis=0)
    padded = (counts + MOE_BLOCK - 1) // MOE_BLOCK * MOE_BLOCK
    ends = jnp.cumsum(padded)
    pstart = ends - padded
    start = jnp.cumsum(counts) - counts
    inv = jnp.zeros((n_pairs,), jnp.int32).at[order].set(jnp.arange(n_pairs, dtype=jnp.int32), unique_indices=True)
    e_clip = jnp.minimum(key, N_EXPERTS - 1)
    slot_of_pair = jnp.where(key < N_EXPERTS, pstart[e_clip] + inv - start[e_clip], 0).reshape(n_rows, TOP_K)
    block_e = jnp.minimum(jnp.searchsorted(ends, jnp.arange(n_blocks, dtype=jnp.int32) * MOE_BLOCK, side='right'),
                          N_EXPERTS - 1).astype(jnp.int32)
    slot_id = jnp.arange(n_blocks * MOE_BLOCK, dtype=jnp.int32)
    e_slot = block_e[slot_id // MOE_BLOCK]
    rank = slot_id - pstart[e_slot]
    in_use = jnp.logical_and(rank >= 0, rank < counts[e_slot])
    src = jnp.clip(start[e_slot] + rank, 0, n_pairs - 1)
    slot_tok = jnp.where(in_use, order[src] // TOP_K, 0).astype(jnp.int32)
    n_used = (ends[-1:] // MOE_BLOCK).astype(jnp.int32)

    run_id, next_e, n_runs = expert_run_tables(block_e)
    meta = jnp.concatenate([n_used, n_runs]).astype(jnp.int32)
    xb = gather_rows(xn2, slot_tok, n_used, BF16)
    hb = moe_up(xb, block_e, next_e, run_id, meta, w_gate_up, b_gate_up, 1024)
    yb = moe_down(hb, block_e, next_e, run_id, meta, w_down, b_down, 1024)
    return yb, slot_of_pair
```

```python
import functools
import math

import jax
import jax.numpy as jnp
from jax import lax
from jax.experimental import pallas as pl
from jax.experimental.pallas import tpu as pltpu

F32 = jnp.float32
BF16 = jnp.bfloat16

V7X_VMEM_LIMIT_BYTES = 60000 * 1024
LANES = 128
SUBLANES = 8

D_MODEL = 2048
N_META = 16
RMS_EPS = 1e-6
SSM_D_INNER = 2048
SSM_HEAD_DIM = 64
SSM_N_HEADS = 32
SSM_N_GROUPS = 4
SSM_HPG = 8
SSM_D_STATE = 128
SSM_CONV = 4
SSM_CHUNK = 128
SSM_CONV_DIM = SSM_D_INNER + 2 * SSM_N_GROUPS * SSM_D_STATE
SSM_NORM_EPS = 1e-5
RWKV_DIM = 2048
RWKV_HEAD = 64
RWKV_N_HEADS = 32
RWKV_LORA = 64
RWKV_GN_EPS = 64e-5
RWKV_SHIFT_DIM = 3 * RWKV_DIM + 2 * RWKV_LORA
RWKV_CHUNK = 64
N_EXPERTS = 32
TOP_K = 4
D_FF = 2048
SWIGLU_LIMIT = 7.0
SWIGLU_ALPHA = 1.702
MOE_BLOCK = 128


def _cparams(*sem):
    return pltpu.CompilerParams(dimension_semantics=sem, vmem_limit_bytes=V7X_VMEM_LIMIT_BYTES)


def _split3(x):
    hi = x.astype(BF16)
    r1 = x - hi.astype(F32)
    mid = r1.astype(BF16)
    lo = (r1 - mid.astype(F32)).astype(BF16)
    return hi, mid, lo


def _dot(a, b, dims=None):
    a = a.astype(BF16)
    b = b.astype(BF16)
    if dims is None:
        return jnp.dot(a, b, preferred_element_type=F32)
    return lax.dot_general(a, b, (dims, ((), ())), preferred_element_type=F32)


def _dot_exact_lhs(x, m01):
    hi, mid, lo = _split3(x)
    m = m01.astype(BF16)
    return (jnp.dot(hi, m, preferred_element_type=F32) + jnp.dot(mid, m, preferred_element_type=F32)
            + jnp.dot(lo, m, preferred_element_type=F32))


def _dot_f32(a, b):
    ah, am, al = _split3(a)
    bh, bm, bl = _split3(b)
    lhs = jnp.concatenate([ah, ah, am, ah, al, am], axis=1)
    rhs = jnp.concatenate([bh, bm, bh, bl, bh, bm], axis=0)
    return jnp.dot(lhs, rhs, preferred_element_type=F32)


def _softplus(x):
    return jnp.maximum(x, 0.0) + jnp.log(1.0 + jnp.exp(-jnp.abs(x)))


def _sigmoid(x):
    return 1.0 / (1.0 + jnp.exp(-x))


def _silu(x):
    return x * _sigmoid(x)


def _rmsnorm_kernel(x_ref, g_ref, o_ref, *, eps):
    x = x_ref[...]
    ms = jnp.mean(x * x, axis=-1, keepdims=True)
    o_ref[...] = (x * lax.rsqrt(ms + eps) * g_ref[...]).astype(o_ref.dtype)


def rmsnorm_rows(x, g, out_dtype, tm):
    rows, d = x.shape
    return pl.pallas_call(
        functools.partial(_rmsnorm_kernel, eps=RMS_EPS),
        out_shape=jax.ShapeDtypeStruct((rows, d), out_dtype),
        grid=(rows // tm,),
        in_specs=[pl.BlockSpec((tm, d), lambda i: (i, 0)), pl.BlockSpec((1, d), lambda i: (0, 0))],
        out_specs=pl.BlockSpec((tm, d), lambda i: (i, 0)),
        compiler_params=_cparams("parallel"),
        name="rmsnorm_rows",
    )(x, g.reshape(1, d))


def _matmul_kernel(x_ref, w_ref, o_ref):
    o_ref[...] = jnp.dot(x_ref[...], w_ref[...], preferred_element_type=F32)


def matmul_bf16(x, w, tm, tn, name):
    rows, k = x.shape
    n = w.shape[1]
    return pl.pallas_call(
        _matmul_kernel,
        out_shape=jax.ShapeDtypeStruct((rows, n), F32),
        grid=(n // tn, rows // tm),
        in_specs=[pl.BlockSpec((tm, k), lambda j, i: (i, 0)), pl.BlockSpec((k, tn), lambda j, i: (0, j))],
        out_specs=pl.BlockSpec((tm, tn), lambda j, i: (i, j)),
        compiler_params=_cparams("parallel", "parallel"),
        name=name,
    )(x, w)


def _ssd_kernel(xbc_ref, z_ref, dtr_ref, conv0_ref, s0_ref, convw_ref, convb_ref, dtb_ref, alog_ref,
                dskip_ref, ng_ref, expand_ref, y_ref, s_ref, buf_ref, *, q, t_valid, exact_state):
    c = pl.program_id(1)
    n_pad = buf_ref.shape[0] - q

    @pl.when(c == 0)
    def _():
        buf_ref[0:n_pad, :] = conv0_ref[0]
        s_ref[0] = s0_ref[0]

    buf_ref[n_pad:n_pad + q, :] = xbc_ref[...]
    acc = convb_ref[...] + convw_ref[SSM_CONV - 1:SSM_CONV, :] * buf_ref[n_pad:n_pad + q, :]
    for k in range(SSM_CONV - 1):
        off = n_pad - (SSM_CONV - 1) + k
        acc = acc + convw_ref[k:k + 1, :] * buf_ref[off:off + q, :]
    buf_ref[0:n_pad, :] = buf_ref[q:q + n_pad, :]
    xbc = _silu(acc)
    gn = SSM_N_GROUPS * SSM_D_STATE
    xs = xbc[:, :SSM_D_INNER]
    bm = xbc[:, SSM_D_INNER:SSM_D_INNER + gn]
    cm = xbc[:, SSM_D_INNER + gn:]

    row = lax.broadcasted_iota(jnp.int32, (q, LANES), 0) + c * q
    lane = lax.broadcasted_iota(jnp.int32, (q, LANES), 1)
    live = jnp.logical_and(row < t_valid, lane < SSM_N_HEADS)
    dt = jnp.where(live, _softplus(dtr_ref[...] + dtb_ref[...]), 0.0)
    da = dt * (-jnp.exp(alog_ref[...]))

    ti = lax.broadcasted_iota(jnp.int32, (q, q), 0)
    si = lax.broadcasted_iota(jnp.int32, (q, q), 1)
    causal = si <= ti
    tril = causal.astype(BF16)
    dh, dm, dl = _split3(da)
    cs = (jnp.dot(tril, dh, preferred_element_type=F32) + jnp.dot(tril, dm, preferred_element_type=F32)
          + jnp.dot(tril, dl, preferred_element_type=F32))
    eye = (lax.broadcasted_iota(jnp.int32, (LANES, LANES), 0)
           == lax.broadcasted_iota(jnp.int32, (LANES, LANES), 1)).astype(BF16)
    ch, cmid, cl = _split3(cs)
    nt = (((1,), (1,)), ((), ()))
    cs_t = (lax.dot_general(eye, ch, nt, preferred_element_type=F32)
            + lax.dot_general(eye, cmid, nt, preferred_element_type=F32)
            + lax.dot_general(eye, cl, nt, preferred_element_type=F32))

    expand = expand_ref[...]
    dt_x = _dot_exact_lhs(dt, expand)
    cs_x = _dot_exact_lhs(cs, expand)
    xdt = xs * dt_x
    ecs = jnp.exp(cs_x)
    dte = jnp.exp(cs_x[q - 1:q, :] - cs_x)

    lane_in_pair = lax.broadcasted_iota(jnp.int32, (q, LANES), 1)
    lo_half = lane_in_pair < SSM_HEAD_DIM

    y_groups = []
    for g in range(SSM_N_GROUPS):
        b_g = bm[:, g * SSM_D_STATE:(g + 1) * SSM_D_STATE]
        c_g = cm[:, g * SSM_D_STATE:(g + 1) * SSM_D_STATE]
        cb = _dot(c_g, b_g, ((1,), (1,)))
        gsl = slice(g * SSM_HPG * SSM_HEAD_DIM, (g + 1) * SSM_HPG * SSM_HEAD_DIM)
        s_g = s_ref[0, g * SSM_HPG:(g + 1) * SSM_HPG].reshape(SSM_HPG * SSM_HEAD_DIM, SSM_D_STATE)
        y_off = _dot(c_g, s_g, ((1,), (1,))) * ecs[:, gsl]
        slabs = []
        for m in range(SSM_HPG // 2):
            lsl = slice(gsl.start + m * LANES, gsl.start + (m + 1) * LANES)
            x_pair = xdt[:, lsl]
            acc_pair = None
            for half in range(2):
                h = g * SSM_HPG + 2 * m + half
                seg = cs[:, h:h + 1] - cs_t[h:h + 1, :]
                lmat = jnp.where(causal, jnp.exp(jnp.where(causal, seg, 0.0)), 0.0)
                x_half = jnp.where(lo_half if half == 0 else jnp.logical_not(lo_half), x_pair, 0.0)
                contrib = _dot(cb * lmat, x_half)
                acc_pair = contrib if acc_pair is None else acc_pair + contrib
            slabs.append(acc_pair)
        y_groups.append(jnp.concatenate(slabs, axis=1) + y_off)

        xw = xdt[:, gsl] * dte[:, gsl]
        tn = ((0,), (0,))
        if exact_state:
            xh, xm, xl = _split3(xw)
            bh, bmid, bl = _split3(b_g)
            upd = _dot(jnp.concatenate([xh, xh, xm, xh, xl, xm], axis=0),
                       jnp.concatenate([bh, bmid, bh, bl, bh, bmid], axis=0), tn)
        else:
            upd = _dot(xw, b_g, tn)
        for j in range(SSM_HPG):
            h = g * SSM_HPG + j
            dec = jnp.exp(cs_t[h:h + 1, q - 1:q])
            s_ref[0, h] = s_ref[0, h] * dec + upd[j * SSM_HEAD_DIM:(j + 1) * SSM_HEAD_DIM, :]

    y = jnp.concatenate(y_groups, axis=1) + dskip_ref[...] * xs
    y = y * _silu(z_ref[...])
    outs = []
    gw = SSM_D_INNER // SSM_N_GROUPS
    for g in range(SSM_N_GROUPS):
        yg = y[:, g * gw:(g + 1) * gw]
        outs.append(yg * lax.rsqrt(jnp.mean(yg * yg, axis=-1, keepdims=True) + SSM_NORM_EPS))
    y_ref[...] = jnp.concatenate(outs, axis=1) * ng_ref[...]


def ssd_branch(xbc, z, dtr, conv0, s0, conv_w, conv_b, dt_bias, a_log, d_skip, norm_g, *, t, q, t_valid,
               exact_state, name):
    nb = s0.shape[0]
    cps = t // q
    n_pad = conv0.shape[1]
    pad_l = lambda v: jnp.pad(v.astype(F32), (0, LANES - v.shape[0])).reshape(1, LANES)
    expand = (jnp.arange(LANES)[:, None] == (jnp.arange(SSM_D_INNER) // SSM_HEAD_DIM)[None, :]).astype(BF16)
    row2 = lambda v: v.reshape(1, -1)
    full = lambda a: pl.BlockSpec(a.shape, lambda b, c: (0,) * a.ndim)
    consts = [conv_w, row2(conv_b), pad_l(dt_bias), pad_l(a_log), row2(jnp.repeat(d_skip, SSM_HEAD_DIM)),
              row2(norm_g), expand]
    kern = functools.partial(_ssd_kernel, q=q, t_valid=t_valid, exact_state=exact_state)
    return pl.pallas_call(
        kern,
        out_shape=(jax.ShapeDtypeStruct((nb * t, SSM_D_INNER), F32),
                   jax.ShapeDtypeStruct(s0.shape, F32)),
        grid=(nb, cps),
        in_specs=[pl.BlockSpec((q, SSM_CONV_DIM), lambda b, c: (b * cps + c, 0)),
                  pl.BlockSpec((q, SSM_D_INNER), lambda b, c: (b * cps + c, 0)),
                  pl.BlockSpec((q, LANES), lambda b, c: (b * cps + c, 0)),
                  pl.BlockSpec((1, n_pad, SSM_CONV_DIM), lambda b, c: (b, 0, 0)),
                  pl.BlockSpec((1,) + s0.shape[1:], lambda b, c: (b, 0, 0, 0))] + [full(a) for a in consts],
        out_specs=(pl.BlockSpec((q, SSM_D_INNER), lambda b, c: (b * cps + c, 0)),
                   pl.BlockSpec((1,) + s0.shape[1:], lambda b, c: (b, 0, 0, 0))),
        scratch_shapes=[pltpu.VMEM((q + n_pad, SSM_CONV_DIM), F32)],
        compiler_params=_cparams("parallel", "arbitrary"),
        name=name,
    )(xbc, z, dtr, conv0, s0, *consts)


RWKV_GROUP_HEADS = 4
RWKV_GROUP_LANES = RWKV_GROUP_HEADS * RWKV_HEAD


def _rep_rows(x, n):
    return jnp.concatenate([x] * n, axis=0)


def _head_mask(rows_per_head, cols_per_head, n_rows, n_cols):
    rr = lax.broadcasted_iota(jnp.int32, (n_rows, n_cols), 0) // rows_per_head
    cc = lax.broadcasted_iota(jnp.int32, (n_rows, n_cols), 1) // cols_per_head
    return rr == cc


def _rwkv_prep(cur, prev, live, mu_ref, w0_ref, w2_ref, a0_ref, a2_ref, kk_ref, ka_ref, hred_ref, hexp_ref):
    rows = cur.shape[0]
    mixed = cur + (prev - cur) * mu_ref[...]
    r = mixed[:, 0:RWKV_DIM]
    k = mixed[:, RWKV_DIM:2 * RWKV_DIM]
    v = mixed[:, 2 * RWKV_DIM:3 * RWKV_DIM]
    lora = mixed[:, 3 * RWKV_DIM:]
    lane = lax.broadcasted_iota(jnp.int32, (rows, LANES), 1)
    wl = jnp.where(lane < RWKV_LORA, jnp.tanh(lora), 0.0)
    al = jnp.where(lane >= RWKV_LORA, lora, 0.0)
    w_log = -_softplus(-(w0_ref[...] + _dot_f32(wl, w2_ref[...]))) - 0.5
    a = _sigmoid(a0_ref[...] + _dot_f32(al, a2_ref[...]))
    logw = -jnp.exp(w_log)
    kk_raw = k * kk_ref[...]
    ss = _dot_exact_lhs(kk_raw * kk_raw, hred_ref[...])
    inv_n = 1.0 / jnp.maximum(jnp.sqrt(ss), 1e-12)
    kk = kk_raw * _dot_exact_lhs(inv_n, hexp_ref[...])
    k2 = k * (1.0 + (a - 1.0) * ka_ref[...])
    if live is not None:
        logw = jnp.where(live, logw, 0.0)
        kk = jnp.where(live, kk, 0.0)
        k2 = jnp.where(live, k2, 0.0)
        v = jnp.where(live, v, 0.0)
    return r, k2, v, kk, a, logw


def _rwkv_post(y, r, k2, v, rk_ref, gng_ref, gnb_ref, hred_ref, hexp_ref):
    inv_h = 1.0 / RWKV_HEAD
    mean = _dot_exact_lhs(_dot_exact_lhs(y, hred_ref[...]) * inv_h, hexp_ref[...])
    yc = y - mean
    var = _dot_exact_lhs(yc * yc, hred_ref[...]) * inv_h
    rstd = _dot_exact_lhs(lax.rsqrt(var + RWKV_GN_EPS), hexp_ref[...])
    bonus = _dot_exact_lhs(_dot_exact_lhs(r * k2 * rk_ref[...], hred_ref[...]), hexp_ref[...]) * v
    return yc * rstd * gng_ref[...] + gnb_ref[...] + bonus


def _state_to_blockdiag(stack, m_state):
    return jnp.where(m_state, jnp.concatenate([stack] * RWKV_GROUP_HEADS, axis=1), 0.0)


def _blockdiag_to_state(s_bd):
    stack = s_bd[:, 0:RWKV_HEAD]
    for hh in range(1, RWKV_GROUP_HEADS):
        stack = stack + s_bd[:, hh * RWKV_HEAD:(hh + 1) * RWKV_HEAD]
    return stack


def _rwkv_kernel(rw_ref, shift0_ref, wkv0_ref, mu_ref, w0_ref, w2_ref, a0_ref, a2_ref, kk_ref, ka_ref, rk_ref,
                 gng_ref, gnb_ref, hred_ref, hexp_ref,
                 y_ref, wkv_ref, buf_ref, s_ref, *, cq, t_valid):
    c = pl.program_id(1)
    n_chunks = pl.num_programs(1)
    n_pad = buf_ref.shape[0] - cq
    gl = RWKV_GROUP_LANES
    gh = RWKV_GROUP_HEADS
    n_groups = RWKV_DIM // gl
    nt = ((1,), (1,))
    tn = ((0,), (0,))

    m_state = _head_mask(RWKV_HEAD, RWKV_HEAD, gl, gl)
    m_exp = _head_mask(cq, RWKV_HEAD, gh * cq, gl)
    m_nbd = _head_mask(cq, cq, gh * cq, gh * cq)

    @pl.when(c == 0)
    def _():
        buf_ref[0:n_pad, :] = shift0_ref[0]
        for g in range(n_groups):
            s_ref[g] = _state_to_blockdiag(wkv0_ref[0, g * gl:(g + 1) * gl, :], m_state)

    buf_ref[n_pad:n_pad + cq, :] = rw_ref[...]
    cur = buf_ref[n_pad:n_pad + cq, :]
    prev = buf_ref[n_pad - 1:n_pad - 1 + cq, :]
    buf_ref[0:n_pad, :] = buf_ref[cq:cq + n_pad, :]
    row = lax.broadcasted_iota(jnp.int32, (cq, 1), 0) + c * cq
    r, k2, v, kk, a, logw = _rwkv_prep(cur, prev, row < t_valid, mu_ref, w0_ref, w2_ref, a0_ref, a2_ref, kk_ref,
                                       ka_ref, hred_ref, hexp_ref)

    ti = lax.broadcasted_iota(jnp.int32, (cq, cq), 0)
    si = lax.broadcasted_iota(jnp.int32, (cq, cq), 1)
    tril = (si <= ti).astype(BF16)
    lh, lm, ll = _split3(logw)
    cl = (jnp.dot(tril, lh, preferred_element_type=F32) + jnp.dot(tril, lm, preferred_element_type=F32)
          + jnp.dot(tril, ll, preferred_element_type=F32))
    p_incl = jnp.exp(cl)
    p_inv = jnp.exp(-cl)
    a_t = jnp.exp(cl - logw) * (-kk)
    r_t = p_incl * r
    b_t = p_inv * (kk * a)
    k_t = p_inv * k2
    p_end = p_incl[cq - 1:cq, :]

    tt = lax.broadcasted_iota(jnp.int32, (cq, gh * cq), 0)
    ss_i = lax.broadcasted_iota(jnp.int32, (cq, gh * cq), 1) % cq
    strict = ss_i < tt
    incl = ss_i <= tt

    def expand_rows(x):
        return jnp.where(m_exp, _rep_rows(x, gh), 0.0)

    groups = range(n_groups)
    sls = [slice(g * gl, (g + 1) * gl) for g in groups]
    lhs_ar = [jnp.concatenate([a_t[:, sl], r_t[:, sl]], axis=0) for sl in sls]
    v_exp = [expand_rows(v[:, sl]) for sl in sls]
    s_bd = [s_ref[g] for g in groups]
    ab = [_dot(lhs_ar[g], expand_rows(b_t[:, sls[g]]), nt) for g in groups]
    ak = [_dot(lhs_ar[g], expand_rows(k_t[:, sls[g]]), nt) for g in groups]
    w_s = [_dot(lhs_ar[g], s_bd[g], nt) for g in groups]
    x = [w_s[g][:cq] + _dot(jnp.where(strict, ak[g][:cq], 0.0), v_exp[g]) for g in groups]
    n_cat = [jnp.where(strict, ab[g][:cq], 0.0) for g in groups]
    n_steps = int(math.log2(cq))
    for step in range(n_steps):
        x = [x[g] + _dot(n_cat[g], expand_rows(x[g])) for g in groups]
        if step + 1 < n_steps:
            n_cat = [_dot(n_cat[g], jnp.where(m_nbd, _rep_rows(n_cat[g], gh), 0.0)) for g in groups]
    y_groups = [w_s[g][cq:] + _dot(jnp.where(incl, ab[g][cq:], 0.0), expand_rows(x[g]))
                + _dot(jnp.where(incl, ak[g][cq:], 0.0), v_exp[g]) for g in groups]
    for g in groups:
        uv = jnp.concatenate([x[g], v[:, sls[g]]], axis=0)
        bk = jnp.concatenate([b_t[:, sls[g]], k_t[:, sls[g]]], axis=0)
        inc = _dot(uv, bk, tn)
        s_ref[g] = jnp.where(m_state, (s_bd[g] + inc) * p_end[:, sls[g]], 0.0)

    y = jnp.concatenate(y_groups, axis=1)
    y_ref[...] = _rwkv_post(y, r, k2, v, rk_ref, gng_ref, gnb_ref, hred_ref, hexp_ref)

    @pl.when(c == n_chunks - 1)
    def _():
        for g in range(n_groups):
            wkv_ref[0, g * gl:(g + 1) * gl, :] = _blockdiag_to_state(s_ref[g])


def _rwkv_consts(mu, w0, w2, a0, a2, k_k, k_a, r_k, gn_g, gn_b):
    row2 = lambda a: a.reshape(1, -1).astype(F32)
    zeros_l = jnp.zeros((RWKV_LORA, RWKV_DIM), F32)
    head_of_lane = jnp.arange(RWKV_DIM) // RWKV_HEAD
    hexp = (jnp.arange(LANES)[:, None] == head_of_lane[None, :]).astype(BF16)
    return [row2(mu), row2(w0), jnp.concatenate([w2, zeros_l], axis=0), row2(a0),
            jnp.concatenate([zeros_l, a2], axis=0), row2(k_k), row2(k_a), row2(r_k), row2(gn_g), row2(gn_b),
            hexp.T, hexp]


def _rwkv_step_kernel(rw_ref, prev_ref, wkv0_ref, mu_ref, w0_ref, w2_ref, a0_ref, a2_ref, kk_ref, ka_ref, rk_ref,
                      gng_ref, gnb_ref, hred_ref, hexp_ref, y_ref, wkv_ref):
    bt = rw_ref.shape[0]
    gl = RWKV_GROUP_LANES
    n_groups = RWKV_DIM // gl
    nt = ((1,), (1,))
    tn = ((0,), (0,))
    m_state = _head_mask(RWKV_HEAD, RWKV_HEAD, gl, gl)
    r, k2, v, kk, a, logw = _rwkv_prep(rw_ref[...], prev_ref[...], None, mu_ref, w0_ref, w2_ref, a0_ref, a2_ref,
                                       kk_ref, ka_ref, hred_ref, hexp_ref)
    w = jnp.exp(logw)
    a_hat = -kk
    b = kk * a
    rowid = lax.broadcasted_iota(jnp.int32, (bt, gl), 0)
    groups = range(n_groups)
    sls = [slice(g * gl, (g + 1) * gl) for g in groups]
    y_acc = [jnp.zeros((bt, gl), F32) for _ in groups]
    for bb in range(bt):
        mine = rowid == bb
        hsl = [slice(g * RWKV_GROUP_HEADS, (g + 1) * RWKV_GROUP_HEADS) for g in groups]
        s_bd = [_state_to_blockdiag(wkv0_ref[bb, hs].reshape(gl, RWKV_HEAD), m_state) for hs in hsl]
        u = [_dot(a_hat[:, sl], s_bd[g], nt) for g, sl in enumerate(sls)]
        s_new = []
        for g, sl in enumerate(sls):
            uv = jnp.concatenate([jnp.where(mine, u[g], 0.0), jnp.where(mine, v[:, sl], 0.0)], axis=0)
            bk = jnp.concatenate([b[:, sl], k2[:, sl]], axis=0)
            uh, um, ul = _split3(uv)
            bh, bmid, bl = _split3(bk)
            inc = _dot(jnp.concatenate([uh, uh, um, uh, ul, um], axis=0),
                       jnp.concatenate([bh, bmid, bh, bl, bh, bmid], axis=0), tn)
            s_new.append(jnp.where(m_state, s_bd[g] * w[bb:bb + 1, sl] + inc, 0.0))
        for g, sl in enumerate(sls):
            y_acc[g] = jnp.where(mine, _dot(r[:, sl], s_new[g], nt), y_acc[g])
            wkv_ref[bb, hsl[g]] = _blockdiag_to_state(s_new[g]).reshape(RWKV_GROUP_HEADS, RWKV_HEAD, RWKV_HEAD)
    y = jnp.concatenate(y_acc, axis=1)
    y_ref[...] = _rwkv_post(y, r, k2, v, rk_ref, gng_ref, gnb_ref, hred_ref, hexp_ref)


def rwkv_step(rw, row0, prev, wkv0, *params, bt, name):
    nb = prev.shape[0]
    consts = _rwkv_consts(*params)
    full = lambda a: pl.BlockSpec(a.shape, lambda i: (0,) * a.ndim)
    state_spec = pl.BlockSpec((bt,) + wkv0.shape[1:], lambda i: (i, 0, 0, 0))
    return pl.pallas_call(
        _rwkv_step_kernel,
        out_shape=(jax.ShapeDtypeStruct((nb, RWKV_DIM), F32), jax.ShapeDtypeStruct(wkv0.shape, F32)),
        grid=(nb // bt,),
        in_specs=[pl.BlockSpec((bt, RWKV_SHIFT_DIM), lambda i: (row0 // bt + i, 0)),
                  pl.BlockSpec((bt, RWKV_SHIFT_DIM), lambda i: (i, 0)),
                  state_spec] + [full(a) for a in consts],
        out_specs=(pl.BlockSpec((bt, RWKV_DIM), lambda i: (i, 0)), state_spec),
        compiler_params=_cparams("parallel"),
        name=name,
    )(rw, prev, wkv0, *consts)


def rwkv_branch(rw, shift0, wkv0, *params, t, cq, t_valid, name):
    nb = wkv0.shape[0]
    cps = t // cq
    n_pad = shift0.shape[1]
    consts = _rwkv_consts(*params)
    full = lambda a: pl.BlockSpec(a.shape, lambda b, c: (0,) * a.ndim)
    wkv_rows = wkv0.reshape(nb, RWKV_DIM, RWKV_HEAD)
    kern = functools.partial(_rwkv_kernel, cq=cq, t_valid=t_valid)
    y, wkv = pl.pallas_call(
        kern,
        out_shape=(jax.ShapeDtypeStruct((nb * t, RWKV_DIM), F32),
                   jax.ShapeDtypeStruct(wkv_rows.shape, F32)),
        grid=(nb, cps),
        in_specs=[pl.BlockSpec((cq, RWKV_SHIFT_DIM), lambda b, c: (b * cps + c, 0)),
                  pl.BlockSpec((1, n_pad, RWKV_SHIFT_DIM), lambda b, c: (b, 0, 0)),
                  pl.BlockSpec((1, RWKV_DIM, RWKV_HEAD), lambda b, c: (b, 0, 0))] + [full(a) for a in consts],
        out_specs=(pl.BlockSpec((cq, RWKV_DIM), lambda b, c: (b * cps + c, 0)),
                   pl.BlockSpec((1, RWKV_DIM, RWKV_HEAD), lambda b, c: (b, 0, 0))),
        scratch_shapes=[pltpu.VMEM((cq + n_pad, RWKV_SHIFT_DIM), F32),
                        pltpu.VMEM((RWKV_DIM // RWKV_GROUP_LANES, RWKV_GROUP_LANES, RWKV_GROUP_LANES), F32)],
        compiler_params=_cparams("parallel", "arbitrary"),
        name=name,
    )(rw, shift0, wkv_rows, *consts)
    return y, wkv.reshape(wkv0.shape)


NEG_BIG = -1e30


def _mix_kernel(g_ref, yap_ref, ybp_ref, yas_ref, ybs_ref, h_ref, wout_ref, ln2_ref, rw_ref, rb_ref,
                h1_ref, xn_ref, idx_ref, gate_ref):
    i = pl.program_id(0)
    last = i == pl.num_programs(0) - 1
    ya = jnp.where(last, yas_ref[...], yap_ref[...])
    yb = jnp.where(last, ybs_ref[...], ybp_ref[...])
    g = g_ref[...]
    merged = _sigmoid(g[:, :D_MODEL]) * ya + _sigmoid(g[:, D_MODEL:]) * yb
    h1 = h_ref[...] + jnp.dot(merged.astype(BF16), wout_ref[...], preferred_element_type=F32)
    h1_ref[...] = h1
    ms = jnp.mean(h1 * h1, axis=-1, keepdims=True)
    xn = h1 * lax.rsqrt(ms + RMS_EPS) * ln2_ref[...]
    xn_ref[...] = xn
    logits = _dot_f32(xn, rw_ref[...]) + rb_ref[...]
    lane = lax.broadcasted_iota(jnp.int32, logits.shape, 1).astype(F32)
    idx_out = jnp.zeros(logits.shape, F32)
    val_out = jnp.zeros(logits.shape, F32)
    top0 = None
    for kth in range(TOP_K):
        m = jnp.max(logits, axis=-1, keepdims=True)
        sel = jnp.min(jnp.where(logits == m, lane, float(LANES)), axis=-1, keepdims=True)
        if kth == 0:
            top0 = m
        idx_out = jnp.where(lane == kth, sel, idx_out)
        val_out = jnp.where(lane == kth, jnp.exp(m - top0), val_out)
        logits = jnp.where(lane == sel, -jnp.inf, logits)
    idx_ref[...] = idx_out.astype(jnp.int32)
    gate_ref[...] = val_out / jnp.sum(val_out, axis=-1, keepdims=True)


def mix_and_route(gates, ya_p, yb_p, ya_s, yb_s, h_rows, w_out_bf, ln2_g, router_w, router_b, tm):
    rows = h_rows.shape[0]
    n_tiles = rows // tm
    last_p = ya_p.shape[0] // tm - 1
    rw_pad = jnp.pad(router_w, ((0, 0), (0, LANES - N_EXPERTS)))
    rb_pad = jnp.pad(router_b, (0, LANES - N_EXPERTS), constant_values=NEG_BIG).reshape(1, LANES)
    row_spec = lambda n: pl.BlockSpec((tm, n), lambda i: (i, 0))
    prompt_spec = pl.BlockSpec((tm, D_MODEL), lambda i: (jnp.minimum(i, last_p), 0))
    fixed = lambda a: pl.BlockSpec(a.shape, lambda i: (0,) * a.ndim)
    return pl.pallas_call(
        _mix_kernel,
        out_shape=(jax.ShapeDtypeStruct((rows, D_MODEL), F32), jax.ShapeDtypeStruct((rows, D_MODEL), F32),
                   jax.ShapeDtypeStruct((rows, LANES), jnp.int32), jax.ShapeDtypeStruct((rows, LANES), F32)),
        grid=(n_tiles,),
        in_specs=[row_spec(2 * D_MODEL), prompt_spec, prompt_spec, fixed(ya_s), fixed(yb_s), row_spec(D_MODEL),
                  fixed(w_out_bf), pl.BlockSpec((1, D_MODEL), lambda i: (0, 0)), fixed(rw_pad), fixed(rb_pad)],
        out_specs=(row_spec(D_MODEL), row_spec(D_MODEL), row_spec(LANES), row_spec(LANES)),
        compiler_params=_cparams("parallel"),
        name="mix_and_route",
    )(gates, ya_p, yb_p, ya_s, yb_s, h_rows, w_out_bf, ln2_g.reshape(1, D_MODEL), rw_pad, rb_pad)


def _gather_rows_kernel(nb_ref, tok_ref, tok_next_ref, x_hbm, o_ref, rows_ref, sem):
    i = pl.program_id(0)
    n_used = nb_ref[0]
    slot = lax.rem(i, 2)

    def copies(idx_ref, buf):
        return [pltpu.make_async_copy(x_hbm.at[pl.ds(idx_ref[0, 0, r], 1), :],
                                      rows_ref.at[buf, pl.ds(r, 1), :], sem.at[buf]) for r in range(MOE_BLOCK)]

    @pl.when(jnp.logical_and(i == 0, n_used > 0))
    def _():
        for cp in copies(tok_ref, 0):
            cp.start()

    @pl.when(i + 1 < n_used)
    def _():
        for cp in copies(tok_next_ref, 1 - slot):
            cp.start()

    @pl.when(i < n_used)
    def _():
        for cp in copies(tok_ref, slot):
            cp.wait()
        o_ref[...] = rows_ref[slot].astype(o_ref.dtype)

    @pl.when(i >= n_used)
    def _():
        o_ref[...] = jnp.zeros_like(o_ref)


def gather_rows(x, slot_tok, n_used, out_dtype):
    n_blocks = slot_tok.shape[0] // MOE_BLOCK
    d = x.shape[1]
    tok3 = slot_tok.reshape(n_blocks, 1, MOE_BLOCK)
    grid_spec = pltpu.PrefetchScalarGridSpec(
        num_scalar_prefetch=1,
        grid=(n_blocks,),
        in_specs=[pl.BlockSpec((1, 1, MOE_BLOCK), lambda i, nb: (i, 0, 0), memory_space=pltpu.SMEM),
                  pl.BlockSpec((1, 1, MOE_BLOCK), lambda i, nb: (jnp.minimum(i + 1, n_blocks - 1), 0, 0),
                               memory_space=pltpu.SMEM),
                  pl.BlockSpec(memory_space=pl.ANY)],
        out_specs=pl.BlockSpec((MOE_BLOCK, d), lambda i, nb: (i, 0)),
        scratch_shapes=[pltpu.VMEM((2, MOE_BLOCK, d), x.dtype), pltpu.SemaphoreType.DMA((2,))],
    )
    return pl.pallas_call(
        _gather_rows_kernel,
        out_shape=jax.ShapeDtypeStruct((n_blocks * MOE_BLOCK, d), out_dtype),
        grid_spec=grid_spec,
        compiler_params=_cparams("arbitrary"),
        name="moe_gather",
    )(n_used, tok3, tok3, x)


def next_run_expert(block_e):
    n = block_e.shape[0]
    idx = jnp.arange(n, dtype=jnp.int32)
    change = jnp.concatenate([jnp.ones((1,), bool), block_e[1:] != block_e[:-1]])
    later_change = jnp.concatenate([jnp.where(change, idx, n)[1:], jnp.full((1,), n, jnp.int32)])
    next_pos = lax.cummin(later_change, axis=0, reverse=True)
    onehot = next_pos[:, None] == idx[None, :]
    return jnp.where(next_pos < n, jnp.sum(jnp.where(onehot, block_e[None, :], 0), axis=1), -1).astype(jnp.int32)


def _stream_expert_weights(be_ref, nxt_ref, copies, cast):
    i = pl.program_id(0)
    first = jnp.logical_or(i == 0, be_ref[i] != be_ref[jnp.maximum(i, 1) - 1])

    @pl.when(i == 0)
    def _():
        for cp in copies(be_ref[0]):
            cp.start()

    @pl.when(first)
    def _():
        for cp in copies(be_ref[i]):
            cp.wait()
        cast()

    @pl.when(jnp.logical_and(first, nxt_ref[i] >= 0))
    def _():
        for cp in copies(nxt_ref[i]):
            cp.start()


def _moe_up_kernel(be_ref, nxt_ref, nb_ref, x_ref, w_hbm, b_ref, o_ref, w_f32, w_bf, sem):
    i = pl.program_id(0)

    def copies(e):
        return [pltpu.make_async_copy(w_hbm.at[e, :, pl.ds(half * D_FF, D_FF)], w_f32.at[half], sem.at[half])
                for half in range(2)]

    def cast():
        w_bf[...] = w_f32[...].astype(BF16)

    _stream_expert_weights(be_ref, nxt_ref, copies, cast)

    @pl.when(i < nb_ref[0])
    def _():
        x = x_ref[...]
        g = jnp.dot(x, w_bf[0], preferred_element_type=F32) + b_ref[0, :, :D_FF]
        u = jnp.dot(x, w_bf[1], preferred_element_type=F32) + b_ref[0, :, D_FF:]
        g = jnp.minimum(g, SWIGLU_LIMIT)
        u = jnp.clip(u, -SWIGLU_LIMIT, SWIGLU_LIMIT)
        o_ref[...] = ((u + 1.0) * (g * _sigmoid(g * SWIGLU_ALPHA))).astype(o_ref.dtype)

    @pl.when(i >= nb_ref[0])
    def _():
        o_ref[...] = jnp.zeros_like(o_ref)


def moe_up(xb, block_e, next_e, n_used, w_gate_up, b_gate_up):
    n_blocks = xb.shape[0] // MOE_BLOCK
    grid_spec = pltpu.PrefetchScalarGridSpec(
        num_scalar_prefetch=3,
        grid=(n_blocks,),
        in_specs=[pl.BlockSpec((MOE_BLOCK, D_MODEL), lambda i, be, nx, nb: (i, 0)),
                  pl.BlockSpec(memory_space=pl.ANY),
                  pl.BlockSpec((1, 1, 2 * D_FF), lambda i, be, nx, nb: (be[i], 0, 0))],
        out_specs=pl.BlockSpec((MOE_BLOCK, D_FF), lambda i, be, nx, nb: (i, 0)),
        scratch_shapes=[pltpu.VMEM((2, D_MODEL, D_FF), F32), pltpu.VMEM((2, D_MODEL, D_FF), BF16),
                        pltpu.SemaphoreType.DMA((2,))],
    )
    return pl.pallas_call(
        _moe_up_kernel,
        out_shape=jax.ShapeDtypeStruct((n_blocks * MOE_BLOCK, D_FF), BF16),
        grid_spec=grid_spec,
        compiler_params=_cparams("arbitrary"),
        name="moe_up",
    )(block_e, next_e, n_used, xb, w_gate_up, b_gate_up.reshape(N_EXPERTS, 1, 2 * D_FF))


def _moe_down_kernel(be_ref, nxt_ref, nb_ref, h_ref, w_hbm, bd_ref, o_ref, w_f32, w_bf, sem):
    i = pl.program_id(0)

    def copies(e):
        return [pltpu.make_async_copy(w_hbm.at[e], w_f32, sem)]

    def cast():
        w_bf[...] = w_f32[...].astype(BF16)

    _stream_expert_weights(be_ref, nxt_ref, copies, cast)

    @pl.when(i < nb_ref[0])
    def _():
        o_ref[...] = jnp.dot(h_ref[...], w_bf[...], preferred_element_type=F32) + bd_ref[0]

    @pl.when(i >= nb_ref[0])
    def _():
        o_ref[...] = jnp.zeros_like(o_ref)


def moe_down(hb, block_e, next_e, n_used, w_down, b_down):
    n_blocks = hb.shape[0] // MOE_BLOCK
    grid_spec = pltpu.PrefetchScalarGridSpec(
        num_scalar_prefetch=3,
        grid=(n_blocks,),
        in_specs=[pl.BlockSpec((MOE_BLOCK, D_FF), lambda i, be, nx, nb: (i, 0)),
                  pl.BlockSpec(memory_space=pl.ANY),
                  pl.BlockSpec((1, 1, D_MODEL), lambda i, be, nx, nb: (be[i], 0, 0))],
        out_specs=pl.BlockSpec((MOE_BLOCK, D_MODEL), lambda i, be, nx, nb: (i, 0)),
        scratch_shapes=[pltpu.VMEM((D_FF, D_MODEL), F32), pltpu.VMEM((D_FF, D_MODEL), BF16),
                        pltpu.SemaphoreType.DMA(())],
    )
    return pl.pallas_call(
        _moe_down_kernel,
        out_shape=jax.ShapeDtypeStruct((n_blocks * MOE_BLOCK, D_MODEL), F32),
        grid_spec=grid_spec,
        compiler_params=_cparams("arbitrary"),
        name="moe_down",
    )(block_e, next_e, n_used, hb, w_down, b_down.reshape(N_EXPERTS, 1, D_MODEL))


def _combine_kernel(slot_ref, slot_next_ref, row0_ref, gate_ref, lnf_ref, yb_hbm, h_hbm, o_ref, rows_ref, h_vmem,
                    sem, hsem):
    i = pl.program_id(0)
    n_tiles = pl.num_programs(0)
    tm = o_ref.shape[-2]
    slot = lax.rem(i, 2)

    def copies(idx_ref, tile, buf):
        row0 = pl.multiple_of(row0_ref[tile], SUBLANES)
        cps = [pltpu.make_async_copy(h_hbm.at[pl.ds(row0, tm), :], h_vmem.at[buf], hsem.at[buf])]
        for n in range(TOP_K * tm):
            cps.append(pltpu.make_async_copy(yb_hbm.at[pl.ds(idx_ref[0, 0, n], 1), :],
                                             rows_ref.at[buf, n // tm, pl.ds(n % tm, 1), :], sem.at[buf]))
        return cps

    @pl.when(i == 0)
    def _():
        for cp in copies(slot_ref, 0, 0):
            cp.start()

    @pl.when(i + 1 < n_tiles)
    def _():
        for cp in copies(slot_next_ref, jnp.minimum(i + 1, n_tiles - 1), 1 - slot):
            cp.start()

    for cp in copies(slot_ref, i, slot):
        cp.wait()
    gate = gate_ref[...]
    acc = h_vmem[slot]
    for k in range(TOP_K):
        acc = acc + rows_ref[slot, k] * gate[:, k:k + 1]
    ms = jnp.mean(acc * acc, axis=-1, keepdims=True)
    o_ref[...] = (acc * lax.rsqrt(ms + RMS_EPS) * lnf_ref[...]).reshape(o_ref.shape)


def moe_combine(yb, h1, slot_of_pair, gates, row0, lnf_g, out_shape, out_index_map, tm):
    n_tiles = row0.shape[0]
    grid_spec = pltpu.PrefetchScalarGridSpec(
        num_scalar_prefetch=0,
        grid=(n_tiles,),
        in_specs=[pl.BlockSpec((1, 1, TOP_K * tm), lambda i: (i, 0, 0), memory_space=pltpu.SMEM),
                  pl.BlockSpec((1, 1, TOP_K * tm), lambda i: (jnp.minimum(i + 1, n_tiles - 1), 0, 0),
                               memory_space=pltpu.SMEM),
                  pl.BlockSpec(memory_space=pltpu.SMEM),
                  pl.BlockSpec((tm, LANES), lambda i: (i, 0)),
                  pl.BlockSpec((1, D_MODEL), lambda i: (0, 0)),
                  pl.BlockSpec(memory_space=pl.ANY),
                  pl.BlockSpec(memory_space=pl.ANY)],
        out_specs=pl.BlockSpec(out_shape[0], out_index_map),
        scratch_shapes=[pltpu.VMEM((2, TOP_K, tm, D_MODEL), F32), pltpu.VMEM((2, tm, D_MODEL), F32),
                        pltpu.SemaphoreType.DMA((2,)), pltpu.SemaphoreType.DMA((2,))],
    )
    return pl.pallas_call(
        _combine_kernel,
        out_shape=jax.ShapeDtypeStruct(out_shape[1], F32),
        grid_spec=grid_spec,
        compiler_params=_cparams("arbitrary"),
        name="moe_combine",
    )(slot_of_pair, slot_of_pair, row0, gates, lnf_g.reshape(1, D_MODEL), yb, h1)


PROMPT_PAD_T = 2176
SAMPLE_PAD_T = SUBLANES
ROW_TILE = 1104


def kernel(x_prompt, x_sample, state_ssm_conv, state_ssm, state_rwkv_shift, state_rwkv_wkv, meta_tokens, ln1_g, w_in, ssm_conv_w, ssm_conv_b, ssm_dt_bias, ssm_A_log, ssm_D, ssm_norm_g, rwkv_mu, rwkv_w0, rwkv_w2, rwkv_a0, rwkv_a2, rwkv_k_k, rwkv_k_a, rwkv_r_k, rwkv_gn_g, rwkv_gn_b, w_out, ln2_g, router_w, router_b, w_gate_up, b_gate_up, w_down, b_down, lnf_g):
    bp, seq, d = x_prompt.shape
    bs = x_sample.shape[0]
    t_prompt = N_META + seq
    tail = jnp.zeros((PROMPT_PAD_T - t_prompt, d), F32)
    pieces = []
    for b in range(bp):
        pieces += [meta_tokens, x_prompt[b], tail]
    h_rows = jnp.concatenate(pieces + [x_sample.reshape(bs, d)], axis=0)
    n_prow = bp * PROMPT_PAD_T

    l = 0
    xn = rmsnorm_rows(h_rows, ln1_g[l], BF16, ROW_TILE)
    w = w_in[l]
    o_xbc = SSM_D_INNER
    o_dt = o_xbc + SSM_CONV_DIM
    o_rw = o_dt + SSM_N_HEADS
    o_g = o_rw + RWKV_SHIFT_DIM
    w_z = w[:, :o_xbc].astype(BF16)
    w_xbc = w[:, o_xbc:o_dt].astype(BF16)
    w_dt = jnp.pad(w[:, o_dt:o_rw], ((0, 0), (0, LANES - SSM_N_HEADS))).astype(BF16)
    w_rw = w[:, o_rw:o_g].astype(BF16)
    w_g = w[:, o_g:].astype(BF16)
    z = matmul_bf16(xn, w_z, ROW_TILE, 1024, "proj_z")
    xbc = matmul_bf16(xn, w_xbc, ROW_TILE, 1024, "proj_xbc")
    dtr = matmul_bf16(xn, w_dt, ROW_TILE, LANES, "proj_dt")

    def sample_rows(a):
        n = a.shape[1]
        return jnp.pad(a[n_prow:].reshape(bs, 1, n), ((0, 0), (0, SAMPLE_PAD_T - 1), (0, 0))).reshape(-1, n)

    ssm_args = (ssm_conv_w[l], ssm_conv_b[l], ssm_dt_bias[l], ssm_A_log[l], ssm_D[l], ssm_norm_g[l])
    conv0_p = jnp.zeros((bp, SUBLANES, SSM_CONV_DIM), F32)
    s0_p = jnp.zeros((bp, SSM_N_HEADS, SSM_HEAD_DIM, SSM_D_STATE), F32)
    ya_p, ssm_p = ssd_branch(xbc, z, dtr, conv0_p, s0_p, *ssm_args, t=PROMPT_PAD_T, q=SSM_CHUNK,
                             t_valid=t_prompt, exact_state=False, name="ssd_prompt")
    conv0_s = jnp.pad(state_ssm_conv[l], ((0, 0), (SUBLANES - (SSM_CONV - 1), 0), (0, 0)))
    ya_s, ssm_s = ssd_branch(sample_rows(xbc), sample_rows(z), sample_rows(dtr), conv0_s, state_ssm[l], *ssm_args,
                             t=SAMPLE_PAD_T, q=SAMPLE_PAD_T, t_valid=1, exact_state=True, name="ssd_sample")
    ya_s = ya_s.reshape(bs, SAMPLE_PAD_T, d)[:, 0]
    rw = matmul_bf16(xn, w_rw, ROW_TILE, 896, "proj_rw")
    rwkv_args = (rwkv_mu[l], rwkv_w0[l], rwkv_w2[l], rwkv_a0[l], rwkv_a2[l], rwkv_k_k[l], rwkv_k_a[l],
                 rwkv_r_k[l], rwkv_gn_g[l], rwkv_gn_b[l])
    shift0_p = jnp.zeros((bp, SUBLANES, RWKV_SHIFT_DIM), F32)
    wkv0_p = jnp.zeros((bp, RWKV_N_HEADS, RWKV_HEAD, RWKV_HEAD), F32)
    yb_p, wkv_p = rwkv_branch(rw, shift0_p, wkv0_p, *rwkv_args, t=PROMPT_PAD_T, cq=RWKV_CHUNK, t_valid=t_prompt,
                              name="rwkv_prompt")
    yb_s, wkv_s = rwkv_step(rw, n_prow, state_rwkv_shift[l], state_rwkv_wkv[l], *rwkv_args, bt=SUBLANES,
                            name="rwkv_sample")
    gates = matmul_bf16(xn, w_g, ROW_TILE, 1024, "proj_gates")
    h1, xn2, top_idx, top_gate = mix_and_route(
        gates, ya_p, yb_p, ya_s, yb_s, h_rows,
        w_out[l].astype(BF16), ln2_g[l], router_w[l], router_b[l], MOE_BLOCK)

    n_rows = h_rows.shape[0]
    row_id = jnp.arange(n_rows, dtype=jnp.int32)
    valid = jnp.logical_or(row_id >= n_prow, row_id % PROMPT_PAD_T < t_prompt)
    yb, slot_of_pair = moe_expert_rows(xn2, top_idx, valid, (bp * t_prompt + bs) * TOP_K, w_gate_up[l],
                                       b_gate_up[l], w_down[l], b_down[l])

    def combine(row0, out_block, out_full, out_map):
        rows = (row0[:, None] + jnp.arange(MOE_BLOCK, dtype=jnp.int32)[None, :])
        slots = jnp.transpose(slot_of_pair[rows], (0, 2, 1)).reshape(row0.shape[0], 1, TOP_K * MOE_BLOCK)
        return moe_combine(yb, h1, slots, top_gate[rows.reshape(-1)], row0, lnf_g, (out_block, out_full),
                           out_map, MOE_BLOCK)

    tiles_per_seq = seq // MOE_BLOCK
    tile_id = jnp.arange(bp * tiles_per_seq, dtype=jnp.int32)
    row0_p = (tile_id // tiles_per_seq) * PROMPT_PAD_T + N_META + (tile_id % tiles_per_seq) * MOE_BLOCK
    y_prompt = combine(row0_p, (1, MOE_BLOCK, d), (bp, seq, d),
                       lambda i: (i // tiles_per_seq, i % tiles_per_seq, 0))
    y_sample = combine(jnp.full((1,), n_prow, jnp.int32), (MOE_BLOCK, d), (bs, d), lambda i: (i, 0))

    last = [b * PROMPT_PAD_T + t_prompt - 1 for b in range(bp)]
    prompt_conv = jnp.stack([xbc[r - (SSM_CONV - 2):r + 1] for r in last])
    prompt_shift = jnp.stack([rw[r] for r in last])
    sample_conv = jnp.concatenate([state_ssm_conv[l][:, 1:], xbc[n_prow:, None]], axis=1)
    return (y_prompt, y_sample.reshape(bs, 1, d), prompt_conv[None], ssm_p[None], prompt_shift[None], wkv_p[None],
            sample_conv[None], ssm_s[None], rw[n_prow:][None], wkv_s[None])


def moe_expert_rows(xn2, top_idx, valid, n_valid_pairs, w_gate_up, b_gate_up, w_down, b_down):
    n_rows = xn2.shape[0]
    experts = jnp.arange(N_EXPERTS, dtype=jnp.int32)
    key = jnp.where(valid[:, None], top_idx[:, :TOP_K], N_EXPERTS).reshape(-1)
    n_pairs = key.shape[0]
    n_blocks = -(-n_valid_pairs // MOE_BLOCK) + N_EXPERTS
    n_slots = n_blocks * MOE_BLOCK
    counts = jnp.sum((key[:, None] == experts[None, :]).astype(jnp.int32), axis=0)
    padded = (counts + MOE_BLOCK - 1) // MOE_BLOCK * MOE_BLOCK
    ends = jnp.cumsum(padded)
    fill_e = jnp.repeat(experts, MOE_BLOCK)
    fill_r = jnp.tile(jnp.arange(MOE_BLOCK, dtype=jnp.int32), N_EXPERTS)
    fill_key = jnp.where(fill_r < jnp.repeat(padded - counts, MOE_BLOCK), fill_e, N_EXPERTS + 1)
    keys = jnp.concatenate([key, fill_key])
    iota = jnp.arange(keys.shape[0], dtype=jnp.int32)
    _, order = lax.sort((keys, iota), num_keys=1, is_stable=True)
    _, slot_of = lax.sort((order, iota), num_keys=1, is_stable=True)
    slot_tok = jnp.where(order[:n_slots] < n_pairs, order[:n_slots] // TOP_K, 0).astype(jnp.int32)
    slot_of_pair = jnp.minimum(slot_of[:n_pairs], n_slots - 1).reshape(n_rows, TOP_K)
    block_start = jnp.arange(n_blocks, dtype=jnp.int32) * MOE_BLOCK
    block_e = jnp.minimum(jnp.sum((ends[None, :] <= block_start[:, None]).astype(jnp.int32), axis=1),
                          N_EXPERTS - 1).astype(jnp.int32)
    n_used = (ends[-1:] // MOE_BLOCK).astype(jnp.int32)
    next_e = next_run_expert(block_e)

    xb = gather_rows(xn2, slot_tok, n_used, BF16)
    hb = moe_up(xb, block_e, next_e, n_used, w_gate_up, b_gate_up)
    yb = moe_down(hb, block_e, next_e, n_used, w_down, b_down)
    return yb, slot_of_pair
```

```python
import functools
import math

import jax
import jax.numpy as jnp
from jax import lax
from jax.experimental import pallas as pl
from jax.experimental.pallas import tpu as pltpu

F32 = jnp.float32
BF16 = jnp.bfloat16

V7X_VMEM_LIMIT_BYTES = 60000 * 1024
LANES = 128
SUBLANES = 8

D_MODEL = 2048
N_META = 16
RMS_EPS = 1e-6
SSM_D_INNER = 2048
SSM_HEAD_DIM = 64
SSM_N_HEADS = 32
SSM_N_GROUPS = 4
SSM_HPG = 8
SSM_D_STATE = 128
SSM_CONV = 4
SSM_CHUNK = 128
SSM_CONV_DIM = SSM_D_INNER + 2 * SSM_N_GROUPS * SSM_D_STATE
SSM_NORM_EPS = 1e-5
RWKV_DIM = 2048
RWKV_HEAD = 64
RWKV_N_HEADS = 32
RWKV_LORA = 64
RWKV_GN_EPS = 64e-5
RWKV_SHIFT_DIM = 3 * RWKV_DIM + 2 * RWKV_LORA
RWKV_CHUNK = 64
N_EXPERTS = 32
TOP_K = 4
D_FF = 2048
SWIGLU_LIMIT = 7.0
SWIGLU_ALPHA = 1.702
MOE_BLOCK = 128


def _cparams(*sem):
    return pltpu.CompilerParams(dimension_semantics=sem, vmem_limit_bytes=V7X_VMEM_LIMIT_BYTES)


def _split3(x):
    hi = x.astype(BF16)
    r1 = x - hi.astype(F32)
    mid = r1.astype(BF16)
    lo = (r1 - mid.astype(F32)).astype(BF16)
    return hi, mid, lo


def _dot(a, b, dims=None):
    a = a.astype(BF16)
    b = b.astype(BF16)
    if dims is None:
        return jnp.dot(a, b, preferred_element_type=F32)
    return lax.dot_general(a, b, (dims, ((), ())), preferred_element_type=F32)


def _dot_exact_lhs(x, m01):
    hi, mid, lo = _split3(x)
    m = m01.astype(BF16)
    return (jnp.dot(hi, m, preferred_element_type=F32) + jnp.dot(mid, m, preferred_element_type=F32)
            + jnp.dot(lo, m, preferred_element_type=F32))


def _dot_split_lhs(x, m01, terms):
    m = m01.astype(BF16)
    out = None
    for part in _split3(x)[:terms]:
        d = jnp.dot(part, m, preferred_element_type=F32)
        out = d if out is None else out + d
    return out


def _dot_f32(a, b, terms=6):
    ah, am, al = _split3(a)
    bh, bm, bl = _split3(b)
    lhs = jnp.concatenate([ah, ah, am, ah, al, am][:terms], axis=1)
    rhs = jnp.concatenate([bh, bm, bh, bl, bh, bm][:terms], axis=0)
    return jnp.dot(lhs, rhs, preferred_element_type=F32)


def _softplus(x):
    return jnp.maximum(x, 0.0) + jnp.log(1.0 + jnp.exp(-jnp.abs(x)))


def _sigmoid(x):
    return 1.0 / (1.0 + jnp.exp(-x))


def _silu(x):
    return x * _sigmoid(x)


def _rmsnorm_kernel(x_ref, g_ref, o_ref, *, eps):
    x = x_ref[...]
    ms = jnp.mean(x * x, axis=-1, keepdims=True)
    o_ref[...] = (x * lax.rsqrt(ms + eps) * g_ref[...]).astype(o_ref.dtype)


def rmsnorm_rows(x, g, out_dtype, tm):
    rows, d = x.shape
    return pl.pallas_call(
        functools.partial(_rmsnorm_kernel, eps=RMS_EPS),
        out_shape=jax.ShapeDtypeStruct((rows, d), out_dtype),
        grid=(rows // tm,),
        in_specs=[pl.BlockSpec((tm, d), lambda i: (i, 0)), pl.BlockSpec((1, d), lambda i: (0, 0))],
        out_specs=pl.BlockSpec((tm, d), lambda i: (i, 0)),
        compiler_params=_cparams("parallel"),
        name="rmsnorm_rows",
    )(x, g.reshape(1, d))


def _matmul_kernel(x_ref, w_ref, o_ref):
    o_ref[...] = jnp.dot(x_ref[...], w_ref[...], preferred_element_type=F32)


def matmul_bf16(x, w, tm, tn, name):
    rows, k = x.shape
    n = w.shape[1]
    return pl.pallas_call(
        _matmul_kernel,
        out_shape=jax.ShapeDtypeStruct((rows, n), F32),
        grid=(n // tn, rows // tm),
        in_specs=[pl.BlockSpec((tm, k), lambda j, i: (i, 0)), pl.BlockSpec((k, tn), lambda j, i: (0, j))],
        out_specs=pl.BlockSpec((tm, tn), lambda j, i: (i, j)),
        compiler_params=_cparams("parallel", "parallel"),
        name=name,
    )(x, w)


def _ssd_kernel(xbc_ref, z_ref, dtr_ref, conv0_ref, s0_ref, convw_ref, convb_ref, dtb_ref, alog_ref,
                dskip_ref, ng_ref, expand_ref, y_ref, s_ref, buf_ref, *, q, t_valid, exact_state):
    c = pl.program_id(1)
    n_pad = buf_ref.shape[0] - q

    @pl.when(c == 0)
    def _():
        buf_ref[0:n_pad, :] = conv0_ref[0]
        s_ref[0] = s0_ref[0]

    buf_ref[n_pad:n_pad + q, :] = xbc_ref[...]
    acc = convb_ref[...] + convw_ref[SSM_CONV - 1:SSM_CONV, :] * buf_ref[n_pad:n_pad + q, :]
    for k in range(SSM_CONV - 1):
        off = n_pad - (SSM_CONV - 1) + k
        acc = acc + convw_ref[k:k + 1, :] * buf_ref[off:off + q, :]
    buf_ref[0:n_pad, :] = buf_ref[q:q + n_pad, :]
    xbc = _silu(acc)
    gn = SSM_N_GROUPS * SSM_D_STATE
    xs = xbc[:, :SSM_D_INNER]
    bm = xbc[:, SSM_D_INNER:SSM_D_INNER + gn]
    cm = xbc[:, SSM_D_INNER + gn:]

    row = lax.broadcasted_iota(jnp.int32, (q, LANES), 0) + c * q
    lane = lax.broadcasted_iota(jnp.int32, (q, LANES), 1)
    live = jnp.logical_and(row < t_valid, lane < SSM_N_HEADS)
    dt = jnp.where(live, _softplus(dtr_ref[...] + dtb_ref[...]), 0.0)
    da = dt * (-jnp.exp(alog_ref[...]))

    ti = lax.broadcasted_iota(jnp.int32, (q, q), 0)
    si = lax.broadcasted_iota(jnp.int32, (q, q), 1)
    causal = si <= ti
    tril = causal.astype(BF16)
    dh, dm, dl = _split3(da)
    cs = (jnp.dot(tril, dh, preferred_element_type=F32) + jnp.dot(tril, dm, preferred_element_type=F32)
          + jnp.dot(tril, dl, preferred_element_type=F32))
    eye = (lax.broadcasted_iota(jnp.int32, (LANES, LANES), 0)
           == lax.broadcasted_iota(jnp.int32, (LANES, LANES), 1)).astype(BF16)
    ch, cmid, cl = _split3(cs)
    nt = (((1,), (1,)), ((), ()))
    cs_t = (lax.dot_general(eye, ch, nt, preferred_element_type=F32)
            + lax.dot_general(eye, cmid, nt, preferred_element_type=F32)
            + lax.dot_general(eye, cl, nt, preferred_element_type=F32))

    expand = expand_ref[...]
    dt_x = _dot_exact_lhs(dt, expand)
    cs_x = _dot_exact_lhs(cs, expand)
    xdt = xs * dt_x
    ecs = jnp.exp(cs_x)
    dte = jnp.exp(cs_x[q - 1:q, :] - cs_x)

    lane_in_pair = lax.broadcasted_iota(jnp.int32, (q, LANES), 1)
    lo_half = lane_in_pair < SSM_HEAD_DIM

    y_groups = []
    for g in range(SSM_N_GROUPS):
        b_g = bm[:, g * SSM_D_STATE:(g + 1) * SSM_D_STATE]
        c_g = cm[:, g * SSM_D_STATE:(g + 1) * SSM_D_STATE]
        cb = _dot(c_g, b_g, ((1,), (1,)))
        gsl = slice(g * SSM_HPG * SSM_HEAD_DIM, (g + 1) * SSM_HPG * SSM_HEAD_DIM)
        s_g = s_ref[0, g * SSM_HPG:(g + 1) * SSM_HPG].reshape(SSM_HPG * SSM_HEAD_DIM, SSM_D_STATE)
        y_off = _dot(c_g, s_g, ((1,), (1,))) * ecs[:, gsl]
        slabs = []
        for m in range(SSM_HPG // 2):
            lsl = slice(gsl.start + m * LANES, gsl.start + (m + 1) * LANES)
            x_pair = xdt[:, lsl]
            acc_pair = None
            for half in range(2):
                h = g * SSM_HPG + 2 * m + half
                seg = cs[:, h:h + 1] - cs_t[h:h + 1, :]
                lmat = jnp.where(causal, jnp.exp(jnp.where(causal, seg, 0.0)), 0.0)
                x_half = jnp.where(lo_half if half == 0 else jnp.logical_not(lo_half), x_pair, 0.0)
                contrib = _dot(cb * lmat, x_half)
                acc_pair = contrib if acc_pair is None else acc_pair + contrib
            slabs.append(acc_pair)
        y_groups.append(jnp.concatenate(slabs, axis=1) + y_off)

        xw = xdt[:, gsl] * dte[:, gsl]
        tn = ((0,), (0,))
        if exact_state:
            xh, xm, xl = _split3(xw)
            bh, bmid, bl = _split3(b_g)
            upd = _dot(jnp.concatenate([xh, xh, xm, xh, xl, xm], axis=0),
                       jnp.concatenate([bh, bmid, bh, bl, bh, bmid], axis=0), tn)
        else:
            upd = _dot(xw, b_g, tn)
        for j in range(SSM_HPG):
            h = g * SSM_HPG + j
            dec = jnp.exp(cs_t[h:h + 1, q - 1:q])
            s_ref[0, h] = s_ref[0, h] * dec + upd[j * SSM_HEAD_DIM:(j + 1) * SSM_HEAD_DIM, :]

    y = jnp.concatenate(y_groups, axis=1) + dskip_ref[...] * xs
    y = y * _silu(z_ref[...])
    outs = []
    gw = SSM_D_INNER // SSM_N_GROUPS
    for g in range(SSM_N_GROUPS):
        yg = y[:, g * gw:(g + 1) * gw]
        outs.append(yg * lax.rsqrt(jnp.mean(yg * yg, axis=-1, keepdims=True) + SSM_NORM_EPS))
    y_ref[...] = jnp.concatenate(outs, axis=1) * ng_ref[...]


def ssd_branch(xbc, z, dtr, conv0, s0, conv_w, conv_b, dt_bias, a_log, d_skip, norm_g, *, t, q, t_valid,
               exact_state, name):
    nb = s0.shape[0]
    cps = t // q
    n_pad = conv0.shape[1]
    pad_l = lambda v: jnp.pad(v.astype(F32), (0, LANES - v.shape[0])).reshape(1, LANES)
    expand = (jnp.arange(LANES)[:, None] == (jnp.arange(SSM_D_INNER) // SSM_HEAD_DIM)[None, :]).astype(BF16)
    row2 = lambda v: v.reshape(1, -1)
    full = lambda a: pl.BlockSpec(a.shape, lambda b, c: (0,) * a.ndim)
    consts = [conv_w, row2(conv_b), pad_l(dt_bias), pad_l(a_log), row2(jnp.repeat(d_skip, SSM_HEAD_DIM)),
              row2(norm_g), expand]
    kern = functools.partial(_ssd_kernel, q=q, t_valid=t_valid, exact_state=exact_state)
    return pl.pallas_call(
        kern,
        out_shape=(jax.ShapeDtypeStruct((nb * t, SSM_D_INNER), F32),
                   jax.ShapeDtypeStruct(s0.shape, F32)),
        grid=(nb, cps),
        in_specs=[pl.BlockSpec((q, SSM_CONV_DIM), lambda b, c: (b * cps + c, 0)),
                  pl.BlockSpec((q, SSM_D_INNER), lambda b, c: (b * cps + c, 0)),
                  pl.BlockSpec((q, LANES), lambda b, c: (b * cps + c, 0)),
                  pl.BlockSpec((1, n_pad, SSM_CONV_DIM), lambda b, c: (b, 0, 0)),
                  pl.BlockSpec((1,) + s0.shape[1:], lambda b, c: (b, 0, 0, 0))] + [full(a) for a in consts],
        out_specs=(pl.BlockSpec((q, SSM_D_INNER), lambda b, c: (b * cps + c, 0)),
                   pl.BlockSpec((1,) + s0.shape[1:], lambda b, c: (b, 0, 0, 0))),
        scratch_shapes=[pltpu.VMEM((q + n_pad, SSM_CONV_DIM), F32)],
        compiler_params=_cparams("parallel", "arbitrary"),
        name=name,
    )(xbc, z, dtr, conv0, s0, *consts)


RWKV_GROUP_HEADS = 4
RWKV_GROUP_LANES = RWKV_GROUP_HEADS * RWKV_HEAD


def _rep_rows(x, n):
    return jnp.concatenate([x] * n, axis=0)


def _head_mask(rows_per_head, cols_per_head, n_rows, n_cols):
    rr = lax.broadcasted_iota(jnp.int32, (n_rows, n_cols), 0) // rows_per_head
    cc = lax.broadcasted_iota(jnp.int32, (n_rows, n_cols), 1) // cols_per_head
    return rr == cc


def _rwkv_prep(cur, prev, live, mu_ref, w0_ref, w2_ref, a0_ref, a2_ref, kk_ref, ka_ref, rk_ref, hred_ref,
               hexp_ref):
    rows = cur.shape[0]
    mixed = cur + (prev - cur) * mu_ref[...]
    r = mixed[:, 0:RWKV_DIM]
    k = mixed[:, RWKV_DIM:2 * RWKV_DIM]
    v = mixed[:, 2 * RWKV_DIM:3 * RWKV_DIM]
    lora = mixed[:, 3 * RWKV_DIM:]
    lane = lax.broadcasted_iota(jnp.int32, (rows, LANES), 1)
    wl = jnp.where(lane < RWKV_LORA, jnp.tanh(lora), 0.0)
    al = jnp.where(lane >= RWKV_LORA, lora, 0.0)
    w_log = -_softplus(-(w0_ref[...] + _dot_f32(wl, w2_ref[...], terms=3))) - 0.5
    a = _sigmoid(a0_ref[...] + _dot_f32(al, a2_ref[...], terms=3))
    logw = -jnp.exp(w_log)
    kk_raw = k * kk_ref[...]
    k2 = k * (1.0 + (a - 1.0) * ka_ref[...])
    sums = _dot_split_lhs(jnp.concatenate([kk_raw * kk_raw, r * k2 * rk_ref[...]], axis=0), hred_ref[...], 2)
    inv_n = 1.0 / jnp.maximum(jnp.sqrt(sums[:rows]), 1e-12)
    kk = kk_raw * _dot_split_lhs(inv_n, hexp_ref[...], 2)
    if live is not None:
        logw = jnp.where(live, logw, 0.0)
        kk = jnp.where(live, kk, 0.0)
        k2 = jnp.where(live, k2, 0.0)
        v = jnp.where(live, v, 0.0)
    return r, k2, v, kk, a, logw, sums[rows:]


def _rwkv_post(y, v, bonus_sum, gng_ref, gnb_ref, hred_ref, hexp_ref):
    rows = y.shape[0]
    inv_h = 1.0 / RWKV_HEAD
    sums = _dot_split_lhs(jnp.concatenate([y, y * y], axis=0), hred_ref[...], 3) * inv_h
    mean = sums[:rows]
    var = jnp.maximum(sums[rows:] - mean * mean, 0.0)
    per_head = jnp.concatenate([mean, lax.rsqrt(var + RWKV_GN_EPS), bonus_sum], axis=0)
    wide = _dot_split_lhs(per_head, hexp_ref[...], 2)
    return ((y - wide[:rows]) * wide[rows:2 * rows] * gng_ref[...] + gnb_ref[...]
            + wide[2 * rows:] * v)


def _state_to_blockdiag(stack, m_state):
    return jnp.where(m_state, jnp.concatenate([stack] * RWKV_GROUP_HEADS, axis=1), 0.0)


def _blockdiag_to_state(s_bd):
    stack = s_bd[:, 0:RWKV_HEAD]
    for hh in range(1, RWKV_GROUP_HEADS):
        stack = stack + s_bd[:, hh * RWKV_HEAD:(hh + 1) * RWKV_HEAD]
    return stack


def _rwkv_kernel(rw_ref, shift0_ref, wkv0_ref, mu_ref, w0_ref, w2_ref, a0_ref, a2_ref, kk_ref, ka_ref, rk_ref,
                 gng_ref, gnb_ref, hred_ref, hexp_ref,
                 y_ref, wkv_ref, buf_ref, s_ref, *, cq, t_valid):
    c = pl.program_id(1)
    n_chunks = pl.num_programs(1)
    n_pad = buf_ref.shape[0] - cq
    gl = RWKV_GROUP_LANES
    gh = RWKV_GROUP_HEADS
    n_groups = RWKV_DIM // gl
    nt = ((1,), (1,))
    tn = ((0,), (0,))

    m_state = _head_mask(RWKV_HEAD, RWKV_HEAD, gl, gl)
    m_exp = _head_mask(cq, RWKV_HEAD, gh * cq, gl)
    m_nbd = _head_mask(cq, cq, gh * cq, gh * cq)

    @pl.when(c == 0)
    def _():
        buf_ref[0:n_pad, :] = shift0_ref[0]
        for g in range(n_groups):
            s_ref[g] = _state_to_blockdiag(wkv0_ref[0, g * gl:(g + 1) * gl, :], m_state)

    buf_ref[n_pad:n_pad + cq, :] = rw_ref[...]
    cur = buf_ref[n_pad:n_pad + cq, :]
    prev = buf_ref[n_pad - 1:n_pad - 1 + cq, :]
    buf_ref[0:n_pad, :] = buf_ref[cq:cq + n_pad, :]
    row = lax.broadcasted_iota(jnp.int32, (cq, 1), 0) + c * cq
    r, k2, v, kk, a, logw, bonus_sum = _rwkv_prep(cur, prev, row < t_valid, mu_ref, w0_ref, w2_ref, a0_ref, a2_ref,
                                                  kk_ref, ka_ref, rk_ref, hred_ref, hexp_ref)

    ti = lax.broadcasted_iota(jnp.int32, (cq, cq), 0)
    si = lax.broadcasted_iota(jnp.int32, (cq, cq), 1)
    tril = (si <= ti).astype(BF16)
    lh, lm, ll = _split3(logw)
    cl = (jnp.dot(tril, lh, preferred_element_type=F32) + jnp.dot(tril, lm, preferred_element_type=F32)
          + jnp.dot(tril, ll, preferred_element_type=F32))
    p_incl = jnp.exp(cl)
    p_inv = jnp.exp(-cl)
    a_t = jnp.exp(cl - logw) * (-kk)
    r_t = p_incl * r
    b_t = p_inv * (kk * a)
    k_t = p_inv * k2
    p_end = p_incl[cq - 1:cq, :]

    tt = lax.broadcasted_iota(jnp.int32, (cq, gh * cq), 0)
    ss_i = lax.broadcasted_iota(jnp.int32, (cq, gh * cq), 1) % cq
    strict = ss_i < tt
    incl = ss_i <= tt

    def expand_rows(x):
        return jnp.where(m_exp, _rep_rows(x, gh), 0.0)

    groups = range(n_groups)
    sls = [slice(g * gl, (g + 1) * gl) for g in groups]
    lhs_ar = [jnp.concatenate([a_t[:, sl], r_t[:, sl]], axis=0) for sl in sls]
    v_exp = [expand_rows(v[:, sl]) for sl in sls]
    s_bd = [s_ref[g] for g in groups]
    ab = [_dot(lhs_ar[g], expand_rows(b_t[:, sls[g]]), nt) for g in groups]
    ak = [_dot(lhs_ar[g], expand_rows(k_t[:, sls[g]]), nt) for g in groups]
    w_s = [_dot(lhs_ar[g], s_bd[g], nt) for g in groups]
    x = [w_s[g][:cq] + _dot(jnp.where(strict, ak[g][:cq], 0.0), v_exp[g]) for g in groups]
    n_cat = [jnp.where(strict, ab[g][:cq], 0.0) for g in groups]
    n_steps = int(math.log2(cq))
    for step in range(n_steps):
        x = [x[g] + _dot(n_cat[g], expand_rows(x[g])) for g in groups]
        if step + 1 < n_steps:
            n_cat = [_dot(n_cat[g], jnp.where(m_nbd, _rep_rows(n_cat[g], gh), 0.0)) for g in groups]
    y_groups = [w_s[g][cq:] + _dot(jnp.where(incl, ab[g][cq:], 0.0), expand_rows(x[g]))
                + _dot(jnp.where(incl, ak[g][cq:], 0.0), v_exp[g]) for g in groups]
    for g in groups:
        uv = jnp.concatenate([x[g], v[:, sls[g]]], axis=0)
        bk = jnp.concatenate([b_t[:, sls[g]], k_t[:, sls[g]]], axis=0)
        inc = _dot(uv, bk, tn)
        s_ref[g] = jnp.where(m_state, (s_bd[g] + inc) * p_end[:, sls[g]], 0.0)

    y = jnp.concatenate(y_groups, axis=1)
    y_ref[...] = _rwkv_post(y, v, bonus_sum, gng_ref, gnb_ref, hred_ref, hexp_ref)

    @pl.when(c == n_chunks - 1)
    def _():
        for g in range(n_groups):
            wkv_ref[0, g * gl:(g + 1) * gl, :] = _blockdiag_to_state(s_ref[g])


def _rwkv_consts(mu, w0, w2, a0, a2, k_k, k_a, r_k, gn_g, gn_b):
    row2 = lambda a: a.reshape(1, -1).astype(F32)
    zeros_l = jnp.zeros((RWKV_LORA, RWKV_DIM), F32)
    head_of_lane = jnp.arange(RWKV_DIM) // RWKV_HEAD
    hexp = (jnp.arange(LANES)[:, None] == head_of_lane[None, :]).astype(BF16)
    return [row2(mu), row2(w0), jnp.concatenate([w2, zeros_l], axis=0), row2(a0),
            jnp.concatenate([zeros_l, a2], axis=0), row2(k_k), row2(k_a), row2(r_k), row2(gn_g), row2(gn_b),
            hexp.T, hexp]


def _rwkv_step_kernel(rw_ref, prev_ref, wkv0_ref, mu_ref, w0_ref, w2_ref, a0_ref, a2_ref, kk_ref, ka_ref, rk_ref,
                      gng_ref, gnb_ref, hred_ref, hexp_ref, y_ref, wkv_ref):
    bt = rw_ref.shape[0]
    gl = RWKV_GROUP_LANES
    n_groups = RWKV_DIM // gl
    nt = ((1,), (1,))
    tn = ((0,), (0,))
    m_state = _head_mask(RWKV_HEAD, RWKV_HEAD, gl, gl)
    r, k2, v, kk, a, logw, bonus_sum = _rwkv_prep(rw_ref[...], prev_ref[...], None, mu_ref, w0_ref, w2_ref, a0_ref,
                                                  a2_ref, kk_ref, ka_ref, rk_ref, hred_ref, hexp_ref)
    w = jnp.exp(logw)
    a_hat = -kk
    b = kk * a
    rowid = lax.broadcasted_iota(jnp.int32, (bt, gl), 0)
    groups = range(n_groups)
    sls = [slice(g * gl, (g + 1) * gl) for g in groups]
    y_acc = [jnp.zeros((bt, gl), F32) for _ in groups]
    for bb in range(bt):
        mine = rowid == bb
        hsl = [slice(g * RWKV_GROUP_HEADS, (g + 1) * RWKV_GROUP_HEADS) for g in groups]
        s_bd = [_state_to_blockdiag(wkv0_ref[bb, hs].reshape(gl, RWKV_HEAD), m_state) for hs in hsl]
        u = [_dot(a_hat[:, sl], s_bd[g], nt) for g, sl in enumerate(sls)]
        s_new = []
        for g, sl in enumerate(sls):
            uv = jnp.concatenate([jnp.where(mine, u[g], 0.0), jnp.where(mine, v[:, sl], 0.0)], axis=0)
            bk = jnp.concatenate([b[:, sl], k2[:, sl]], axis=0)
            uh, um, ul = _split3(uv)
            bh, bmid, bl = _split3(bk)
            inc = _dot(jnp.concatenate([uh, uh, um, uh, ul, um], axis=0),
                       jnp.concatenate([bh, bmid, bh, bl, bh, bmid], axis=0), tn)
            s_new.append(jnp.where(m_state, s_bd[g] * w[bb:bb + 1, sl] + inc, 0.0))
        for g, sl in enumerate(sls):
            y_acc[g] = jnp.where(mine, _dot(r[:, sl], s_new[g], nt), y_acc[g])
            wkv_ref[bb, hsl[g]] = _blockdiag_to_state(s_new[g]).reshape(RWKV_GROUP_HEADS, RWKV_HEAD, RWKV_HEAD)
    y = jnp.concatenate(y_acc, axis=1)
    y_ref[...] = _rwkv_post(y, v, bonus_sum, gng_ref, gnb_ref, hred_ref, hexp_ref)


def rwkv_step(rw, row0, prev, wkv0, *params, bt, name):
    nb = prev.shape[0]
    consts = _rwkv_consts(*params)
    full = lambda a: pl.BlockSpec(a.shape, lambda i: (0,) * a.ndim)
    state_spec = pl.BlockSpec((bt,) + wkv0.shape[1:], lambda i: (i, 0, 0, 0))
    return pl.pallas_call(
        _rwkv_step_kernel,
        out_shape=(jax.ShapeDtypeStruct((nb, RWKV_DIM), F32), jax.ShapeDtypeStruct(wkv0.shape, F32)),
        grid=(nb // bt,),
        in_specs=[pl.BlockSpec((bt, RWKV_SHIFT_DIM), lambda i: (row0 // bt + i, 0)),
                  pl.BlockSpec((bt, RWKV_SHIFT_DIM), lambda i: (i, 0)),
                  state_spec] + [full(a) for a in consts],
        out_specs=(pl.BlockSpec((bt, RWKV_DIM), lambda i: (i, 0)), state_spec),
        compiler_params=_cparams("parallel"),
        name=name,
    )(rw, prev, wkv0, *consts)


def rwkv_branch(rw, shift0, wkv0, *params, t, cq, t_valid, name):
    nb = wkv0.shape[0]
    cps = t // cq
    n_pad = shift0.shape[1]
    consts = _rwkv_consts(*params)
    full = lambda a: pl.BlockSpec(a.shape, lambda b, c: (0,) * a.ndim)
    wkv_rows = wkv0.reshape(nb, RWKV_DIM, RWKV_HEAD)
    kern = functools.partial(_rwkv_kernel, cq=cq, t_valid=t_valid)
    y, wkv = pl.pallas_call(
        kern,
        out_shape=(jax.ShapeDtypeStruct((nb * t, RWKV_DIM), F32),
                   jax.ShapeDtypeStruct(wkv_rows.shape, F32)),
        grid=(nb, cps),
        in_specs=[pl.BlockSpec((cq, RWKV_SHIFT_DIM), lambda b, c: (b * cps + c, 0)),
                  pl.BlockSpec((1, n_pad, RWKV_SHIFT_DIM), lambda b, c: (b, 0, 0)),
                  pl.BlockSpec((1, RWKV_DIM, RWKV_HEAD), lambda b, c: (b, 0, 0))] + [full(a) for a in consts],
        out_specs=(pl.BlockSpec((cq, RWKV_DIM), lambda b, c: (b * cps + c, 0)),
                   pl.BlockSpec((1, RWKV_DIM, RWKV_HEAD), lambda b, c: (b, 0, 0))),
        scratch_shapes=[pltpu.VMEM((cq + n_pad, RWKV_SHIFT_DIM), F32),
                        pltpu.VMEM((RWKV_DIM // RWKV_GROUP_LANES, RWKV_GROUP_LANES, RWKV_GROUP_LANES), F32)],
        compiler_params=_cparams("parallel", "arbitrary"),
        name=name,
    )(rw, shift0, wkv_rows, *consts)
    return y, wkv.reshape(wkv0.shape)


NEG_BIG = -1e30


def _mix_kernel(g_ref, yap_ref, ybp_ref, yas_ref, ybs_ref, h_ref, wout_ref, ln2_ref, rw_ref, rb_ref,
                h1_ref, xn_ref, idx_ref, gate_ref):
    i = pl.program_id(0)
    last = i == pl.num_programs(0) - 1
    ya = jnp.where(last, yas_ref[...], yap_ref[...])
    yb = jnp.where(last, ybs_ref[...], ybp_ref[...])
    g = g_ref[...]
    merged = _sigmoid(g[:, :D_MODEL]) * ya + _sigmoid(g[:, D_MODEL:]) * yb
    h1 = h_ref[...] + jnp.dot(merged.astype(BF16), wout_ref[...], preferred_element_type=F32)
    h1_ref[...] = h1
    ms = jnp.mean(h1 * h1, axis=-1, keepdims=True)
    xn = h1 * lax.rsqrt(ms + RMS_EPS) * ln2_ref[...]
    xn_ref[...] = xn
    xh, xm, xl = _split3(xn)
    logits = jnp.dot(jnp.concatenate([xh, xh, xm, xh, xl, xm], axis=1), rw_ref[...],
                     preferred_element_type=F32) + rb_ref[...]
    lane = lax.broadcasted_iota(jnp.int32, logits.shape, 1).astype(F32)
    idx_out = jnp.zeros(logits.shape, F32)
    val_out = jnp.zeros(logits.shape, F32)
    top0 = None
    for kth in range(TOP_K):
        m = jnp.max(logits, axis=-1, keepdims=True)
        sel = jnp.min(jnp.where(logits == m, lane, float(LANES)), axis=-1, keepdims=True)
        if kth == 0:
            top0 = m
        idx_out = jnp.where(lane == kth, sel, idx_out)
        val_out = jnp.where(lane == kth, jnp.exp(m - top0), val_out)
        logits = jnp.where(lane == sel, -jnp.inf, logits)
    idx_ref[...] = idx_out.astype(jnp.int32)
    gate_ref[...] = val_out / jnp.sum(val_out, axis=-1, keepdims=True)


def mix_and_route(gates, ya_p, yb_p, ya_s, yb_s, h_rows, w_out_bf, ln2_g, router_w, router_b, tm):
    rows = h_rows.shape[0]
    n_tiles = rows // tm
    last_p = ya_p.shape[0] // tm - 1
    wh, wm, wl = _split3(jnp.pad(router_w, ((0, 0), (0, LANES - N_EXPERTS))))
    rw_pad = jnp.concatenate([wh, wm, wh, wl, wh, wm], axis=0)
    rb_pad = jnp.pad(router_b, (0, LANES - N_EXPERTS), constant_values=NEG_BIG).reshape(1, LANES)
    row_spec = lambda n: pl.BlockSpec((tm, n), lambda i: (i, 0))
    prompt_spec = pl.BlockSpec((tm, D_MODEL), lambda i: (jnp.minimum(i, last_p), 0))
    fixed = lambda a: pl.BlockSpec(a.shape, lambda i: (0,) * a.ndim)
    return pl.pallas_call(
        _mix_kernel,
        out_shape=(jax.ShapeDtypeStruct((rows, D_MODEL), F32), jax.ShapeDtypeStruct((rows, D_MODEL), F32),
                   jax.ShapeDtypeStruct((rows, LANES), jnp.int32), jax.ShapeDtypeStruct((rows, LANES), F32)),
        grid=(n_tiles,),
        in_specs=[row_spec(2 * D_MODEL), prompt_spec, prompt_spec, fixed(ya_s), fixed(yb_s), row_spec(D_MODEL),
                  fixed(w_out_bf), pl.BlockSpec((1, D_MODEL), lambda i: (0, 0)), fixed(rw_pad), fixed(rb_pad)],
        out_specs=(row_spec(D_MODEL), row_spec(D_MODEL), row_spec(LANES), row_spec(LANES)),
        compiler_params=_cparams("parallel"),
        name="mix_and_route",
    )(gates, ya_p, yb_p, ya_s, yb_s, h_rows, w_out_bf, ln2_g.reshape(1, D_MODEL), rw_pad, rb_pad)


def next_run_expert(block_e):
    n = block_e.shape[0]
    idx = jnp.arange(n, dtype=jnp.int32)
    change = jnp.concatenate([jnp.ones((1,), bool), block_e[1:] != block_e[:-1]])
    later_change = jnp.concatenate([jnp.where(change, idx, n)[1:], jnp.full((1,), n, jnp.int32)])
    next_pos = lax.cummin(later_change, axis=0, reverse=True)
    onehot = next_pos[:, None] == idx[None, :]
    return jnp.where(next_pos < n, jnp.sum(jnp.where(onehot, block_e[None, :], 0), axis=1), -1).astype(jnp.int32)


def _stream_expert_weights(be_ref, nxt_ref, copies, cast):
    i = pl.program_id(0)
    first = jnp.logical_or(i == 0, be_ref[i] != be_ref[jnp.maximum(i, 1) - 1])

    @pl.when(i == 0)
    def _():
        for cp in copies(be_ref[0]):
            cp.start()

    @pl.when(first)
    def _():
        for cp in copies(be_ref[i]):
            cp.wait()
        cast()

    @pl.when(jnp.logical_and(first, nxt_ref[i] >= 0))
    def _():
        for cp in copies(nxt_ref[i]):
            cp.start()


def _moe_up_kernel(be_ref, nxt_ref, nb_ref, tok_ref, tok_next_ref, x_hbm, w_hbm, b_ref, o_ref, w_f32, w_bf, sem,
                   rows_ref, rsem):
    i = pl.program_id(0)
    n_used = nb_ref[0]
    slot = lax.rem(i, 2)

    def row_copies(idx_ref, buf):
        return [pltpu.make_async_copy(x_hbm.at[pl.ds(idx_ref[0, 0, r], 1), :],
                                      rows_ref.at[buf, pl.ds(r, 1), :], rsem.at[buf]) for r in range(MOE_BLOCK)]

    @pl.when(jnp.logical_and(i == 0, n_used > 0))
    def _():
        for cp in row_copies(tok_ref, 0):
            cp.start()

    @pl.when(i + 1 < n_used)
    def _():
        for cp in row_copies(tok_next_ref, 1 - slot):
            cp.start()

    def copies(e):
        return [pltpu.make_async_copy(w_hbm.at[e, :, pl.ds(half * D_FF, D_FF)], w_f32.at[half], sem.at[half])
                for half in range(2)]

    def cast():
        w_bf[...] = w_f32[...].astype(BF16)

    _stream_expert_weights(be_ref, nxt_ref, copies, cast)

    @pl.when(i < n_used)
    def _():
        for cp in row_copies(tok_ref, slot):
            cp.wait()
        x = rows_ref[slot].astype(BF16)
        g = jnp.dot(x, w_bf[0], preferred_element_type=F32) + b_ref[0, :, :D_FF]
        u = jnp.dot(x, w_bf[1], preferred_element_type=F32) + b_ref[0, :, D_FF:]
        g = jnp.minimum(g, SWIGLU_LIMIT)
        u = jnp.clip(u, -SWIGLU_LIMIT, SWIGLU_LIMIT)
        o_ref[...] = ((u + 1.0) * (g * _sigmoid(g * SWIGLU_ALPHA))).astype(o_ref.dtype)

    @pl.when(i >= n_used)
    def _():
        o_ref[...] = jnp.zeros_like(o_ref)


def moe_up(x, slot_tok, block_e, next_e, n_used, w_gate_up, b_gate_up):
    n_blocks = slot_tok.shape[0] // MOE_BLOCK
    tok3 = slot_tok.reshape(n_blocks, 1, MOE_BLOCK)
    grid_spec = pltpu.PrefetchScalarGridSpec(
        num_scalar_prefetch=3,
        grid=(n_blocks,),
        in_specs=[pl.BlockSpec((1, 1, MOE_BLOCK), lambda i, be, nx, nb: (i, 0, 0), memory_space=pltpu.SMEM),
                  pl.BlockSpec((1, 1, MOE_BLOCK), lambda i, be, nx, nb: (jnp.minimum(i + 1, n_blocks - 1), 0, 0),
                               memory_space=pltpu.SMEM),
                  pl.BlockSpec(memory_space=pl.ANY),
                  pl.BlockSpec(memory_space=pl.ANY),
                  pl.BlockSpec((1, 1, 2 * D_FF), lambda i, be, nx, nb: (be[i], 0, 0))],
        out_specs=pl.BlockSpec((MOE_BLOCK, D_FF), lambda i, be, nx, nb: (i, 0)),
        scratch_shapes=[pltpu.VMEM((2, D_MODEL, D_FF), F32), pltpu.VMEM((2, D_MODEL, D_FF), BF16),
                        pltpu.SemaphoreType.DMA((2,)),
                        pltpu.VMEM((2, MOE_BLOCK, D_MODEL), x.dtype), pltpu.SemaphoreType.DMA((2,))],
    )
    return pl.pallas_call(
        _moe_up_kernel,
        out_shape=jax.ShapeDtypeStruct((n_blocks * MOE_BLOCK, D_FF), BF16),
        grid_spec=grid_spec,
        compiler_params=_cparams("arbitrary"),
        name="moe_up",
    )(block_e, next_e, n_used, tok3, tok3, x, w_gate_up, b_gate_up.reshape(N_EXPERTS, 1, 2 * D_FF))


def _moe_down_kernel(be_ref, nxt_ref, nb_ref, h_ref, w_hbm, bd_ref, o_ref, w_f32, w_bf, sem):
    i = pl.program_id(0)

    def copies(e):
        return [pltpu.make_async_copy(w_hbm.at[e], w_f32, sem)]

    def cast():
        w_bf[...] = w_f32[...].astype(BF16)

    _stream_expert_weights(be_ref, nxt_ref, copies, cast)

    @pl.when(i < nb_ref[0])
    def _():
        o_ref[...] = jnp.dot(h_ref[...], w_bf[...], preferred_element_type=F32) + bd_ref[0]

    @pl.when(i >= nb_ref[0])
    def _():
        o_ref[...] = jnp.zeros_like(o_ref)


def moe_down(hb, block_e, next_e, n_used, w_down, b_down):
    n_blocks = hb.shape[0] // MOE_BLOCK
    grid_spec = pltpu.PrefetchScalarGridSpec(
        num_scalar_prefetch=3,
        grid=(n_blocks,),
        in_specs=[pl.BlockSpec((MOE_BLOCK, D_FF), lambda i, be, nx, nb: (i, 0)),
                  pl.BlockSpec(memory_space=pl.ANY),
                  pl.BlockSpec((1, 1, D_MODEL), lambda i, be, nx, nb: (be[i], 0, 0))],
        out_specs=pl.BlockSpec((MOE_BLOCK, D_MODEL), lambda i, be, nx, nb: (i, 0)),
        scratch_shapes=[pltpu.VMEM((D_FF, D_MODEL), F32), pltpu.VMEM((D_FF, D_MODEL), BF16),
                        pltpu.SemaphoreType.DMA(())],
    )
    return pl.pallas_call(
        _moe_down_kernel,
        out_shape=jax.ShapeDtypeStruct((n_blocks * MOE_BLOCK, D_MODEL), F32),
        grid_spec=grid_spec,
        compiler_params=_cparams("arbitrary"),
        name="moe_down",
    )(block_e, next_e, n_used, hb, w_down, b_down.reshape(N_EXPERTS, 1, D_MODEL))


def _combine_kernel(slot_ref, slot_next_ref, row0_ref, gate_ref, lnf_ref, yb_hbm, h_hbm, o_ref, rows_ref, h_vmem,
                    sem, hsem):
    i = pl.program_id(0)
    n_tiles = pl.num_programs(0)
    tm = o_ref.shape[-2]
    slot = lax.rem(i, 2)

    def copies(idx_ref, tile, buf):
        row0 = pl.multiple_of(row0_ref[tile], SUBLANES)
        cps = [pltpu.make_async_copy(h_hbm.at[pl.ds(row0, tm), :], h_vmem.at[buf], hsem.at[buf])]
        for n in range(TOP_K * tm):
            cps.append(pltpu.make_async_copy(yb_hbm.at[pl.ds(idx_ref[0, 0, n], 1), :],
                                             rows_ref.at[buf, n // tm, pl.ds(n % tm, 1), :], sem.at[buf]))
        return cps

    @pl.when(i == 0)
    def _():
        for cp in copies(slot_ref, 0, 0):
            cp.start()

    @pl.when(i + 1 < n_tiles)
    def _():
        for cp in copies(slot_next_ref, jnp.minimum(i + 1, n_tiles - 1), 1 - slot):
            cp.start()

    for cp in copies(slot_ref, i, slot):
        cp.wait()
    gate = gate_ref[...]
    acc = h_vmem[slot]
    for k in range(TOP_K):
        acc = acc + rows_ref[slot, k] * gate[:, k:k + 1]
    ms = jnp.mean(acc * acc, axis=-1, keepdims=True)
    o_ref[...] = (acc * lax.rsqrt(ms + RMS_EPS) * lnf_ref[...]).reshape(o_ref.shape)


def moe_combine(yb, h1, slot_of_pair, gates, row0, lnf_g, out_shape, out_index_map, tm):
    n_tiles = row0.shape[0]
    grid_spec = pltpu.PrefetchScalarGridSpec(
        num_scalar_prefetch=0,
        grid=(n_tiles,),
        in_specs=[pl.BlockSpec((1, 1, TOP_K * tm), lambda i: (i, 0, 0), memory_space=pltpu.SMEM),
                  pl.BlockSpec((1, 1, TOP_K * tm), lambda i: (jnp.minimum(i + 1, n_tiles - 1), 0, 0),
                               memory_space=pltpu.SMEM),
                  pl.BlockSpec(memory_space=pltpu.SMEM),
                  pl.BlockSpec((tm, LANES), lambda i: (i, 0)),
                  pl.BlockSpec((1, D_MODEL), lambda i: (0, 0)),
                  pl.BlockSpec(memory_space=pl.ANY),
                  pl.BlockSpec(memory_space=pl.ANY)],
        out_specs=pl.BlockSpec(out_shape[0], out_index_map),
        scratch_shapes=[pltpu.VMEM((2, TOP_K, tm, D_MODEL), F32), pltpu.VMEM((2, tm, D_MODEL), F32),
                        pltpu.SemaphoreType.DMA((2,)), pltpu.SemaphoreType.DMA((2,))],
    )
    return pl.pallas_call(
        _combine_kernel,
        out_shape=jax.ShapeDtypeStruct(out_shape[1], F32),
        grid_spec=grid_spec,
        compiler_params=_cparams("arbitrary"),
        name="moe_combine",
    )(slot_of_pair, slot_of_pair, row0, gates, lnf_g.reshape(1, D_MODEL), yb, h1)


PROMPT_PAD_T = 2176
SAMPLE_PAD_T = SUBLANES
ROW_TILE = 1104


def kernel(x_prompt, x_sample, state_ssm_conv, state_ssm, state_rwkv_shift, state_rwkv_wkv, meta_tokens, ln1_g, w_in, ssm_conv_w, ssm_conv_b, ssm_dt_bias, ssm_A_log, ssm_D, ssm_norm_g, rwkv_mu, rwkv_w0, rwkv_w2, rwkv_a0, rwkv_a2, rwkv_k_k, rwkv_k_a, rwkv_r_k, rwkv_gn_g, rwkv_gn_b, w_out, ln2_g, router_w, router_b, w_gate_up, b_gate_up, w_down, b_down, lnf_g):
    bp, seq, d = x_prompt.shape
    bs = x_sample.shape[0]
    t_prompt = N_META + seq
    tail = jnp.zeros((PROMPT_PAD_T - t_prompt, d), F32)
    pieces = []
    for b in range(bp):
        pieces += [meta_tokens, x_prompt[b], tail]
    h_rows = jnp.concatenate(pieces + [x_sample.reshape(bs, d)], axis=0)
    n_prow = bp * PROMPT_PAD_T

    l = 0
    xn = rmsnorm_rows(h_rows, ln1_g[l], BF16, ROW_TILE)
    w = w_in[l]
    o_xbc = SSM_D_INNER
    o_dt = o_xbc + SSM_CONV_DIM
    o_rw = o_dt + SSM_N_HEADS
    o_g = o_rw + RWKV_SHIFT_DIM
    w_z = w[:, :o_xbc].astype(BF16)
    w_xbc = w[:, o_xbc:o_dt].astype(BF16)
    w_dt = jnp.pad(w[:, o_dt:o_rw], ((0, 0), (0, LANES - SSM_N_HEADS))).astype(BF16)
    w_rw = w[:, o_rw:o_g].astype(BF16)
    w_g = w[:, o_g:].astype(BF16)
    z = matmul_bf16(xn, w_z, ROW_TILE, 1024, "proj_z")
    xbc = matmul_bf16(xn, w_xbc, ROW_TILE, 1024, "proj_xbc")
    dtr = matmul_bf16(xn, w_dt, ROW_TILE, LANES, "proj_dt")

    def sample_rows(a):
        n = a.shape[1]
        return jnp.pad(a[n_prow:].reshape(bs, 1, n), ((0, 0), (0, SAMPLE_PAD_T - 1), (0, 0))).reshape(-1, n)

    ssm_args = (ssm_conv_w[l], ssm_conv_b[l], ssm_dt_bias[l], ssm_A_log[l], ssm_D[l], ssm_norm_g[l])
    conv0_p = jnp.zeros((bp, SUBLANES, SSM_CONV_DIM), F32)
    s0_p = jnp.zeros((bp, SSM_N_HEADS, SSM_HEAD_DIM, SSM_D_STATE), F32)
    ya_p, ssm_p = ssd_branch(xbc, z, dtr, conv0_p, s0_p, *ssm_args, t=PROMPT_PAD_T, q=SSM_CHUNK,
                             t_valid=t_prompt, exact_state=False, name="ssd_prompt")
    conv0_s = jnp.pad(state_ssm_conv[l], ((0, 0), (SUBLANES - (SSM_CONV - 1), 0), (0, 0)))
    ya_s, ssm_s = ssd_branch(sample_rows(xbc), sample_rows(z), sample_rows(dtr), conv0_s, state_ssm[l], *ssm_args,
                             t=SAMPLE_PAD_T, q=SAMPLE_PAD_T, t_valid=1, exact_state=True, name="ssd_sample")
    ya_s = ya_s.reshape(bs, SAMPLE_PAD_T, d)[:, 0]
    rw = matmul_bf16(xn, w_rw, ROW_TILE, 896, "proj_rw")
    rwkv_args = (rwkv_mu[l], rwkv_w0[l], rwkv_w2[l], rwkv_a0[l], rwkv_a2[l], rwkv_k_k[l], rwkv_k_a[l],
                 rwkv_r_k[l], rwkv_gn_g[l], rwkv_gn_b[l])
    shift0_p = jnp.zeros((bp, SUBLANES, RWKV_SHIFT_DIM), F32)
    wkv0_p = jnp.zeros((bp, RWKV_N_HEADS, RWKV_HEAD, RWKV_HEAD), F32)
    yb_p, wkv_p = rwkv_branch(rw, shift0_p, wkv0_p, *rwkv_args, t=PROMPT_PAD_T, cq=RWKV_CHUNK, t_valid=t_prompt,
                              name="rwkv_prompt")
    yb_s, wkv_s = rwkv_step(rw, n_prow, state_rwkv_shift[l], state_rwkv_wkv[l], *rwkv_args, bt=SUBLANES,
                            name="rwkv_sample")
    gates = matmul_bf16(xn, w_g, ROW_TILE, 1024, "proj_gates")
    h1, xn2, top_idx, top_gate = mix_and_route(
        gates, ya_p, yb_p, ya_s, yb_s, h_rows,
        w_out[l].astype(BF16), ln2_g[l], router_w[l], router_b[l], MOE_BLOCK)

    n_rows = h_rows.shape[0]
    row_id = jnp.arange(n_rows, dtype=jnp.int32)
    valid = jnp.logical_or(row_id >= n_prow, row_id % PROMPT_PAD_T < t_prompt)
    yb, slot_of_pair = moe_expert_rows(xn2, top_idx, valid, (bp * t_prompt + bs) * TOP_K, w_gate_up[l],
                                       b_gate_up[l], w_down[l], b_down[l])

    def combine(row0, out_block, out_full, out_map):
        rows = (row0[:, None] + jnp.arange(MOE_BLOCK, dtype=jnp.int32)[None, :])
        slots = jnp.transpose(slot_of_pair[rows], (0, 2, 1)).reshape(row0.shape[0], 1, TOP_K * MOE_BLOCK)
        return moe_combine(yb, h1, slots, top_gate[rows.reshape(-1)], row0, lnf_g, (out_block, out_full),
                           out_map, MOE_BLOCK)

    tiles_per_seq = seq // MOE_BLOCK
    tile_id = jnp.arange(bp * tiles_per_seq, dtype=jnp.int32)
    row0_p = (tile_id // tiles_per_seq) * PROMPT_PAD_T + N_META + (tile_id % tiles_per_seq) * MOE_BLOCK
    y_prompt = combine(row0_p, (1, MOE_BLOCK, d), (bp, seq, d),
                       lambda i: (i // tiles_per_seq, i % tiles_per_seq, 0))
    y_sample = combine(jnp.full((1,), n_prow, jnp.int32), (MOE_BLOCK, d), (bs, d), lambda i: (i, 0))

    last = [b * PROMPT_PAD_T + t_prompt - 1 for b in range(bp)]
    prompt_conv = jnp.stack([xbc[r - (SSM_CONV - 2):r + 1] for r in last])
    prompt_shift = jnp.stack([rw[r] for r in last])
    sample_conv = jnp.concatenate([state_ssm_conv[l][:, 1:], xbc[n_prow:, None]], axis=1)
    return (y_prompt, y_sample.reshape(bs, 1, d), prompt_conv[None], ssm_p[None], prompt_shift[None], wkv_p[None],
            sample_conv[None], ssm_s[None], rw[n_prow:][None], wkv_s[None])


def moe_expert_rows(xn2, top_idx, valid, n_valid_pairs, w_gate_up, b_gate_up, w_down, b_down):
    n_rows = xn2.shape[0]
    experts = jnp.arange(N_EXPERTS, dtype=jnp.int32)
    key = jnp.where(valid[:, None], top_idx[:, :TOP_K], N_EXPERTS).reshape(-1)
    n_pairs = key.shape[0]
    n_blocks = -(-n_valid_pairs // MOE_BLOCK) + N_EXPERTS
    n_slots = n_blocks * MOE_BLOCK
    counts = jnp.sum((key[:, None] == experts[None, :]).astype(jnp.int32), axis=0)
    padded = (counts + MOE_BLOCK - 1) // MOE_BLOCK * MOE_BLOCK
    ends = jnp.cumsum(padded)
    fill_e = jnp.repeat(experts, MOE_BLOCK)
    fill_r = jnp.tile(jnp.arange(MOE_BLOCK, dtype=jnp.int32), N_EXPERTS)
    fill_key = jnp.where(fill_r < jnp.repeat(padded - counts, MOE_BLOCK), fill_e, N_EXPERTS + 1)
    keys = jnp.concatenate([key, fill_key])
    iota = jnp.arange(keys.shape[0], dtype=jnp.int32)
    _, order = lax.sort((keys, iota), num_keys=1, is_stable=True)
    _, slot_of = lax.sort((order, iota), num_keys=1, is_stable=True)
    slot_tok = jnp.where(order[:n_slots] < n_pairs, order[:n_slots] // TOP_K, 0).astype(jnp.int32)
    slot_of_pair = jnp.minimum(slot_of[:n_pairs], n_slots - 1).reshape(n_rows, TOP_K)
    block_start = jnp.arange(n_blocks, dtype=jnp.int32) * MOE_BLOCK
    block_e = jnp.minimum(jnp.sum((ends[None, :] <= block_start[:, None]).astype(jnp.int32), axis=1),
                          N_EXPERTS - 1).astype(jnp.int32)
    n_used = (ends[-1:] // MOE_BLOCK).astype(jnp.int32)
    next_e = next_run_expert(block_e)

    hb = moe_up(xn2, slot_tok, block_e, next_e, n_used, w_gate_up, b_gate_up)
    yb = moe_down(hb, block_e, next_e, n_used, w_down, b_down)
    return yb, slot_of_pair
```

```python
import functools
import math

import jax
import jax.numpy as jnp
from jax import lax
from jax.experimental import pallas as pl
from jax.experimental.pallas import tpu as pltpu

F32 = jnp.float32
BF16 = jnp.bfloat16

V7X_VMEM_LIMIT_BYTES = 60000 * 1024
LANES = 128
SUBLANES = 8

D_MODEL = 2048
N_META = 16
RMS_EPS = 1e-6
SSM_D_INNER = 2048
SSM_HEAD_DIM = 64
SSM_N_HEADS = 32
SSM_N_GROUPS = 4
SSM_HPG = 8
SSM_D_STATE = 128
SSM_CONV = 4
SSM_CHUNK = 128
SSM_CONV_DIM = SSM_D_INNER + 2 * SSM_N_GROUPS * SSM_D_STATE
SSM_NORM_EPS = 1e-5
RWKV_DIM = 2048
RWKV_HEAD = 64
RWKV_N_HEADS = 32
RWKV_LORA = 64
RWKV_GN_EPS = 64e-5
RWKV_SHIFT_DIM = 3 * RWKV_DIM + 2 * RWKV_LORA
RWKV_CHUNK = 64
N_EXPERTS = 32
TOP_K = 4
D_FF = 2048
SWIGLU_LIMIT = 7.0
SWIGLU_ALPHA = 1.702
MOE_BLOCK = 128


def _cparams(*sem):
    return pltpu.CompilerParams(dimension_semantics=sem, vmem_limit_bytes=V7X_VMEM_LIMIT_BYTES)


def _split3(x):
    hi = x.astype(BF16)
    r1 = x - hi.astype(F32)
    mid = r1.astype(BF16)
    lo = (r1 - mid.astype(F32)).astype(BF16)
    return hi, mid, lo


def _dot(a, b, dims=None):
    a = a.astype(BF16)
    b = b.astype(BF16)
    if dims is None:
        return jnp.dot(a, b, preferred_element_type=F32)
    return lax.dot_general(a, b, (dims, ((), ())), preferred_element_type=F32)


def _dot_exact_lhs(x, m01):
    hi, mid, lo = _split3(x)
    m = m01.astype(BF16)
    return (jnp.dot(hi, m, preferred_element_type=F32) + jnp.dot(mid, m, preferred_element_type=F32)
            + jnp.dot(lo, m, preferred_element_type=F32))


def _dot_split_lhs(x, m01, terms):
    m = m01.astype(BF16)
    out = None
    for part in _split3(x)[:terms]:
        d = jnp.dot(part, m, preferred_element_type=F32)
        out = d if out is None else out + d
    return out


def _dot_f32(a, b, terms=6):
    ah, am, al = _split3(a)
    bh, bm, bl = _split3(b)
    lhs = jnp.concatenate([ah, ah, am, ah, al, am][:terms], axis=1)
    rhs = jnp.concatenate([bh, bm, bh, bl, bh, bm][:terms], axis=0)
    return jnp.dot(lhs, rhs, preferred_element_type=F32)


def _softplus(x):
    return jnp.maximum(x, 0.0) + jnp.log(1.0 + jnp.exp(-jnp.abs(x)))


def _sigmoid(x):
    return 1.0 / (1.0 + jnp.exp(-x))


def _silu(x):
    return x * _sigmoid(x)


def _rmsnorm_kernel(x_ref, g_ref, o_ref, *, eps):
    x = x_ref[...]
    ms = jnp.mean(x * x, axis=-1, keepdims=True)
    o_ref[...] = (x * lax.rsqrt(ms + eps) * g_ref[...]).astype(o_ref.dtype)


def rmsnorm_rows(x, g, out_dtype, tm):
    rows, d = x.shape
    return pl.pallas_call(
        functools.partial(_rmsnorm_kernel, eps=RMS_EPS),
        out_shape=jax.ShapeDtypeStruct((rows, d), out_dtype),
        grid=(rows // tm,),
        in_specs=[pl.BlockSpec((tm, d), lambda i: (i, 0)), pl.BlockSpec((1, d), lambda i: (0, 0))],
        out_specs=pl.BlockSpec((tm, d), lambda i: (i, 0)),
        compiler_params=_cparams("parallel"),
        name="rmsnorm_rows",
    )(x, g.reshape(1, d))


def _matmul_kernel(x_ref, w_ref, o_ref):
    o_ref[...] = jnp.dot(x_ref[...], w_ref[...], preferred_element_type=F32)


def matmul_bf16(x, w, tm, tn, name):
    rows, k = x.shape
    n = w.shape[1]
    return pl.pallas_call(
        _matmul_kernel,
        out_shape=jax.ShapeDtypeStruct((rows, n), F32),
        grid=(n // tn, rows // tm),
        in_specs=[pl.BlockSpec((tm, k), lambda j, i: (i, 0)), pl.BlockSpec((k, tn), lambda j, i: (0, j))],
        out_specs=pl.BlockSpec((tm, tn), lambda j, i: (i, j)),
        compiler_params=_cparams("parallel", "parallel"),
        name=name,
    )(x, w)


def _ssd_kernel(xbc_ref, z_ref, dtr_ref, conv0_ref, s0_ref, convw_ref, convb_ref, dtb_ref, alog_ref,
                dskip_ref, ng_ref, expand_ref, y_ref, s_ref, buf_ref, *, q, t_valid, exact_state):
    c = pl.program_id(1)
    n_pad = buf_ref.shape[0] - q

    @pl.when(c == 0)
    def _():
        buf_ref[0:n_pad, :] = conv0_ref[0]
        s_ref[0] = s0_ref[0]

    buf_ref[n_pad:n_pad + q, :] = xbc_ref[...]
    acc = convb_ref[...] + convw_ref[SSM_CONV - 1:SSM_CONV, :] * buf_ref[n_pad:n_pad + q, :]
    for k in range(SSM_CONV - 1):
        off = n_pad - (SSM_CONV - 1) + k
        acc = acc + convw_ref[k:k + 1, :] * buf_ref[off:off + q, :]
    buf_ref[0:n_pad, :] = buf_ref[q:q + n_pad, :]
    xbc = _silu(acc)
    gn = SSM_N_GROUPS * SSM_D_STATE
    xs = xbc[:, :SSM_D_INNER]
    bm = xbc[:, SSM_D_INNER:SSM_D_INNER + gn]
    cm = xbc[:, SSM_D_INNER + gn:]

    row = lax.broadcasted_iota(jnp.int32, (q, LANES), 0) + c * q
    lane = lax.broadcasted_iota(jnp.int32, (q, LANES), 1)
    live = jnp.logical_and(row < t_valid, lane < SSM_N_HEADS)
    dt = jnp.where(live, _softplus(dtr_ref[...] + dtb_ref[...]), 0.0)
    da = dt * (-jnp.exp(alog_ref[...]))

    ti = lax.broadcasted_iota(jnp.int32, (q, q), 0)
    si = lax.broadcasted_iota(jnp.int32, (q, q), 1)
    causal = si <= ti
    tril = causal.astype(BF16)
    dh, dm, dl = _split3(da)
    cs = (jnp.dot(tril, dh, preferred_element_type=F32) + jnp.dot(tril, dm, preferred_element_type=F32)
          + jnp.dot(tril, dl, preferred_element_type=F32))
    eye = (lax.broadcasted_iota(jnp.int32, (LANES, LANES), 0)
           == lax.broadcasted_iota(jnp.int32, (LANES, LANES), 1)).astype(BF16)
    ch, cmid, cl = _split3(cs)
    nt = (((1,), (1,)), ((), ()))
    cs_t = (lax.dot_general(eye, ch, nt, preferred_element_type=F32)
            + lax.dot_general(eye, cmid, nt, preferred_element_type=F32)
            + lax.dot_general(eye, cl, nt, preferred_element_type=F32))

    expand = expand_ref[...]
    dt_x = _dot_exact_lhs(dt, expand)
    cs_x = _dot_exact_lhs(cs, expand)
    xdt = xs * dt_x
    ecs = jnp.exp(cs_x)
    dte = jnp.exp(cs_x[q - 1:q, :] - cs_x)

    lane_in_pair = lax.broadcasted_iota(jnp.int32, (q, LANES), 1)
    lo_half = lane_in_pair < SSM_HEAD_DIM

    y_groups = []
    for g in range(SSM_N_GROUPS):
        b_g = bm[:, g * SSM_D_STATE:(g + 1) * SSM_D_STATE]
        c_g = cm[:, g * SSM_D_STATE:(g + 1) * SSM_D_STATE]
        cb = _dot(c_g, b_g, ((1,), (1,)))
        gsl = slice(g * SSM_HPG * SSM_HEAD_DIM, (g + 1) * SSM_HPG * SSM_HEAD_DIM)
        s_g = s_ref[0, g * SSM_HPG:(g + 1) * SSM_HPG].reshape(SSM_HPG * SSM_HEAD_DIM, SSM_D_STATE)
        y_off = _dot(c_g, s_g, ((1,), (1,))) * ecs[:, gsl]
        slabs = []
        for m in range(SSM_HPG // 2):
            lsl = slice(gsl.start + m * LANES, gsl.start + (m + 1) * LANES)
            x_pair = xdt[:, lsl]
            acc_pair = None
            for half in range(2):
                h = g * SSM_HPG + 2 * m + half
                seg = cs[:, h:h + 1] - cs_t[h:h + 1, :]
                lmat = jnp.where(causal, jnp.exp(jnp.where(causal, seg, 0.0)), 0.0)
                x_half = jnp.where(lo_half if half == 0 else jnp.logical_not(lo_half), x_pair, 0.0)
                contrib = _dot(cb * lmat, x_half)
                acc_pair = contrib if acc_pair is None else acc_pair + contrib
            slabs.append(acc_pair)
        y_groups.append(jnp.concatenate(slabs, axis=1) + y_off)

        xw = xdt[:, gsl] * dte[:, gsl]
        tn = ((0,), (0,))
        if exact_state:
            xh, xm, xl = _split3(xw)
            bh, bmid, bl = _split3(b_g)
            upd = _dot(jnp.concatenate([xh, xh, xm, xh, xl, xm], axis=0),
                       jnp.concatenate([bh, bmid, bh, bl, bh, bmid], axis=0), tn)
        else:
            upd = _dot(xw, b_g, tn)
        for j in range(SSM_HPG):
            h = g * SSM_HPG + j
            dec = jnp.exp(cs_t[h:h + 1, q - 1:q])
            s_ref[0, h] = s_ref[0, h] * dec + upd[j * SSM_HEAD_DIM:(j + 1) * SSM_HEAD_DIM, :]

    y = jnp.concatenate(y_groups, axis=1) + dskip_ref[...] * xs
    y = y * _silu(z_ref[...])
    outs = []
    gw = SSM_D_INNER // SSM_N_GROUPS
    for g in range(SSM_N_GROUPS):
        yg = y[:, g * gw:(g + 1) * gw]
        outs.append(yg * lax.rsqrt(jnp.mean(yg * yg, axis=-1, keepdims=True) + SSM_NORM_EPS))
    y_ref[...] = jnp.concatenate(outs, axis=1) * ng_ref[...]


def ssd_branch(xbc, z, dtr, conv0, s0, conv_w, conv_b, dt_bias, a_log, d_skip, norm_g, *, t, q, t_valid,
               exact_state, name):
    nb = s0.shape[0]
    cps = t // q
    n_pad = conv0.shape[1]
    pad_l = lambda v: jnp.pad(v.astype(F32), (0, LANES - v.shape[0])).reshape(1, LANES)
    expand = (jnp.arange(LANES)[:, None] == (jnp.arange(SSM_D_INNER) // SSM_HEAD_DIM)[None, :]).astype(BF16)
    row2 = lambda v: v.reshape(1, -1)
    full = lambda a: pl.BlockSpec(a.shape, lambda b, c: (0,) * a.ndim)
    consts = [conv_w, row2(conv_b), pad_l(dt_bias), pad_l(a_log), row2(jnp.repeat(d_skip, SSM_HEAD_DIM)),
              row2(norm_g), expand]
    kern = functools.partial(_ssd_kernel, q=q, t_valid=t_valid, exact_state=exact_state)
    return pl.pallas_call(
        kern,
        out_shape=(jax.ShapeDtypeStruct((nb * t, SSM_D_INNER), F32),
                   jax.ShapeDtypeStruct(s0.shape, F32)),
        grid=(nb, cps),
        in_specs=[pl.BlockSpec((q, SSM_CONV_DIM), lambda b, c: (b * cps + c, 0)),
                  pl.BlockSpec((q, SSM_D_INNER), lambda b, c: (b * cps + c, 0)),
                  pl.BlockSpec((q, LANES), lambda b, c: (b * cps + c, 0)),
                  pl.BlockSpec((1, n_pad, SSM_CONV_DIM), lambda b, c: (b, 0, 0)),
                  pl.BlockSpec((1,) + s0.shape[1:], lambda b, c: (b, 0, 0, 0))] + [full(a) for a in consts],
        out_specs=(pl.BlockSpec((q, SSM_D_INNER), lambda b, c: (b * cps + c, 0)),
                   pl.BlockSpec((1,) + s0.shape[1:], lambda b, c: (b, 0, 0, 0))),
        scratch_shapes=[pltpu.VMEM((q + n_pad, SSM_CONV_DIM), F32)],
        compiler_params=_cparams("parallel", "arbitrary"),
        name=name,
    )(xbc, z, dtr, conv0, s0, *consts)


RWKV_GROUP_HEADS = 4
RWKV_GROUP_LANES = RWKV_GROUP_HEADS * RWKV_HEAD


def _rep_rows(x, n):
    return jnp.concatenate([x] * n, axis=0)


def _head_mask(rows_per_head, cols_per_head, n_rows, n_cols):
    rr = lax.broadcasted_iota(jnp.int32, (n_rows, n_cols), 0) // rows_per_head
    cc = lax.broadcasted_iota(jnp.int32, (n_rows, n_cols), 1) // cols_per_head
    return rr == cc


def _rwkv_prep(cur, prev, live, mu_ref, w0_ref, w2_ref, a0_ref, a2_ref, kk_ref, ka_ref, rk_ref, hred_ref,
               hexp_ref):
    rows = cur.shape[0]
    mixed = cur + (prev - cur) * mu_ref[...]
    r = mixed[:, 0:RWKV_DIM]
    k = mixed[:, RWKV_DIM:2 * RWKV_DIM]
    v = mixed[:, 2 * RWKV_DIM:3 * RWKV_DIM]
    lora = mixed[:, 3 * RWKV_DIM:]
    lane = lax.broadcasted_iota(jnp.int32, (rows, LANES), 1)
    wl = jnp.where(lane < RWKV_LORA, jnp.tanh(lora), 0.0)
    al = jnp.where(lane >= RWKV_LORA, lora, 0.0)
    logw = -math.exp(-0.5) * _sigmoid(w0_ref[...] + _dot_f32(wl, w2_ref[...], terms=3))
    a = _sigmoid(a0_ref[...] + _dot_f32(al, a2_ref[...], terms=3))
    kk_raw = k * kk_ref[...]
    k2 = k * (1.0 + (a - 1.0) * ka_ref[...])
    sums = _dot_split_lhs(jnp.concatenate([kk_raw * kk_raw, r * k2 * rk_ref[...]], axis=0), hred_ref[...], 2)
    inv_n = 1.0 / jnp.maximum(jnp.sqrt(sums[:rows]), 1e-12)
    kk = kk_raw * _dot_split_lhs(inv_n, hexp_ref[...], 2)
    if live is not None:
        logw = jnp.where(live, logw, 0.0)
        kk = jnp.where(live, kk, 0.0)
        k2 = jnp.where(live, k2, 0.0)
        v = jnp.where(live, v, 0.0)
    return r, k2, v, kk, a, logw, sums[rows:]


def _rwkv_post(y, v, bonus_sum, gng_ref, gnb_ref, hred_ref, hexp_ref):
    rows = y.shape[0]
    inv_h = 1.0 / RWKV_HEAD
    sums = _dot_split_lhs(jnp.concatenate([y, y * y], axis=0), hred_ref[...], 3) * inv_h
    mean = sums[:rows]
    var = jnp.maximum(sums[rows:] - mean * mean, 0.0)
    per_head = jnp.concatenate([mean, lax.rsqrt(var + RWKV_GN_EPS), bonus_sum], axis=0)
    wide = _dot_split_lhs(per_head, hexp_ref[...], 2)
    return ((y - wide[:rows]) * wide[rows:2 * rows] * gng_ref[...] + gnb_ref[...]
            + wide[2 * rows:] * v)


def _state_to_blockdiag(stack, m_state):
    return jnp.where(m_state, jnp.concatenate([stack] * RWKV_GROUP_HEADS, axis=1), 0.0)


def _blockdiag_to_state(s_bd):
    stack = s_bd[:, 0:RWKV_HEAD]
    for hh in range(1, RWKV_GROUP_HEADS):
        stack = stack + s_bd[:, hh * RWKV_HEAD:(hh + 1) * RWKV_HEAD]
    return stack


def _rwkv_kernel(rw_ref, shift0_ref, wkv0_ref, mu_ref, w0_ref, w2_ref, a0_ref, a2_ref, kk_ref, ka_ref, rk_ref,
                 gng_ref, gnb_ref, hred_ref, hexp_ref,
                 y_ref, wkv_ref, buf_ref, s_ref, *, cq, t_valid):
    c = pl.program_id(1)
    n_chunks = pl.num_programs(1)
    n_pad = buf_ref.shape[0] - cq
    gl = RWKV_GROUP_LANES
    gh = RWKV_GROUP_HEADS
    n_groups = RWKV_DIM // gl
    nt = ((1,), (1,))
    tn = ((0,), (0,))

    m_state = _head_mask(RWKV_HEAD, RWKV_HEAD, gl, gl)
    m_exp = _head_mask(cq, RWKV_HEAD, gh * cq, gl)
    m_nbd = _head_mask(cq, cq, gh * cq, gh * cq)

    @pl.when(c == 0)
    def _():
        buf_ref[0:n_pad, :] = shift0_ref[0]
        for g in range(n_groups):
            s_ref[g] = _state_to_blockdiag(wkv0_ref[0, g * gl:(g + 1) * gl, :], m_state)

    buf_ref[n_pad:n_pad + cq, :] = rw_ref[...]
    cur = buf_ref[n_pad:n_pad + cq, :]
    prev = buf_ref[n_pad - 1:n_pad - 1 + cq, :]
    buf_ref[0:n_pad, :] = buf_ref[cq:cq + n_pad, :]
    row = lax.broadcasted_iota(jnp.int32, (cq, 1), 0) + c * cq
    r, k2, v, kk, a, logw, bonus_sum = _rwkv_prep(cur, prev, row < t_valid, mu_ref, w0_ref, w2_ref, a0_ref, a2_ref,
                                                  kk_ref, ka_ref, rk_ref, hred_ref, hexp_ref)

    ti = lax.broadcasted_iota(jnp.int32, (cq, cq), 0)
    si = lax.broadcasted_iota(jnp.int32, (cq, cq), 1)
    tril = (si <= ti).astype(BF16)
    lh, lm, ll = _split3(logw)
    cl = (jnp.dot(tril, lh, preferred_element_type=F32) + jnp.dot(tril, lm, preferred_element_type=F32)
          + jnp.dot(tril, ll, preferred_element_type=F32))
    p_incl = jnp.exp(cl)
    p_inv = jnp.exp(-cl)
    a_t = jnp.exp(cl - logw) * (-kk)
    r_t = p_incl * r
    b_t = p_inv * (kk * a)
    k_t = p_inv * k2
    p_end = p_incl[cq - 1:cq, :]

    tt = lax.broadcasted_iota(jnp.int32, (cq, gh * cq), 0)
    ss_i = lax.broadcasted_iota(jnp.int32, (cq, gh * cq), 1) % cq
    strict = ss_i < tt
    incl = ss_i <= tt

    def expand_rows(x):
        return jnp.where(m_exp, _rep_rows(x, gh), 0.0)

    groups = range(n_groups)
    sls = [slice(g * gl, (g + 1) * gl) for g in groups]
    lhs_ar = [jnp.concatenate([a_t[:, sl], r_t[:, sl]], axis=0) for sl in sls]
    v_exp = [expand_rows(v[:, sl]) for sl in sls]
    s_bd = [s_ref[g] for g in groups]
    ab = [_dot(lhs_ar[g], expand_rows(b_t[:, sls[g]]), nt) for g in groups]
    ak = [_dot(lhs_ar[g], expand_rows(k_t[:, sls[g]]), nt) for g in groups]
    w_s = [_dot(lhs_ar[g], s_bd[g], nt) for g in groups]
    x = [w_s[g][:cq] + _dot(jnp.where(strict, ak[g][:cq], 0.0), v_exp[g]) for g in groups]
    n_cat = [jnp.where(strict, ab[g][:cq], 0.0) for g in groups]
    n_steps = int(math.log2(cq))
    for step in range(n_steps):
        x = [x[g] + _dot(n_cat[g], expand_rows(x[g])) for g in groups]
        if step + 1 < n_steps:
            n_cat = [_dot(n_cat[g], jnp.where(m_nbd, _rep_rows(n_cat[g], gh), 0.0)) for g in groups]
    y_groups = [w_s[g][cq:] + _dot(jnp.where(incl, ab[g][cq:], 0.0), expand_rows(x[g]))
                + _dot(jnp.where(incl, ak[g][cq:], 0.0), v_exp[g]) for g in groups]
    for g in groups:
        uv = jnp.concatenate([x[g], v[:, sls[g]]], axis=0)
        bk = jnp.concatenate([b_t[:, sls[g]], k_t[:, sls[g]]], axis=0)
        inc = _dot(uv, bk, tn)
        s_ref[g] = jnp.where(m_state, (s_bd[g] + inc) * p_end[:, sls[g]], 0.0)

    y = jnp.concatenate(y_groups, axis=1)
    y_ref[...] = _rwkv_post(y, v, bonus_sum, gng_ref, gnb_ref, hred_ref, hexp_ref)

    @pl.when(c == n_chunks - 1)
    def _():
        for g in range(n_groups):
            wkv_ref[0, g * gl:(g + 1) * gl, :] = _blockdiag_to_state(s_ref[g])


def _rwkv_consts(mu, w0, w2, a0, a2, k_k, k_a, r_k, gn_g, gn_b):
    row2 = lambda a: a.reshape(1, -1).astype(F32)
    zeros_l = jnp.zeros((RWKV_LORA, RWKV_DIM), F32)
    head_of_lane = jnp.arange(RWKV_DIM) // RWKV_HEAD
    hexp = (jnp.arange(LANES)[:, None] == head_of_lane[None, :]).astype(BF16)
    return [row2(mu), row2(w0), jnp.concatenate([w2, zeros_l], axis=0), row2(a0),
            jnp.concatenate([zeros_l, a2], axis=0), row2(k_k), row2(k_a), row2(r_k), row2(gn_g), row2(gn_b),
            hexp.T, hexp]


def _rwkv_step_prep_kernel(rw_ref, prev_ref, mu_ref, w0_ref, w2_ref, a0_ref, a2_ref, kk_ref, ka_ref, rk_ref,
                           gng_ref, gnb_ref, hred_ref, hexp_ref, vecs_ref, v_ref, bsum_ref):
    r, k2, v, kk, a, logw, bonus_sum = _rwkv_prep(rw_ref[...], prev_ref[...], None, mu_ref, w0_ref, w2_ref, a0_ref,
                                                  a2_ref, kk_ref, ka_ref, rk_ref, hred_ref, hexp_ref)
    for n, vec in enumerate((-kk, kk * a, jnp.exp(logw), k2, r, v)):
        vecs_ref[n] = vec.T
    v_ref[...] = v
    bsum_ref[...] = bonus_sum


def _rwkv_step_state_kernel(vecs_ref, s_ref, s_out_ref, yt_ref):
    a_t, b_t, w_t, k_t, r_t = (vecs_ref[n] for n in range(5))

    def body(i, carry):
        s_i = s_ref[0, i]
        u = jnp.sum(s_i * a_t, axis=0, keepdims=True)
        s_new = s_i * w_t + u * b_t + vecs_ref[5, pl.ds(i, 1), :] * k_t
        s_out_ref[0, i] = s_new
        yt_ref[pl.ds(i, 1), :] = jnp.sum(s_new * r_t, axis=0, keepdims=True)
        return carry

    lax.fori_loop(0, RWKV_HEAD, body, 0, unroll=4)


def _rwkv_step_post_kernel(yt_ref, v_ref, bsum_ref, gng_ref, gnb_ref, hred_ref, hexp_ref, y_ref):
    y_ref[...] = _rwkv_post(yt_ref[...].T, v_ref[...], bsum_ref[...], gng_ref, gnb_ref, hred_ref, hexp_ref)


def rwkv_step(rw, row0, prev, wkv_t, *params, name):
    nb = prev.shape[0]
    consts = _rwkv_consts(*params)
    full = lambda a: pl.BlockSpec(a.shape, lambda i: (0,) * a.ndim)
    n_vec = 6
    vecs, v_rows, bsum = pl.pallas_call(
        _rwkv_step_prep_kernel,
        out_shape=(jax.ShapeDtypeStruct((n_vec, RWKV_DIM, nb), F32), jax.ShapeDtypeStruct((nb, RWKV_DIM), F32),
                   jax.ShapeDtypeStruct((nb, LANES), F32)),
        grid=(1,),
        in_specs=[pl.BlockSpec((nb, RWKV_SHIFT_DIM), lambda i: (row0 // nb, 0)),
                  pl.BlockSpec((nb, RWKV_SHIFT_DIM), lambda i: (0, 0))] + [full(a) for a in consts],
        out_specs=(pl.BlockSpec((n_vec, RWKV_DIM, nb), lambda i: (0, 0, 0)),
                   pl.BlockSpec((nb, RWKV_DIM), lambda i: (0, 0)), pl.BlockSpec((nb, LANES), lambda i: (0, 0))),
        compiler_params=_cparams("arbitrary"),
        name=name + "_prep",
    )(rw, prev, *consts)
    state_spec = pl.BlockSpec((1, RWKV_HEAD, RWKV_HEAD, nb), lambda h: (h, 0, 0, 0))
    wkv_new, y_t = pl.pallas_call(
        _rwkv_step_state_kernel,
        out_shape=(jax.ShapeDtypeStruct(wkv_t.shape, F32), jax.ShapeDtypeStruct((RWKV_DIM, nb), F32)),
        grid=(RWKV_N_HEADS,),
        in_specs=[pl.BlockSpec((n_vec, RWKV_HEAD, nb), lambda h: (0, h, 0)), state_spec],
        out_specs=(state_spec, pl.BlockSpec((RWKV_HEAD, nb), lambda h: (h, 0))),
        compiler_params=_cparams("parallel"),
        name=name + "_state",
    )(vecs, wkv_t)
    post_consts = consts[8:]
    y = pl.pallas_call(
        _rwkv_step_post_kernel,
        out_shape=jax.ShapeDtypeStruct((nb, RWKV_DIM), F32),
        grid=(1,),
        in_specs=[pl.BlockSpec((RWKV_DIM, nb), lambda i: (0, 0)), pl.BlockSpec((nb, RWKV_DIM), lambda i: (0, 0)),
                  pl.BlockSpec((nb, LANES), lambda i: (0, 0))] + [full(a) for a in post_consts],
        out_specs=pl.BlockSpec((nb, RWKV_DIM), lambda i: (0, 0)),
        compiler_params=_cparams("arbitrary"),
        name=name + "_post",
    )(y_t, v_rows, bsum, *post_consts)
    return y, wkv_new


def rwkv_branch(rw, shift0, wkv0, *params, t, cq, t_valid, name):
    nb = wkv0.shape[0]
    cps = t // cq
    n_pad = shift0.shape[1]
    consts = _rwkv_consts(*params)
    full = lambda a: pl.BlockSpec(a.shape, lambda b, c: (0,) * a.ndim)
    wkv_rows = wkv0.reshape(nb, RWKV_DIM, RWKV_HEAD)
    kern = functools.partial(_rwkv_kernel, cq=cq, t_valid=t_valid)
    y, wkv = pl.pallas_call(
        kern,
        out_shape=(jax.ShapeDtypeStruct((nb * t, RWKV_DIM), F32),
                   jax.ShapeDtypeStruct(wkv_rows.shape, F32)),
        grid=(nb, cps),
        in_specs=[pl.BlockSpec((cq, RWKV_SHIFT_DIM), lambda b, c: (b * cps + c, 0)),
                  pl.BlockSpec((1, n_pad, RWKV_SHIFT_DIM), lambda b, c: (b, 0, 0)),
                  pl.BlockSpec((1, RWKV_DIM, RWKV_HEAD), lambda b, c: (b, 0, 0))] + [full(a) for a in consts],
        out_specs=(pl.BlockSpec((cq, RWKV_DIM), lambda b, c: (b * cps + c, 0)),
                   pl.BlockSpec((1, RWKV_DIM, RWKV_HEAD), lambda b, c: (b, 0, 0))),
        scratch_shapes=[pltpu.VMEM((cq + n_pad, RWKV_SHIFT_DIM), F32),
                        pltpu.VMEM((RWKV_DIM // RWKV_GROUP_LANES, RWKV_GROUP_LANES, RWKV_GROUP_LANES), F32)],
        compiler_params=_cparams("parallel", "arbitrary"),
        name=name,
    )(rw, shift0, wkv_rows, *consts)
    return y, wkv.reshape(wkv0.shape)


NEG_BIG = -1e30


def _mix_kernel(g_ref, yap_ref, ybp_ref, yas_ref, ybs_ref, h_ref, wout_ref, ln2_ref, rw_ref, rb_ref,
                h1_ref, xn_ref, idx_ref, gate_ref):
    i = pl.program_id(0)
    last = i == pl.num_programs(0) - 1
    ya = jnp.where(last, yas_ref[...], yap_ref[...])
    yb = jnp.where(last, ybs_ref[...], ybp_ref[...])
    g = g_ref[...]
    merged = _sigmoid(g[:, :D_MODEL]) * ya + _sigmoid(g[:, D_MODEL:]) * yb
    h1 = h_ref[...] + jnp.dot(merged.astype(BF16), wout_ref[...], preferred_element_type=F32)
    h1_ref[...] = h1
    ms = jnp.mean(h1 * h1, axis=-1, keepdims=True)
    xn = h1 * lax.rsqrt(ms + RMS_EPS) * ln2_ref[...]
    xn_ref[...] = xn
    xh, xm, xl = _split3(xn)
    logits = jnp.dot(jnp.concatenate([xh, xh, xm, xh, xl, xm], axis=1), rw_ref[...],
                     preferred_element_type=F32) + rb_ref[...]
    lane = lax.broadcasted_iota(jnp.int32, logits.shape, 1).astype(F32)
    idx_out = jnp.zeros(logits.shape, F32)
    val_out = jnp.zeros(logits.shape, F32)
    top0 = None
    for kth in range(TOP_K):
        m = jnp.max(logits, axis=-1, keepdims=True)
        sel = jnp.min(jnp.where(logits == m, lane, float(LANES)), axis=-1, keepdims=True)
        if kth == 0:
            top0 = m
        idx_out = jnp.where(lane == kth, sel, idx_out)
        val_out = jnp.where(lane == kth, jnp.exp(m - top0), val_out)
        logits = jnp.where(lane == sel, -jnp.inf, logits)
    idx_ref[...] = idx_out.astype(jnp.int32)
    gate_ref[...] = val_out / jnp.sum(val_out, axis=-1, keepdims=True)


def mix_and_route(gates, ya_p, yb_p, ya_s, yb_s, h_rows, w_out_bf, ln2_g, router_w, router_b, tm):
    rows = h_rows.shape[0]
    n_tiles = rows // tm
    last_p = ya_p.shape[0] // tm - 1
    wh, wm, wl = _split3(jnp.pad(router_w, ((0, 0), (0, LANES - N_EXPERTS))))
    rw_pad = jnp.concatenate([wh, wm, wh, wl, wh, wm], axis=0)
    rb_pad = jnp.pad(router_b, (0, LANES - N_EXPERTS), constant_values=NEG_BIG).reshape(1, LANES)
    row_spec = lambda n: pl.BlockSpec((tm, n), lambda i: (i, 0))
    prompt_spec = pl.BlockSpec((tm, D_MODEL), lambda i: (jnp.minimum(i, last_p), 0))
    fixed = lambda a: pl.BlockSpec(a.shape, lambda i: (0,) * a.ndim)
    return pl.pallas_call(
        _mix_kernel,
        out_shape=(jax.ShapeDtypeStruct((rows, D_MODEL), F32), jax.ShapeDtypeStruct((rows, D_MODEL), F32),
                   jax.ShapeDtypeStruct((rows, LANES), jnp.int32), jax.ShapeDtypeStruct((rows, LANES), F32)),
        grid=(n_tiles,),
        in_specs=[row_spec(2 * D_MODEL), prompt_spec, prompt_spec, fixed(ya_s), fixed(yb_s), row_spec(D_MODEL),
                  fixed(w_out_bf), pl.BlockSpec((1, D_MODEL), lambda i: (0, 0)), fixed(rw_pad), fixed(rb_pad)],
        out_specs=(row_spec(D_MODEL), row_spec(D_MODEL), row_spec(LANES), row_spec(LANES)),
        compiler_params=_cparams("parallel"),
        name="mix_and_route",
    )(gates, ya_p, yb_p, ya_s, yb_s, h_rows, w_out_bf, ln2_g.reshape(1, D_MODEL), rw_pad, rb_pad)


def next_run_expert(block_e):
    n = block_e.shape[0]
    idx = jnp.arange(n, dtype=jnp.int32)
    change = jnp.concatenate([jnp.ones((1,), bool), block_e[1:] != block_e[:-1]])
    later_change = jnp.concatenate([jnp.where(change, idx, n)[1:], jnp.full((1,), n, jnp.int32)])
    next_pos = lax.cummin(later_change, axis=0, reverse=True)
    onehot = next_pos[:, None] == idx[None, :]
    return jnp.where(next_pos < n, jnp.sum(jnp.where(onehot, block_e[None, :], 0), axis=1), -1).astype(jnp.int32)


def _stream_expert_weights(be_ref, nxt_ref, copies, cast):
    i = pl.program_id(0)
    first = jnp.logical_or(i == 0, be_ref[i] != be_ref[jnp.maximum(i, 1) - 1])

    @pl.when(i == 0)
    def _():
        for cp in copies(be_ref[0]):
            cp.start()

    @pl.when(first)
    def _():
        for cp in copies(be_ref[i]):
            cp.wait()
        cast()

    @pl.when(jnp.logical_and(first, nxt_ref[i] >= 0))
    def _():
        for cp in copies(nxt_ref[i]):
            cp.start()


def _moe_up_kernel(be_ref, nxt_ref, nb_ref, tok_ref, tok_next_ref, x_hbm, w_hbm, b_ref, o_ref, w_f32, w_bf, sem,
                   rows_ref, rsem):
    i = pl.program_id(0)
    n_used = nb_ref[0]
    slot = lax.rem(i, 2)

    def row_copies(idx_ref, buf):
        return [pltpu.make_async_copy(x_hbm.at[pl.ds(idx_ref[0, 0, r], 1), :],
                                      rows_ref.at[buf, pl.ds(r, 1), :], rsem.at[buf]) for r in range(MOE_BLOCK)]

    @pl.when(jnp.logical_and(i == 0, n_used > 0))
    def _():
        for cp in row_copies(tok_ref, 0):
            cp.start()

    @pl.when(i + 1 < n_used)
    def _():
        for cp in row_copies(tok_next_ref, 1 - slot):
            cp.start()

    def copies(e):
        return [pltpu.make_async_copy(w_hbm.at[e, :, pl.ds(half * D_FF, D_FF)], w_f32.at[half], sem.at[half])
                for half in range(2)]

    def cast():
        w_bf[...] = w_f32[...].astype(BF16)

    _stream_expert_weights(be_ref, nxt_ref, copies, cast)

    @pl.when(i < n_used)
    def _():
        for cp in row_copies(tok_ref, slot):
            cp.wait()
        x = rows_ref[slot].astype(BF16)
        g = jnp.dot(x, w_bf[0], preferred_element_type=F32) + b_ref[0, :, :D_FF]
        u = jnp.dot(x, w_bf[1], preferred_element_type=F32) + b_ref[0, :, D_FF:]
        g = jnp.minimum(g, SWIGLU_LIMIT)
        u = jnp.clip(u, -SWIGLU_LIMIT, SWIGLU_LIMIT)
        o_ref[...] = ((u + 1.0) * (g * _sigmoid(g * SWIGLU_ALPHA))).astype(o_ref.dtype)

    @pl.when(i >= n_used)
    def _():
        o_ref[...] = jnp.zeros_like(o_ref)


def moe_up(x, slot_tok, block_e, next_e, n_used, w_gate_up, b_gate_up):
    n_blocks = slot_tok.shape[0] // MOE_BLOCK
    tok3 = slot_tok.reshape(n_blocks, 1, MOE_BLOCK)
    grid_spec = pltpu.PrefetchScalarGridSpec(
        num_scalar_prefetch=3,
        grid=(n_blocks,),
        in_specs=[pl.BlockSpec((1, 1, MOE_BLOCK), lambda i, be, nx, nb: (i, 0, 0), memory_space=pltpu.SMEM),
                  pl.BlockSpec((1, 1, MOE_BLOCK), lambda i, be, nx, nb: (jnp.minimum(i + 1, n_blocks - 1), 0, 0),
                               memory_space=pltpu.SMEM),
                  pl.BlockSpec(memory_space=pl.ANY),
                  pl.BlockSpec(memory_space=pl.ANY),
                  pl.BlockSpec((1, 1, 2 * D_FF), lambda i, be, nx, nb: (be[i], 0, 0))],
        out_specs=pl.BlockSpec((MOE_BLOCK, D_FF), lambda i, be, nx, nb: (i, 0)),
        scratch_shapes=[pltpu.VMEM((2, D_MODEL, D_FF), F32), pltpu.VMEM((2, D_MODEL, D_FF), BF16),
                        pltpu.SemaphoreType.DMA((2,)),
                        pltpu.VMEM((2, MOE_BLOCK, D_MODEL), x.dtype), pltpu.SemaphoreType.DMA((2,))],
    )
    return pl.pallas_call(
        _moe_up_kernel,
        out_shape=jax.ShapeDtypeStruct((n_blocks * MOE_BLOCK, D_FF), BF16),
        grid_spec=grid_spec,
        compiler_params=_cparams("arbitrary"),
        name="moe_up",
    )(block_e, next_e, n_used, tok3, tok3, x, w_gate_up, b_gate_up.reshape(N_EXPERTS, 1, 2 * D_FF))


def _moe_down_kernel(be_ref, nxt_ref, nb_ref, h_ref, w_hbm, bd_ref, o_ref, w_f32, w_bf, sem):
    i = pl.program_id(0)

    def copies(e):
        return [pltpu.make_async_copy(w_hbm.at[e], w_f32, sem)]

    def cast():
        w_bf[...] = w_f32[...].astype(BF16)

    _stream_expert_weights(be_ref, nxt_ref, copies, cast)

    @pl.when(i < nb_ref[0])
    def _():
        o_ref[...] = jnp.dot(h_ref[...], w_bf[...], preferred_element_type=F32) + bd_ref[0]

    @pl.when(i >= nb_ref[0])
    def _():
        o_ref[...] = jnp.zeros_like(o_ref)


def moe_down(hb, block_e, next_e, n_used, w_down, b_down):
    n_blocks = hb.shape[0] // MOE_BLOCK
    grid_spec = pltpu.PrefetchScalarGridSpec(
        num_scalar_prefetch=3,
        grid=(n_blocks,),
        in_specs=[pl.BlockSpec((MOE_BLOCK, D_FF), lambda i, be, nx, nb: (i, 0)),
                  pl.BlockSpec(memory_space=pl.ANY),
                  pl.BlockSpec((1, 1, D_MODEL), lambda i, be, nx, nb: (be[i], 0, 0))],
        out_specs=pl.BlockSpec((MOE_BLOCK, D_MODEL), lambda i, be, nx, nb: (i, 0)),
        scratch_shapes=[pltpu.VMEM((D_FF, D_MODEL), F32), pltpu.VMEM((D_FF, D_MODEL), BF16),
                        pltpu.SemaphoreType.DMA(())],
    )
    return pl.pallas_call(
        _moe_down_kernel,
        out_shape=jax.ShapeDtypeStruct((n_blocks * MOE_BLOCK, D_MODEL), F32),
        grid_spec=grid_spec,
        compiler_params=_cparams("arbitrary"),
        name="moe_down",
    )(block_e, next_e, n_used, hb, w_down, b_down.reshape(N_EXPERTS, 1, D_MODEL))


def _combine_kernel(slot_ref, slot_next_ref, row0_ref, gate_ref, lnf_ref, yb_hbm, h_hbm, o_ref, rows_ref, h_vmem,
                    sem, hsem):
    i = pl.program_id(0)
    n_tiles = pl.num_programs(0)
    tm = o_ref.shape[-2]
    slot = lax.rem(i, 2)

    def copies(idx_ref, tile, buf):
        row0 = pl.multiple_of(row0_ref[tile], SUBLANES)
        cps = [pltpu.make_async_copy(h_hbm.at[pl.ds(row0, tm), :], h_vmem.at[buf], hsem.at[buf])]
        for n in range(TOP_K * tm):
            cps.append(pltpu.make_async_copy(yb_hbm.at[pl.ds(idx_ref[0, 0, n], 1), :],
                                             rows_ref.at[buf, n // tm, pl.ds(n % tm, 1), :], sem.at[buf]))
        return cps

    @pl.when(i == 0)
    def _():
        for cp in copies(slot_ref, 0, 0):
            cp.start()

    @pl.when(i + 1 < n_tiles)
    def _():
        for cp in copies(slot_next_ref, jnp.minimum(i + 1, n_tiles - 1), 1 - slot):
            cp.start()

    for cp in copies(slot_ref, i, slot):
        cp.wait()
    gate = gate_ref[...]
    acc = h_vmem[slot]
    for k in range(TOP_K):
        acc = acc + rows_ref[slot, k] * gate[:, k:k + 1]
    ms = jnp.mean(acc * acc, axis=-1, keepdims=True)
    o_ref[...] = (acc * lax.rsqrt(ms + RMS_EPS) * lnf_ref[...]).reshape(o_ref.shape)


def moe_combine(yb, h1, slot_of_pair, gates, row0, lnf_g, out_shape, out_index_map, tm):
    n_tiles = row0.shape[0]
    grid_spec = pltpu.PrefetchScalarGridSpec(
        num_scalar_prefetch=0,
        grid=(n_tiles,),
        in_specs=[pl.BlockSpec((1, 1, TOP_K * tm), lambda i: (i, 0, 0), memory_space=pltpu.SMEM),
                  pl.BlockSpec((1, 1, TOP_K * tm), lambda i: (jnp.minimum(i + 1, n_tiles - 1), 0, 0),
                               memory_space=pltpu.SMEM),
                  pl.BlockSpec(memory_space=pltpu.SMEM),
                  pl.BlockSpec((tm, LANES), lambda i: (i, 0)),
                  pl.BlockSpec((1, D_MODEL), lambda i: (0, 0)),
                  pl.BlockSpec(memory_space=pl.ANY),
                  pl.BlockSpec(memory_space=pl.ANY)],
        out_specs=pl.BlockSpec(out_shape[0], out_index_map),
        scratch_shapes=[pltpu.VMEM((2, TOP_K, tm, D_MODEL), F32), pltpu.VMEM((2, tm, D_MODEL), F32),
                        pltpu.SemaphoreType.DMA((2,)), pltpu.SemaphoreType.DMA((2,))],
    )
    return pl.pallas_call(
        _combine_kernel,
        out_shape=jax.ShapeDtypeStruct(out_shape[1], F32),
        grid_spec=grid_spec,
        compiler_params=_cparams("arbitrary"),
        name="moe_combine",
    )(slot_of_pair, slot_of_pair, row0, gates, lnf_g.reshape(1, D_MODEL), yb, h1)


PROMPT_PAD_T = 2176
SAMPLE_PAD_T = SUBLANES
ROW_TILE = 1104


def kernel(x_prompt, x_sample, state_ssm_conv, state_ssm, state_rwkv_shift, state_rwkv_wkv, meta_tokens, ln1_g, w_in, ssm_conv_w, ssm_conv_b, ssm_dt_bias, ssm_A_log, ssm_D, ssm_norm_g, rwkv_mu, rwkv_w0, rwkv_w2, rwkv_a0, rwkv_a2, rwkv_k_k, rwkv_k_a, rwkv_r_k, rwkv_gn_g, rwkv_gn_b, w_out, ln2_g, router_w, router_b, w_gate_up, b_gate_up, w_down, b_down, lnf_g):
    bp, seq, d = x_prompt.shape
    bs = x_sample.shape[0]
    t_prompt = N_META + seq
    tail = jnp.zeros((PROMPT_PAD_T - t_prompt, d), F32)
    pieces = []
    for b in range(bp):
        pieces += [meta_tokens, x_prompt[b], tail]
    h_rows = jnp.concatenate(pieces + [x_sample.reshape(bs, d)], axis=0)
    n_prow = bp * PROMPT_PAD_T

    l = 0
    xn = rmsnorm_rows(h_rows, ln1_g[l], BF16, ROW_TILE)
    w = w_in[l]
    o_xbc = SSM_D_INNER
    o_dt = o_xbc + SSM_CONV_DIM
    o_rw = o_dt + SSM_N_HEADS
    o_g = o_rw + RWKV_SHIFT_DIM
    w_z = w[:, :o_xbc].astype(BF16)
    w_xbc = w[:, o_xbc:o_dt].astype(BF16)
    w_dt = jnp.pad(w[:, o_dt:o_rw], ((0, 0), (0, LANES - SSM_N_HEADS))).astype(BF16)
    w_rw = w[:, o_rw:o_g].astype(BF16)
    w_g = w[:, o_g:].astype(BF16)
    z = matmul_bf16(xn, w_z, ROW_TILE, 1024, "proj_z")
    xbc = matmul_bf16(xn, w_xbc, ROW_TILE, 1024, "proj_xbc")
    dtr = matmul_bf16(xn, w_dt, ROW_TILE, LANES, "proj_dt")

    def sample_rows(a):
        n = a.shape[1]
        return jnp.pad(a[n_prow:].reshape(bs, 1, n), ((0, 0), (0, SAMPLE_PAD_T - 1), (0, 0))).reshape(-1, n)

    ssm_args = (ssm_conv_w[l], ssm_conv_b[l], ssm_dt_bias[l], ssm_A_log[l], ssm_D[l], ssm_norm_g[l])
    conv0_p = jnp.zeros((bp, SUBLANES, SSM_CONV_DIM), F32)
    s0_p = jnp.zeros((bp, SSM_N_HEADS, SSM_HEAD_DIM, SSM_D_STATE), F32)
    ya_p, ssm_p = ssd_branch(xbc, z, dtr, conv0_p, s0_p, *ssm_args, t=PROMPT_PAD_T, q=SSM_CHUNK,
                             t_valid=t_prompt, exact_state=False, name="ssd_prompt")
    conv0_s = jnp.pad(state_ssm_conv[l], ((0, 0), (SUBLANES - (SSM_CONV - 1), 0), (0, 0)))
    ya_s, ssm_s = ssd_branch(sample_rows(xbc), sample_rows(z), sample_rows(dtr), conv0_s, state_ssm[l], *ssm_args,
                             t=SAMPLE_PAD_T, q=SAMPLE_PAD_T, t_valid=1, exact_state=True, name="ssd_sample")
    ya_s = ya_s.reshape(bs, SAMPLE_PAD_T, d)[:, 0]
    rw = matmul_bf16(xn, w_rw, ROW_TILE, 896, "proj_rw")
    rwkv_args = (rwkv_mu[l], rwkv_w0[l], rwkv_w2[l], rwkv_a0[l], rwkv_a2[l], rwkv_k_k[l], rwkv_k_a[l],
                 rwkv_r_k[l], rwkv_gn_g[l], rwkv_gn_b[l])
    shift0_p = jnp.zeros((bp, SUBLANES, RWKV_SHIFT_DIM), F32)
    wkv0_p = jnp.zeros((bp, RWKV_N_HEADS, RWKV_HEAD, RWKV_HEAD), F32)
    yb_p, wkv_p = rwkv_branch(rw, shift0_p, wkv0_p, *rwkv_args, t=PROMPT_PAD_T, cq=RWKV_CHUNK, t_valid=t_prompt,
                              name="rwkv_prompt")
    yb_s, wkv_t = rwkv_step(rw, n_prow, state_rwkv_shift[l], jnp.transpose(state_rwkv_wkv[l], (1, 2, 3, 0)),
                            *rwkv_args, name="rwkv_sample")
    wkv_s = jnp.transpose(wkv_t, (3, 0, 1, 2))
    gates = matmul_bf16(xn, w_g, ROW_TILE, 1024, "proj_gates")
    h1, xn2, top_idx, top_gate = mix_and_route(
        gates, ya_p, yb_p, ya_s, yb_s, h_rows,
        w_out[l].astype(BF16), ln2_g[l], router_w[l], router_b[l], MOE_BLOCK)

    n_rows = h_rows.shape[0]
    row_id = jnp.arange(n_rows, dtype=jnp.int32)
    valid = jnp.logical_or(row_id >= n_prow, row_id % PROMPT_PAD_T < t_prompt)
    yb, slot_of_pair = moe_expert_rows(xn2, top_idx, valid, (bp * t_prompt + bs) * TOP_K, w_gate_up[l],
                                       b_gate_up[l], w_down[l], b_down[l])

    def combine(row0, out_block, out_full, out_map):
        rows = (row0[:, None] + jnp.arange(MOE_BLOCK, dtype=jnp.int32)[None, :])
        slots = jnp.transpose(slot_of_pair[rows], (0, 2, 1)).reshape(row0.shape[0], 1, TOP_K * MOE_BLOCK)
        return moe_combine(yb, h1, slots, top_gate[rows.reshape(-1)], row0, lnf_g, (out_block, out_full),
                           out_map, MOE_BLOCK)

    tiles_per_seq = seq // MOE_BLOCK
    tile_id = jnp.arange(bp * tiles_per_seq, dtype=jnp.int32)
    row0_p = (tile_id // tiles_per_seq) * PROMPT_PAD_T + N_META + (tile_id % tiles_per_seq) * MOE_BLOCK
    y_prompt = combine(row0_p, (1, MOE_BLOCK, d), (bp, seq, d),
                       lambda i: (i // tiles_per_seq, i % tiles_per_seq, 0))
    y_sample = combine(jnp.full((1,), n_prow, jnp.int32), (MOE_BLOCK, d), (bs, d), lambda i: (i, 0))

    last = [b * PROMPT_PAD_T + t_prompt - 1 for b in range(bp)]
    prompt_conv = jnp.stack([xbc[r - (SSM_CONV - 2):r + 1] for r in last])
    prompt_shift = jnp.stack([rw[r] for r in last])
    sample_conv = jnp.concatenate([state_ssm_conv[l][:, 1:], xbc[n_prow:, None]], axis=1)
    return (y_prompt, y_sample.reshape(bs, 1, d), prompt_conv[None], ssm_p[None], prompt_shift[None], wkv_p[None],
            sample_conv[None], ssm_s[None], rw[n_prow:][None], wkv_s[None])


def moe_expert_rows(xn2, top_idx, valid, n_valid_pairs, w_gate_up, b_gate_up, w_down, b_down):
    n_rows = xn2.shape[0]
    experts = jnp.arange(N_EXPERTS, dtype=jnp.int32)
    key = jnp.where(valid[:, None], top_idx[:, :TOP_K], N_EXPERTS).reshape(-1)
    n_pairs = key.shape[0]
    n_blocks = -(-n_valid_pairs // MOE_BLOCK) + N_EXPERTS
    n_slots = n_blocks * MOE_BLOCK
    counts = jnp.sum((key[:, None] == experts[None, :]).astype(jnp.int32), axis=0)
    padded = (counts + MOE_BLOCK - 1) // MOE_BLOCK * MOE_BLOCK
    ends = jnp.cumsum(padded)
    fill_e = jnp.repeat(experts, MOE_BLOCK)
    fill_r = jnp.tile(jnp.arange(MOE_BLOCK, dtype=jnp.int32), N_EXPERTS)
    fill_key = jnp.where(fill_r < jnp.repeat(padded - counts, MOE_BLOCK), fill_e, N_EXPERTS + 1)
    keys = jnp.concatenate([key, fill_key])
    iota = jnp.arange(keys.shape[0], dtype=jnp.int32)
    _, order = lax.sort((keys, iota), num_keys=1, is_stable=True)
    _, slot_of = lax.sort((order, iota), num_keys=1, is_stable=True)
    slot_tok = jnp.where(order[:n_slots] < n_pairs, order[:n_slots] // TOP_K, 0).astype(jnp.int32)
    slot_of_pair = jnp.minimum(slot_of[:n_pairs], n_slots - 1).reshape(n_rows, TOP_K)
    block_start = jnp.arange(n_blocks, dtype=jnp.int32) * MOE_BLOCK
    block_e = jnp.minimum(jnp.sum((ends[None, :] <= block_start[:, None]).astype(jnp.int32), axis=1),
                          N_EXPERTS - 1).astype(jnp.int32)
    n_used = (ends[-1:] // MOE_BLOCK).astype(jnp.int32)
    next_e = next_run_expert(block_e)

    hb = moe_up(xn2, slot_tok, block_e, next_e, n_used, w_gate_up, b_gate_up)
    yb = moe_down(hb, block_e, next_e, n_used, w_down, b_down)
    return yb, slot_of_pair
```

```python
import functools
import math

import jax
import jax.numpy as jnp
from jax import lax
from jax.experimental import pallas as pl
from jax.experimental.pallas import tpu as pltpu

F32 = jnp.float32
BF16 = jnp.bfloat16

V7X_VMEM_LIMIT_BYTES = 60000 * 1024
LANES = 128
SUBLANES = 8

D_MODEL = 2048
N_META = 16
RMS_EPS = 1e-6
SSM_D_INNER = 2048
SSM_HEAD_DIM = 64
SSM_N_HEADS = 32
SSM_N_GROUPS = 4
SSM_HPG = 8
SSM_D_STATE = 128
SSM_CONV = 4
SSM_CHUNK = 128
SSM_CONV_DIM = SSM_D_INNER + 2 * SSM_N_GROUPS * SSM_D_STATE
SSM_NORM_EPS = 1e-5
RWKV_DIM = 2048
RWKV_HEAD = 64
RWKV_N_HEADS = 32
RWKV_LORA = 64
RWKV_GN_EPS = 64e-5
RWKV_SHIFT_DIM = 3 * RWKV_DIM + 2 * RWKV_LORA
RWKV_CHUNK = 64
N_EXPERTS = 32
TOP_K = 4
D_FF = 2048
SWIGLU_LIMIT = 7.0
SWIGLU_ALPHA = 1.702
MOE_BLOCK = 128


def _cparams(*sem):
    return pltpu.CompilerParams(dimension_semantics=sem, vmem_limit_bytes=V7X_VMEM_LIMIT_BYTES)


def _split3(x):
    hi = x.astype(BF16)
    r1 = x - hi.astype(F32)
    mid = r1.astype(BF16)
    lo = (r1 - mid.astype(F32)).astype(BF16)
    return hi, mid, lo


def _dot(a, b, dims=None):
    a = a.astype(BF16)
    b = b.astype(BF16)
    if dims is None:
        return jnp.dot(a, b, preferred_element_type=F32)
    return lax.dot_general(a, b, (dims, ((), ())), preferred_element_type=F32)


def _dot_exact_lhs(x, m01):
    hi, mid, lo = _split3(x)
    m = m01.astype(BF16)
    return (jnp.dot(hi, m, preferred_element_type=F32) + jnp.dot(mid, m, preferred_element_type=F32)
            + jnp.dot(lo, m, preferred_element_type=F32))


def _dot_split_lhs(x, m01, terms):
    m = m01.astype(BF16)
    out = None
    for part in _split3(x)[:terms]:
        d = jnp.dot(part, m, preferred_element_type=F32)
        out = d if out is None else out + d
    return out


def _dot_f32(a, b, terms=6):
    ah, am, al = _split3(a)
    bh, bm, bl = _split3(b)
    lhs = jnp.concatenate([ah, ah, am, ah, al, am][:terms], axis=1)
    rhs = jnp.concatenate([bh, bm, bh, bl, bh, bm][:terms], axis=0)
    return jnp.dot(lhs, rhs, preferred_element_type=F32)


def _softplus(x):
    return jnp.maximum(x, 0.0) + jnp.log(1.0 + jnp.exp(-jnp.abs(x)))


def _sigmoid(x):
    return 1.0 / (1.0 + jnp.exp(-x))


def _silu(x):
    return x * _sigmoid(x)


def _rmsnorm_kernel(x_ref, g_ref, o_ref, *, eps):
    x = x_ref[...]
    ms = jnp.mean(x * x, axis=-1, keepdims=True)
    o_ref[...] = (x * lax.rsqrt(ms + eps) * g_ref[...]).astype(o_ref.dtype)


def rmsnorm_rows(x, g, out_dtype, tm):
    rows, d = x.shape
    return pl.pallas_call(
        functools.partial(_rmsnorm_kernel, eps=RMS_EPS),
        out_shape=jax.ShapeDtypeStruct((rows, d), out_dtype),
        grid=(rows // tm,),
        in_specs=[pl.BlockSpec((tm, d), lambda i: (i, 0)), pl.BlockSpec((1, d), lambda i: (0, 0))],
        out_specs=pl.BlockSpec((tm, d), lambda i: (i, 0)),
        compiler_params=_cparams("parallel"),
        name="rmsnorm_rows",
    )(x, g.reshape(1, d))


def _matmul_kernel(x_ref, w_ref, o_ref):
    o_ref[...] = jnp.dot(x_ref[...], w_ref[...], preferred_element_type=F32)


def matmul_bf16(x, w, tm, tn, name):
    rows, k = x.shape
    n = w.shape[1]
    return pl.pallas_call(
        _matmul_kernel,
        out_shape=jax.ShapeDtypeStruct((rows, n), F32),
        grid=(n // tn, rows // tm),
        in_specs=[pl.BlockSpec((tm, k), lambda j, i: (i, 0)), pl.BlockSpec((k, tn), lambda j, i: (0, j))],
        out_specs=pl.BlockSpec((tm, tn), lambda j, i: (i, j)),
        compiler_params=_cparams("parallel", "parallel"),
        name=name,
    )(x, w)


def _ssd_kernel(xbc_ref, z_ref, dtr_ref, conv0_ref, s0_ref, convw_ref, convb_ref, dtb_ref, alog_ref,
                dskip_ref, ng_ref, expand_ref, y_ref, s_ref, buf_ref, *, q, t_valid):
    c = pl.program_id(1)
    n_pad = buf_ref.shape[0] - q

    @pl.when(c == 0)
    def _():
        buf_ref[0:n_pad, :] = conv0_ref[0]
        s_ref[0] = s0_ref[0]

    buf_ref[n_pad:n_pad + q, :] = xbc_ref[...]
    acc = convb_ref[...] + convw_ref[SSM_CONV - 1:SSM_CONV, :] * buf_ref[n_pad:n_pad + q, :]
    for k in range(SSM_CONV - 1):
        off = n_pad - (SSM_CONV - 1) + k
        acc = acc + convw_ref[k:k + 1, :] * buf_ref[off:off + q, :]
    buf_ref[0:n_pad, :] = buf_ref[q:q + n_pad, :]
    xbc = _silu(acc)
    gn = SSM_N_GROUPS * SSM_D_STATE
    xs = xbc[:, :SSM_D_INNER]
    bm = xbc[:, SSM_D_INNER:SSM_D_INNER + gn]
    cm = xbc[:, SSM_D_INNER + gn:]

    row = lax.broadcasted_iota(jnp.int32, (q, LANES), 0) + c * q
    lane = lax.broadcasted_iota(jnp.int32, (q, LANES), 1)
    live = jnp.logical_and(row < t_valid, lane < SSM_N_HEADS)
    dt = jnp.where(live, _softplus(dtr_ref[...] + dtb_ref[...]), 0.0)
    da = dt * (-jnp.exp(alog_ref[...]))

    ti = lax.broadcasted_iota(jnp.int32, (q, q), 0)
    si = lax.broadcasted_iota(jnp.int32, (q, q), 1)
    causal = si <= ti
    tril = causal.astype(BF16)
    dh, dm, dl = _split3(da)
    cs = (jnp.dot(tril, dh, preferred_element_type=F32) + jnp.dot(tril, dm, preferred_element_type=F32)
          + jnp.dot(tril, dl, preferred_element_type=F32))
    eye = (lax.broadcasted_iota(jnp.int32, (LANES, LANES), 0)
           == lax.broadcasted_iota(jnp.int32, (LANES, LANES), 1)).astype(BF16)
    ch, cmid, cl = _split3(cs)
    nt = (((1,), (1,)), ((), ()))
    cs_t = (lax.dot_general(eye, ch, nt, preferred_element_type=F32)
            + lax.dot_general(eye, cmid, nt, preferred_element_type=F32)
            + lax.dot_general(eye, cl, nt, preferred_element_type=F32))

    expand = expand_ref[...]
    dt_x = _dot_exact_lhs(dt, expand)
    cs_x = _dot_exact_lhs(cs, expand)
    xdt = xs * dt_x
    ecs = jnp.exp(cs_x)
    dte = jnp.exp(cs_x[q - 1:q, :] - cs_x)

    lane_in_pair = lax.broadcasted_iota(jnp.int32, (q, LANES), 1)
    lo_half = lane_in_pair < SSM_HEAD_DIM

    y_groups = []
    for g in range(SSM_N_GROUPS):
        b_g = bm[:, g * SSM_D_STATE:(g + 1) * SSM_D_STATE]
        c_g = cm[:, g * SSM_D_STATE:(g + 1) * SSM_D_STATE]
        cb = _dot(c_g, b_g, ((1,), (1,)))
        gsl = slice(g * SSM_HPG * SSM_HEAD_DIM, (g + 1) * SSM_HPG * SSM_HEAD_DIM)
        s_g = s_ref[0, g * SSM_HPG:(g + 1) * SSM_HPG].reshape(SSM_HPG * SSM_HEAD_DIM, SSM_D_STATE)
        y_off = _dot(c_g, s_g, ((1,), (1,))) * ecs[:, gsl]
        slabs = []
        for m in range(SSM_HPG // 2):
            lsl = slice(gsl.start + m * LANES, gsl.start + (m + 1) * LANES)
            x_pair = xdt[:, lsl]
            acc_pair = None
            for half in range(2):
                h = g * SSM_HPG + 2 * m + half
                seg = cs[:, h:h + 1] - cs_t[h:h + 1, :]
                lmat = jnp.where(causal, jnp.exp(jnp.where(causal, seg, 0.0)), 0.0)
                x_half = jnp.where(lo_half if half == 0 else jnp.logical_not(lo_half), x_pair, 0.0)
                contrib = _dot(cb * lmat, x_half)
                acc_pair = contrib if acc_pair is None else acc_pair + contrib
            slabs.append(acc_pair)
        y_groups.append(jnp.concatenate(slabs, axis=1) + y_off)

        xw = xdt[:, gsl] * dte[:, gsl]
        upd = _dot(xw, b_g, ((0,), (0,)))
        for j in range(SSM_HPG):
            h = g * SSM_HPG + j
            dec = jnp.exp(cs_t[h:h + 1, q - 1:q])
            s_ref[0, h] = s_ref[0, h] * dec + upd[j * SSM_HEAD_DIM:(j + 1) * SSM_HEAD_DIM, :]

    y = jnp.concatenate(y_groups, axis=1) + dskip_ref[...] * xs
    y = y * _silu(z_ref[...])
    outs = []
    gw = SSM_D_INNER // SSM_N_GROUPS
    for g in range(SSM_N_GROUPS):
        yg = y[:, g * gw:(g + 1) * gw]
        outs.append(yg * lax.rsqrt(jnp.mean(yg * yg, axis=-1, keepdims=True) + SSM_NORM_EPS))
    y_ref[...] = jnp.concatenate(outs, axis=1) * ng_ref[...]


def ssd_branch(xbc, z, dtr, conv0, s0, conv_w, conv_b, dt_bias, a_log, d_skip, norm_g, *, t, q, t_valid, name):
    nb = s0.shape[0]
    cps = t // q
    n_pad = conv0.shape[1]
    pad_l = lambda v: jnp.pad(v.astype(F32), (0, LANES - v.shape[0])).reshape(1, LANES)
    expand = (jnp.arange(LANES)[:, None] == (jnp.arange(SSM_D_INNER) // SSM_HEAD_DIM)[None, :]).astype(BF16)
    row2 = lambda v: v.reshape(1, -1)
    full = lambda a: pl.BlockSpec(a.shape, lambda b, c: (0,) * a.ndim)
    consts = [conv_w, row2(conv_b), pad_l(dt_bias), pad_l(a_log), row2(jnp.repeat(d_skip, SSM_HEAD_DIM)),
              row2(norm_g), expand]
    kern = functools.partial(_ssd_kernel, q=q, t_valid=t_valid)
    return pl.pallas_call(
        kern,
        out_shape=(jax.ShapeDtypeStruct((nb * t, SSM_D_INNER), F32),
                   jax.ShapeDtypeStruct(s0.shape, F32)),
        grid=(nb, cps),
        in_specs=[pl.BlockSpec((q, SSM_CONV_DIM), lambda b, c: (b * cps + c, 0)),
                  pl.BlockSpec((q, SSM_D_INNER), lambda b, c: (b * cps + c, 0)),
                  pl.BlockSpec((q, LANES), lambda b, c: (b * cps + c, 0)),
                  pl.BlockSpec((1, n_pad, SSM_CONV_DIM), lambda b, c: (b, 0, 0)),
                  pl.BlockSpec((1,) + s0.shape[1:], lambda b, c: (b, 0, 0, 0))] + [full(a) for a in consts],
        out_specs=(pl.BlockSpec((q, SSM_D_INNER), lambda b, c: (b * cps + c, 0)),
                   pl.BlockSpec((1,) + s0.shape[1:], lambda b, c: (b, 0, 0, 0))),
        scratch_shapes=[pltpu.VMEM((q + n_pad, SSM_CONV_DIM), F32)],
        compiler_params=_cparams("parallel", "arbitrary"),
        name=name,
    )(xbc, z, dtr, conv0, s0, *consts)


def _ssd_step_kernel(xbc_ref, z_ref, dtr_ref, conv_ref, s0_ref, convw_ref, convb_ref, dtb_ref, alog_ref, dskip_ref,
                     ng_ref, expand_ref, y_ref, s_ref):
    bt = xbc_ref.shape[0]
    acc = convb_ref[...] + convw_ref[SSM_CONV - 1:SSM_CONV, :] * xbc_ref[...]
    for k in range(SSM_CONV - 1):
        acc = acc + convw_ref[k:k + 1, :] * conv_ref[k]
    xbc = _silu(acc)
    gn = SSM_N_GROUPS * SSM_D_STATE
    xs = xbc[:, :SSM_D_INNER]
    bm = xbc[:, SSM_D_INNER:SSM_D_INNER + gn]
    cm = xbc[:, SSM_D_INNER + gn:]
    lane = lax.broadcasted_iota(jnp.int32, (bt, LANES), 1)
    dt = jnp.where(lane < SSM_N_HEADS, _softplus(dtr_ref[...] + dtb_ref[...]), 0.0)
    dec = jnp.exp(dt * (-jnp.exp(alog_ref[...])))
    xdt = xs * _dot_exact_lhs(dt, expand_ref[...])
    hd = SSM_HPG * SSM_HEAD_DIM
    rowid = lax.broadcasted_iota(jnp.int32, (bt, hd), 0)
    tn = ((0,), (0,))
    y_acc = [jnp.zeros((bt, hd), F32) for _ in range(SSM_N_GROUPS)]
    for bb in range(bt):
        mine = rowid == bb
        for g in range(SSM_N_GROUPS):
            b_g = bm[:, g * SSM_D_STATE:(g + 1) * SSM_D_STATE]
            c_g = cm[:, g * SSM_D_STATE:(g + 1) * SSM_D_STATE]
            xh, xm, xl = _split3(jnp.where(mine, xdt[:, g * hd:(g + 1) * hd], 0.0))
            bh, bmid, bl = _split3(b_g)
            upd = _dot(jnp.concatenate([xh, xh, xm, xh, xl, xm], axis=0),
                       jnp.concatenate([bh, bmid, bh, bl, bh, bmid], axis=0), tn)
            heads = []
            for j in range(SSM_HPG):
                h = g * SSM_HPG + j
                s_new = s0_ref[bb, h] * dec[bb:bb + 1, h:h + 1] + upd[j * SSM_HEAD_DIM:(j + 1) * SSM_HEAD_DIM, :]
                s_ref[bb, h] = s_new
                heads.append(s_new)
            y_all = _dot(c_g, jnp.concatenate(heads, axis=0), ((1,), (1,)))
            y_acc[g] = jnp.where(mine, y_all, y_acc[g])
    y = jnp.concatenate(y_acc, axis=1) + dskip_ref[...] * xs
    y = y * _silu(z_ref[...])
    outs = []
    gw = SSM_D_INNER // SSM_N_GROUPS
    for g in range(SSM_N_GROUPS):
        yg = y[:, g * gw:(g + 1) * gw]
        outs.append(yg * lax.rsqrt(jnp.mean(yg * yg, axis=-1, keepdims=True) + SSM_NORM_EPS))
    y_ref[...] = jnp.concatenate(outs, axis=1) * ng_ref[...]


def ssd_step(xbc, z, dtr, row0, conv_prev, s0, conv_w, conv_b, dt_bias, a_log, d_skip, norm_g, *, bt, name):
    nb = s0.shape[0]
    pad_l = lambda v: jnp.pad(v.astype(F32), (0, LANES - v.shape[0])).reshape(1, LANES)
    expand = (jnp.arange(LANES)[:, None] == (jnp.arange(SSM_D_INNER) // SSM_HEAD_DIM)[None, :]).astype(BF16)
    row2 = lambda v: v.reshape(1, -1)
    full = lambda a: pl.BlockSpec(a.shape, lambda i: (0,) * a.ndim)
    consts = [conv_w, row2(conv_b), pad_l(dt_bias), pad_l(a_log), row2(jnp.repeat(d_skip, SSM_HEAD_DIM)),
              row2(norm_g), expand]
    rows = lambda n: pl.BlockSpec((bt, n), lambda i: (row0 // bt + i, 0))
    state_spec = pl.BlockSpec((bt,) + s0.shape[1:], lambda i: (i, 0, 0, 0))
    return pl.pallas_call(
        _ssd_step_kernel,
        out_shape=(jax.ShapeDtypeStruct((nb, SSM_D_INNER), F32), jax.ShapeDtypeStruct(s0.shape, F32)),
        grid=(nb // bt,),
        in_specs=[rows(SSM_CONV_DIM), rows(SSM_D_INNER), rows(LANES),
                  pl.BlockSpec((SSM_CONV - 1, bt, SSM_CONV_DIM), lambda i: (0, i, 0)),
                  state_spec] + [full(a) for a in consts],
        out_specs=(pl.BlockSpec((bt, SSM_D_INNER), lambda i: (i, 0)), state_spec),
        compiler_params=_cparams("parallel"),
        name=name,
    )(xbc, z, dtr, conv_prev, s0, *consts)


RWKV_GROUP_HEADS = 4
RWKV_GROUP_LANES = RWKV_GROUP_HEADS * RWKV_HEAD


def _rep_rows(x, n):
    return jnp.concatenate([x] * n, axis=0)


def _head_mask(rows_per_head, cols_per_head, n_rows, n_cols):
    rr = lax.broadcasted_iota(jnp.int32, (n_rows, n_cols), 0) // rows_per_head
    cc = lax.broadcasted_iota(jnp.int32, (n_rows, n_cols), 1) // cols_per_head
    return rr == cc


def _rwkv_prep(cur, prev, live, mu_ref, w0_ref, w2_ref, a0_ref, a2_ref, kk_ref, ka_ref, rk_ref, hred_ref,
               hexp_ref):
    rows = cur.shape[0]
    mixed = cur + (prev - cur) * mu_ref[...]
    r = mixed[:, 0:RWKV_DIM]
    k = mixed[:, RWKV_DIM:2 * RWKV_DIM]
    v = mixed[:, 2 * RWKV_DIM:3 * RWKV_DIM]
    lora = mixed[:, 3 * RWKV_DIM:]
    lane = lax.broadcasted_iota(jnp.int32, (rows, LANES), 1)
    wl = jnp.where(lane < RWKV_LORA, jnp.tanh(lora), 0.0)
    al = jnp.where(lane >= RWKV_LORA, lora, 0.0)
    logw = -math.exp(-0.5) * _sigmoid(w0_ref[...] + _dot_f32(wl, w2_ref[...], terms=3))
    a = _sigmoid(a0_ref[...] + _dot_f32(al, a2_ref[...], terms=3))
    kk_raw = k * kk_ref[...]
    k2 = k * (1.0 + (a - 1.0) * ka_ref[...])
    sums = _dot_split_lhs(jnp.concatenate([kk_raw * kk_raw, r * k2 * rk_ref[...]], axis=0), hred_ref[...], 2)
    inv_n = 1.0 / jnp.maximum(jnp.sqrt(sums[:rows]), 1e-12)
    kk = kk_raw * _dot_split_lhs(inv_n, hexp_ref[...], 2)
    if live is not None:
        logw = jnp.where(live, logw, 0.0)
        kk = jnp.where(live, kk, 0.0)
        k2 = jnp.where(live, k2, 0.0)
        v = jnp.where(live, v, 0.0)
    return r, k2, v, kk, a, logw, sums[rows:]


def _rwkv_post(y, v, bonus_sum, gng_ref, gnb_ref, hred_ref, hexp_ref):
    rows = y.shape[0]
    inv_h = 1.0 / RWKV_HEAD
    sums = _dot_split_lhs(jnp.concatenate([y, y * y], axis=0), hred_ref[...], 3) * inv_h
    mean = sums[:rows]
    var = jnp.maximum(sums[rows:] - mean * mean, 0.0)
    per_head = jnp.concatenate([mean, lax.rsqrt(var + RWKV_GN_EPS), bonus_sum], axis=0)
    wide = _dot_split_lhs(per_head, hexp_ref[...], 2)
    return ((y - wide[:rows]) * wide[rows:2 * rows] * gng_ref[...] + gnb_ref[...]
            + wide[2 * rows:] * v)


def _state_to_blockdiag(stack, m_state):
    return jnp.where(m_state, jnp.concatenate([stack] * RWKV_GROUP_HEADS, axis=1), 0.0)


def _blockdiag_to_state(s_bd):
    stack = s_bd[:, 0:RWKV_HEAD]
    for hh in range(1, RWKV_GROUP_HEADS):
        stack = stack + s_bd[:, hh * RWKV_HEAD:(hh + 1) * RWKV_HEAD]
    return stack


def _rwkv_kernel(rw_ref, shift0_ref, wkv0_ref, mu_ref, w0_ref, w2_ref, a0_ref, a2_ref, kk_ref, ka_ref, rk_ref,
                 gng_ref, gnb_ref, hred_ref, hexp_ref,
                 y_ref, wkv_ref, buf_ref, s_ref, *, cq, t_valid):
    c = pl.program_id(1)
    n_chunks = pl.num_programs(1)
    n_pad = buf_ref.shape[0] - cq
    gl = RWKV_GROUP_LANES
    gh = RWKV_GROUP_HEADS
    n_groups = RWKV_DIM // gl
    nt = ((1,), (1,))
    tn = ((0,), (0,))

    m_state = _head_mask(RWKV_HEAD, RWKV_HEAD, gl, gl)
    m_exp = _head_mask(cq, RWKV_HEAD, gh * cq, gl)
    m_nbd = _head_mask(cq, cq, gh * cq, gh * cq)

    @pl.when(c == 0)
    def _():
        buf_ref[0:n_pad, :] = shift0_ref[0]
        for g in range(n_groups):
            s_ref[g] = _state_to_blockdiag(wkv0_ref[0, g * gl:(g + 1) * gl, :], m_state)

    buf_ref[n_pad:n_pad + cq, :] = rw_ref[...]
    cur = buf_ref[n_pad:n_pad + cq, :]
    prev = buf_ref[n_pad - 1:n_pad - 1 + cq, :]
    buf_ref[0:n_pad, :] = buf_ref[cq:cq + n_pad, :]
    row = lax.broadcasted_iota(jnp.int32, (cq, 1), 0) + c * cq
    r, k2, v, kk, a, logw, bonus_sum = _rwkv_prep(cur, prev, row < t_valid, mu_ref, w0_ref, w2_ref, a0_ref, a2_ref,
                                                  kk_ref, ka_ref, rk_ref, hred_ref, hexp_ref)

    ti = lax.broadcasted_iota(jnp.int32, (cq, cq), 0)
    si = lax.broadcasted_iota(jnp.int32, (cq, cq), 1)
    tril = (si <= ti).astype(BF16)
    lh, lm, ll = _split3(logw)
    cl = (jnp.dot(tril, lh, preferred_element_type=F32) + jnp.dot(tril, lm, preferred_element_type=F32)
          + jnp.dot(tril, ll, preferred_element_type=F32))
    p_incl = jnp.exp(cl)
    p_inv = jnp.exp(-cl)
    a_t = jnp.exp(cl - logw) * (-kk)
    r_t = p_incl * r
    b_t = p_inv * (kk * a)
    k_t = p_inv * k2
    p_end = p_incl[cq - 1:cq, :]

    tt = lax.broadcasted_iota(jnp.int32, (cq, gh * cq), 0)
    ss_i = lax.broadcasted_iota(jnp.int32, (cq, gh * cq), 1) % cq
    strict = ss_i < tt
    incl = ss_i <= tt

    def expand_rows(x):
        return jnp.where(m_exp, _rep_rows(x, gh), 0.0)

    groups = range(n_groups)
    sls = [slice(g * gl, (g + 1) * gl) for g in groups]
    lhs_ar = [jnp.concatenate([a_t[:, sl], r_t[:, sl]], axis=0) for sl in sls]
    v_exp = [expand_rows(v[:, sl]) for sl in sls]
    s_bd = [s_ref[g] for g in groups]
    hs = gh * cq
    abk = [_dot(lhs_ar[g], jnp.concatenate([expand_rows(b_t[:, sls[g]]), expand_rows(k_t[:, sls[g]])], axis=0), nt)
           for g in groups]
    ab = [m[:, :hs] for m in abk]
    ak = [m[:, hs:] for m in abk]
    w_s = [_dot(lhs_ar[g], s_bd[g], nt) for g in groups]
    x = [w_s[g][:cq] + _dot(jnp.where(strict, ak[g][:cq], 0.0), v_exp[g]) for g in groups]
    n_cat = [jnp.where(strict, ab[g][:cq], 0.0) for g in groups]
    n_steps = int(math.log2(cq))
    for step in range(n_steps):
        x = [x[g] + _dot(n_cat[g], expand_rows(x[g])) for g in groups]
        if step + 1 < n_steps:
            n_cat = [_dot(n_cat[g], jnp.where(m_nbd, _rep_rows(n_cat[g], gh), 0.0)) for g in groups]
    incl2 = jnp.concatenate([incl, incl], axis=1)
    y_groups = [w_s[g][cq:] + _dot(jnp.where(incl2, abk[g][cq:], 0.0),
                                   jnp.concatenate([expand_rows(x[g]), v_exp[g]], axis=0)) for g in groups]
    for g in groups:
        uv = jnp.concatenate([x[g], v[:, sls[g]]], axis=0)
        bk = jnp.concatenate([b_t[:, sls[g]], k_t[:, sls[g]]], axis=0)
        inc = _dot(uv, bk, tn)
        s_ref[g] = jnp.where(m_state, (s_bd[g] + inc) * p_end[:, sls[g]], 0.0)

    y = jnp.concatenate(y_groups, axis=1)
    y_ref[...] = _rwkv_post(y, v, bonus_sum, gng_ref, gnb_ref, hred_ref, hexp_ref)

    @pl.when(c == n_chunks - 1)
    def _():
        for g in range(n_groups):
            wkv_ref[0, g * gl:(g + 1) * gl, :] = _blockdiag_to_state(s_ref[g])


def _rwkv_consts(mu, w0, w2, a0, a2, k_k, k_a, r_k, gn_g, gn_b):
    row2 = lambda a: a.reshape(1, -1).astype(F32)
    zeros_l = jnp.zeros((RWKV_LORA, RWKV_DIM), F32)
    head_of_lane = jnp.arange(RWKV_DIM) // RWKV_HEAD
    hexp = (jnp.arange(LANES)[:, None] == head_of_lane[None, :]).astype(BF16)
    return [row2(mu), row2(w0), jnp.concatenate([w2, zeros_l], axis=0), row2(a0),
            jnp.concatenate([zeros_l, a2], axis=0), row2(k_k), row2(k_a), row2(r_k), row2(gn_g), row2(gn_b),
            hexp.T, hexp]


def _rwkv_step_prep_kernel(rw_ref, prev_ref, mu_ref, w0_ref, w2_ref, a0_ref, a2_ref, kk_ref, ka_ref, rk_ref,
                           gng_ref, gnb_ref, hred_ref, hexp_ref, vecs_ref, v_ref, bsum_ref):
    r, k2, v, kk, a, logw, bonus_sum = _rwkv_prep(rw_ref[...], prev_ref[...], None, mu_ref, w0_ref, w2_ref, a0_ref,
                                                  a2_ref, kk_ref, ka_ref, rk_ref, hred_ref, hexp_ref)
    for n, vec in enumerate((-kk, kk * a, jnp.exp(logw), k2, r, v)):
        vecs_ref[n] = vec.T
    v_ref[...] = v
    bsum_ref[...] = bonus_sum


def _rwkv_step_state_kernel(vecs_ref, s_ref, s_out_ref, yt_ref):
    a_t, b_t, w_t, k_t, r_t = (vecs_ref[n] for n in range(5))

    def body(i, carry):
        s_i = s_ref[0, i]
        u = jnp.sum(s_i * a_t, axis=0, keepdims=True)
        s_new = s_i * w_t + u * b_t + vecs_ref[5, pl.ds(i, 1), :] * k_t
        s_out_ref[0, i] = s_new
        yt_ref[pl.ds(i, 1), :] = jnp.sum(s_new * r_t, axis=0, keepdims=True)
        return carry

    lax.fori_loop(0, RWKV_HEAD, body, 0, unroll=4)


def _rwkv_step_post_kernel(yt_ref, v_ref, bsum_ref, gng_ref, gnb_ref, hred_ref, hexp_ref, y_ref):
    y_ref[...] = _rwkv_post(yt_ref[...].T, v_ref[...], bsum_ref[...], gng_ref, gnb_ref, hred_ref, hexp_ref)


def rwkv_step(rw, row0, prev, wkv_t, *params, name):
    nb = prev.shape[0]
    consts = _rwkv_consts(*params)
    full = lambda a: pl.BlockSpec(a.shape, lambda i: (0,) * a.ndim)
    n_vec = 6
    vecs, v_rows, bsum = pl.pallas_call(
        _rwkv_step_prep_kernel,
        out_shape=(jax.ShapeDtypeStruct((n_vec, RWKV_DIM, nb), F32), jax.ShapeDtypeStruct((nb, RWKV_DIM), F32),
                   jax.ShapeDtypeStruct((nb, LANES), F32)),
        grid=(1,),
        in_specs=[pl.BlockSpec((nb, RWKV_SHIFT_DIM), lambda i: (row0 // nb, 0)),
                  pl.BlockSpec((nb, RWKV_SHIFT_DIM), lambda i: (0, 0))] + [full(a) for a in consts],
        out_specs=(pl.BlockSpec((n_vec, RWKV_DIM, nb), lambda i: (0, 0, 0)),
                   pl.BlockSpec((nb, RWKV_DIM), lambda i: (0, 0)), pl.BlockSpec((nb, LANES), lambda i: (0, 0))),
        compiler_params=_cparams("arbitrary"),
        name=name + "_prep",
    )(rw, prev, *consts)
    state_spec = pl.BlockSpec((1, RWKV_HEAD, RWKV_HEAD, nb), lambda h: (h, 0, 0, 0))
    wkv_new, y_t = pl.pallas_call(
        _rwkv_step_state_kernel,
        out_shape=(jax.ShapeDtypeStruct(wkv_t.shape, F32), jax.ShapeDtypeStruct((RWKV_DIM, nb), F32)),
        grid=(RWKV_N_HEADS,),
        in_specs=[pl.BlockSpec((n_vec, RWKV_HEAD, nb), lambda h: (0, h, 0)), state_spec],
        out_specs=(state_spec, pl.BlockSpec((RWKV_HEAD, nb), lambda h: (h, 0))),
        compiler_params=_cparams("parallel"),
        name=name + "_state",
    )(vecs, wkv_t)
    post_consts = consts[8:]
    y = pl.pallas_call(
        _rwkv_step_post_kernel,
        out_shape=jax.ShapeDtypeStruct((nb, RWKV_DIM), F32),
        grid=(1,),
        in_specs=[pl.BlockSpec((RWKV_DIM, nb), lambda i: (0, 0)), pl.BlockSpec((nb, RWKV_DIM), lambda i: (0, 0)),
                  pl.BlockSpec((nb, LANES), lambda i: (0, 0))] + [full(a) for a in post_consts],
        out_specs=pl.BlockSpec((nb, RWKV_DIM), lambda i: (0, 0)),
        compiler_params=_cparams("arbitrary"),
        name=name + "_post",
    )(y_t, v_rows, bsum, *post_consts)
    return y, wkv_new


def rwkv_branch(rw, shift0, wkv0, *params, t, cq, t_valid, name):
    nb = wkv0.shape[0]
    cps = t // cq
    n_pad = shift0.shape[1]
    consts = _rwkv_consts(*params)
    full = lambda a: pl.BlockSpec(a.shape, lambda b, c: (0,) * a.ndim)
    wkv_rows = wkv0.reshape(nb, RWKV_DIM, RWKV_HEAD)
    kern = functools.partial(_rwkv_kernel, cq=cq, t_valid=t_valid)
    y, wkv = pl.pallas_call(
        kern,
        out_shape=(jax.ShapeDtypeStruct((nb * t, RWKV_DIM), F32),
                   jax.ShapeDtypeStruct(wkv_rows.shape, F32)),
        grid=(nb, cps),
        in_specs=[pl.BlockSpec((cq, RWKV_SHIFT_DIM), lambda b, c: (b * cps + c, 0)),
                  pl.BlockSpec((1, n_pad, RWKV_SHIFT_DIM), lambda b, c: (b, 0, 0)),
                  pl.BlockSpec((1, RWKV_DIM, RWKV_HEAD), lambda b, c: (b, 0, 0))] + [full(a) for a in consts],
        out_specs=(pl.BlockSpec((cq, RWKV_DIM), lambda b, c: (b * cps + c, 0)),
                   pl.BlockSpec((1, RWKV_DIM, RWKV_HEAD), lambda b, c: (b, 0, 0))),
        scratch_shapes=[pltpu.VMEM((cq + n_pad, RWKV_SHIFT_DIM), F32),
                        pltpu.VMEM((RWKV_DIM // RWKV_GROUP_LANES, RWKV_GROUP_LANES, RWKV_GROUP_LANES), F32)],
        compiler_params=_cparams("parallel", "arbitrary"),
        name=name,
    )(rw, shift0, wkv_rows, *consts)
    return y, wkv.reshape(wkv0.shape)


NEG_BIG = -1e30


def _mix_kernel(g_ref, yap_ref, ybp_ref, yas_ref, ybs_ref, h_ref, wout_ref, ln2_ref, rw_ref, rb_ref,
                h1_ref, xn_ref, idx_ref, gate_ref):
    i = pl.program_id(0)
    last = i == pl.num_programs(0) - 1
    ya = jnp.where(last, yas_ref[...], yap_ref[...])
    yb = jnp.where(last, ybs_ref[...], ybp_ref[...])
    g = g_ref[...]
    merged = _sigmoid(g[:, :D_MODEL]) * ya + _sigmoid(g[:, D_MODEL:]) * yb
    h1 = h_ref[...] + jnp.dot(merged.astype(BF16), wout_ref[...], preferred_element_type=F32)
    h1_ref[...] = h1
    ms = jnp.mean(h1 * h1, axis=-1, keepdims=True)
    xn = h1 * lax.rsqrt(ms + RMS_EPS) * ln2_ref[...]
    xn_ref[...] = xn
    xh, xm, xl = _split3(xn)
    logits = jnp.dot(jnp.concatenate([xh, xh, xm, xh, xl, xm], axis=1), rw_ref[...],
                     preferred_element_type=F32) + rb_ref[...]
    lane = lax.broadcasted_iota(jnp.int32, logits.shape, 1).astype(F32)
    idx_out = jnp.zeros(logits.shape, F32)
    val_out = jnp.zeros(logits.shape, F32)
    top0 = None
    for kth in range(TOP_K):
        m = jnp.max(logits, axis=-1, keepdims=True)
        sel = jnp.min(jnp.where(logits == m, lane, float(LANES)), axis=-1, keepdims=True)
        if kth == 0:
            top0 = m
        idx_out = jnp.where(lane == kth, sel, idx_out)
        val_out = jnp.where(lane == kth, jnp.exp(m - top0), val_out)
        logits = jnp.where(lane == sel, -jnp.inf, logits)
    idx_ref[...] = idx_out.astype(jnp.int32)
    gate_ref[...] = val_out / jnp.sum(val_out, axis=-1, keepdims=True)


def mix_and_route(gates, ya_p, yb_p, ya_s, yb_s, h_rows, w_out_bf, ln2_g, router_w, router_b, tm):
    rows = h_rows.shape[0]
    n_tiles = rows // tm
    last_p = ya_p.shape[0] // tm - 1
    wh, wm, wl = _split3(jnp.pad(router_w, ((0, 0), (0, LANES - N_EXPERTS))))
    rw_pad = jnp.concatenate([wh, wm, wh, wl, wh, wm], axis=0)
    rb_pad = jnp.pad(router_b, (0, LANES - N_EXPERTS), constant_values=NEG_BIG).reshape(1, LANES)
    row_spec = lambda n: pl.BlockSpec((tm, n), lambda i: (i, 0))
    prompt_spec = pl.BlockSpec((tm, D_MODEL), lambda i: (jnp.minimum(i, last_p), 0))
    fixed = lambda a: pl.BlockSpec(a.shape, lambda i: (0,) * a.ndim)
    return pl.pallas_call(
        _mix_kernel,
        out_shape=(jax.ShapeDtypeStruct((rows, D_MODEL), F32), jax.ShapeDtypeStruct((rows, D_MODEL), F32),
                   jax.ShapeDtypeStruct((rows, LANES), jnp.int32), jax.ShapeDtypeStruct((rows, LANES), F32)),
        grid=(n_tiles,),
        in_specs=[row_spec(2 * D_MODEL), prompt_spec, prompt_spec, fixed(ya_s), fixed(yb_s), row_spec(D_MODEL),
                  fixed(w_out_bf), pl.BlockSpec((1, D_MODEL), lambda i: (0, 0)), fixed(rw_pad), fixed(rb_pad)],
        out_specs=(row_spec(D_MODEL), row_spec(D_MODEL), row_spec(LANES), row_spec(LANES)),
        compiler_params=_cparams("parallel"),
        name="mix_and_route",
    )(gates, ya_p, yb_p, ya_s, yb_s, h_rows, w_out_bf, ln2_g.reshape(1, D_MODEL), rw_pad, rb_pad)


def next_run_expert(block_e):
    n = block_e.shape[0]
    idx = jnp.arange(n, dtype=jnp.int32)
    change = jnp.concatenate([jnp.ones((1,), bool), block_e[1:] != block_e[:-1]])
    later_change = jnp.concatenate([jnp.where(change, idx, n)[1:], jnp.full((1,), n, jnp.int32)])
    next_pos = lax.cummin(later_change, axis=0, reverse=True)
    onehot = next_pos[:, None] == idx[None, :]
    return jnp.where(next_pos < n, jnp.sum(jnp.where(onehot, block_e[None, :], 0), axis=1), -1).astype(jnp.int32)


def _stream_expert_weights(be_ref, nxt_ref, copies, cast):
    i = pl.program_id(0)
    first = jnp.logical_or(i == 0, be_ref[i] != be_ref[jnp.maximum(i, 1) - 1])

    @pl.when(i == 0)
    def _():
        for cp in copies(be_ref[0]):
            cp.start()

    @pl.when(first)
    def _():
        for cp in copies(be_ref[i]):
            cp.wait()
        cast()

    @pl.when(jnp.logical_and(first, nxt_ref[i] >= 0))
    def _():
        for cp in copies(nxt_ref[i]):
            cp.start()


def _moe_up_kernel(be_ref, nxt_ref, nb_ref, tok_ref, tok_next_ref, x_hbm, w_hbm, b_ref, o_ref, w_f32, w_bf, sem,
                   rows_ref, rsem):
    i = pl.program_id(0)
    n_used = nb_ref[0]
    slot = lax.rem(i, 2)

    def row_copies(idx_ref, buf):
        return [pltpu.make_async_copy(x_hbm.at[pl.ds(idx_ref[0, 0, r], 1), :],
                                      rows_ref.at[buf, pl.ds(r, 1), :], rsem.at[buf]) for r in range(MOE_BLOCK)]

    @pl.when(jnp.logical_and(i == 0, n_used > 0))
    def _():
        for cp in row_copies(tok_ref, 0):
            cp.start()

    @pl.when(i + 1 < n_used)
    def _():
        for cp in row_copies(tok_next_ref, 1 - slot):
            cp.start()

    def copies(e):
        return [pltpu.make_async_copy(w_hbm.at[e, :, pl.ds(half * D_FF, D_FF)], w_f32.at[half], sem.at[half])
                for half in range(2)]

    def cast():
        w_bf[...] = w_f32[...].astype(BF16)

    _stream_expert_weights(be_ref, nxt_ref, copies, cast)

    @pl.when(i < n_used)
    def _():
        for cp in row_copies(tok_ref, slot):
            cp.wait()
        x = rows_ref[slot].astype(BF16)
        g = jnp.dot(x, w_bf[0], preferred_element_type=F32) + b_ref[0, :, :D_FF]
        u = jnp.dot(x, w_bf[1], preferred_element_type=F32) + b_ref[0, :, D_FF:]
        g = jnp.minimum(g, SWIGLU_LIMIT)
        u = jnp.clip(u, -SWIGLU_LIMIT, SWIGLU_LIMIT)
        o_ref[...] = ((u + 1.0) * (g * _sigmoid(g * SWIGLU_ALPHA))).astype(o_ref.dtype)

    @pl.when(i >= n_used)
    def _():
        o_ref[...] = jnp.zeros_like(o_ref)


def moe_up(x, slot_tok, block_e, next_e, n_used, w_gate_up, b_gate_up):
    n_blocks = slot_tok.shape[0] // MOE_BLOCK
    tok3 = slot_tok.reshape(n_blocks, 1, MOE_BLOCK)
    grid_spec = pltpu.PrefetchScalarGridSpec(
        num_scalar_prefetch=3,
        grid=(n_blocks,),
        in_specs=[pl.BlockSpec((1, 1, MOE_BLOCK), lambda i, be, nx, nb: (i, 0, 0), memory_space=pltpu.SMEM),
                  pl.BlockSpec((1, 1, MOE_BLOCK), lambda i, be, nx, nb: (jnp.minimum(i + 1, n_blocks - 1), 0, 0),
                               memory_space=pltpu.SMEM),
                  pl.BlockSpec(memory_space=pl.ANY),
                  pl.BlockSpec(memory_space=pl.ANY),
                  pl.BlockSpec((1, 1, 2 * D_FF), lambda i, be, nx, nb: (be[i], 0, 0))],
        out_specs=pl.BlockSpec((MOE_BLOCK, D_FF), lambda i, be, nx, nb: (i, 0)),
        scratch_shapes=[pltpu.VMEM((2, D_MODEL, D_FF), F32), pltpu.VMEM((2, D_MODEL, D_FF), BF16),
                        pltpu.SemaphoreType.DMA((2,)),
                        pltpu.VMEM((2, MOE_BLOCK, D_MODEL), x.dtype), pltpu.SemaphoreType.DMA((2,))],
    )
    return pl.pallas_call(
        _moe_up_kernel,
        out_shape=jax.ShapeDtypeStruct((n_blocks * MOE_BLOCK, D_FF), BF16),
        grid_spec=grid_spec,
        compiler_params=_cparams("arbitrary"),
        name="moe_up",
    )(block_e, next_e, n_used, tok3, tok3, x, w_gate_up, b_gate_up.reshape(N_EXPERTS, 1, 2 * D_FF))


def _moe_down_kernel(be_ref, nxt_ref, nb_ref, h_ref, w_hbm, bd_ref, o_ref, w_f32, w_bf, sem):
    i = pl.program_id(0)

    def copies(e):
        return [pltpu.make_async_copy(w_hbm.at[e], w_f32, sem)]

    def cast():
        w_bf[...] = w_f32[...].astype(BF16)

    _stream_expert_weights(be_ref, nxt_ref, copies, cast)

    @pl.when(i < nb_ref[0])
    def _():
        o_ref[...] = jnp.dot(h_ref[...], w_bf[...], preferred_element_type=F32) + bd_ref[0]

    @pl.when(i >= nb_ref[0])
    def _():
        o_ref[...] = jnp.zeros_like(o_ref)


def moe_down(hb, block_e, next_e, n_used, w_down, b_down):
    n_blocks = hb.shape[0] // MOE_BLOCK
    grid_spec = pltpu.PrefetchScalarGridSpec(
        num_scalar_prefetch=3,
        grid=(n_blocks,),
        in_specs=[pl.BlockSpec((MOE_BLOCK, D_FF), lambda i, be, nx, nb: (i, 0)),
                  pl.BlockSpec(memory_space=pl.ANY),
                  pl.BlockSpec((1, 1, D_MODEL), lambda i, be, nx, nb: (be[i], 0, 0))],
        out_specs=pl.BlockSpec((MOE_BLOCK, D_MODEL), lambda i, be, nx, nb: (i, 0)),
        scratch_shapes=[pltpu.VMEM((D_FF, D_MODEL), F32), pltpu.VMEM((D_FF, D_MODEL), BF16),
                        pltpu.SemaphoreType.DMA(())],
    )
    return pl.pallas_call(
        _moe_down_kernel,
        out_shape=jax.ShapeDtypeStruct((n_blocks * MOE_BLOCK, D_MODEL), F32),
        grid_spec=grid_spec,
        compiler_params=_cparams("arbitrary"),
        name="moe_down",
    )(block_e, next_e, n_used, hb, w_down, b_down.reshape(N_EXPERTS, 1, D_MODEL))


def _combine_kernel(slot_ref, slot_next_ref, row0_ref, gate_ref, lnf_ref, yb_hbm, h_hbm, o_ref, rows_ref, h_vmem,
                    sem, hsem):
    i = pl.program_id(0)
    n_tiles = pl.num_programs(0)
    tm = o_ref.shape[-2]
    slot = lax.rem(i, 2)

    def copies(idx_ref, tile, buf):
        row0 = pl.multiple_of(row0_ref[tile], SUBLANES)
        cps = [pltpu.make_async_copy(h_hbm.at[pl.ds(row0, tm), :], h_vmem.at[buf], hsem.at[buf])]
        for n in range(TOP_K * tm):
            cps.append(pltpu.make_async_copy(yb_hbm.at[pl.ds(idx_ref[0, 0, n], 1), :],
                                             rows_ref.at[buf, n // tm, pl.ds(n % tm, 1), :], sem.at[buf]))
        return cps

    @pl.when(i == 0)
    def _():
        for cp in copies(slot_ref, 0, 0):
            cp.start()

    @pl.when(i + 1 < n_tiles)
    def _():
        for cp in copies(slot_next_ref, jnp.minimum(i + 1, n_tiles - 1), 1 - slot):
            cp.start()

    for cp in copies(slot_ref, i, slot):
        cp.wait()
    gate = gate_ref[...]
    acc = h_vmem[slot]
    for k in range(TOP_K):
        acc = acc + rows_ref[slot, k] * gate[:, k:k + 1]
    ms = jnp.mean(acc * acc, axis=-1, keepdims=True)
    o_ref[...] = (acc * lax.rsqrt(ms + RMS_EPS) * lnf_ref[...]).reshape(o_ref.shape)


def moe_combine(yb, h1, slot_of_pair, gates, row0, lnf_g, out_shape, out_index_map, tm):
    n_tiles = row0.shape[0]
    grid_spec = pltpu.PrefetchScalarGridSpec(
        num_scalar_prefetch=0,
        grid=(n_tiles,),
        in_specs=[pl.BlockSpec((1, 1, TOP_K * tm), lambda i: (i, 0, 0), memory_space=pltpu.SMEM),
                  pl.BlockSpec((1, 1, TOP_K * tm), lambda i: (jnp.minimum(i + 1, n_tiles - 1), 0, 0),
                               memory_space=pltpu.SMEM),
                  pl.BlockSpec(memory_space=pltpu.SMEM),
                  pl.BlockSpec((tm, LANES), lambda i: (i, 0)),
                  pl.BlockSpec((1, D_MODEL), lambda i: (0, 0)),
                  pl.BlockSpec(memory_space=pl.ANY),
                  pl.BlockSpec(memory_space=pl.ANY)],
        out_specs=pl.BlockSpec(out_shape[0], out_index_map),
        scratch_shapes=[pltpu.VMEM((2, TOP_K, tm, D_MODEL), F32), pltpu.VMEM((2, tm, D_MODEL), F32),
                        pltpu.SemaphoreType.DMA((2,)), pltpu.SemaphoreType.DMA((2,))],
    )
    return pl.pallas_call(
        _combine_kernel,
        out_shape=jax.ShapeDtypeStruct(out_shape[1], F32),
        grid_spec=grid_spec,
        compiler_params=_cparams("arbitrary"),
        name="moe_combine",
    )(slot_of_pair, slot_of_pair, row0, gates, lnf_g.reshape(1, D_MODEL), yb, h1)


PROMPT_PAD_T = 2176
ROW_TILE = 1104


def kernel(x_prompt, x_sample, state_ssm_conv, state_ssm, state_rwkv_shift, state_rwkv_wkv, meta_tokens, ln1_g, w_in, ssm_conv_w, ssm_conv_b, ssm_dt_bias, ssm_A_log, ssm_D, ssm_norm_g, rwkv_mu, rwkv_w0, rwkv_w2, rwkv_a0, rwkv_a2, rwkv_k_k, rwkv_k_a, rwkv_r_k, rwkv_gn_g, rwkv_gn_b, w_out, ln2_g, router_w, router_b, w_gate_up, b_gate_up, w_down, b_down, lnf_g):
    bp, seq, d = x_prompt.shape
    bs = x_sample.shape[0]
    t_prompt = N_META + seq
    tail = jnp.zeros((PROMPT_PAD_T - t_prompt, d), F32)
    pieces = []
    for b in range(bp):
        pieces += [meta_tokens, x_prompt[b], tail]
    h_rows = jnp.concatenate(pieces + [x_sample.reshape(bs, d)], axis=0)
    n_prow = bp * PROMPT_PAD_T

    l = 0
    xn = rmsnorm_rows(h_rows, ln1_g[l], BF16, ROW_TILE)
    w = w_in[l]
    o_xbc = SSM_D_INNER
    o_dt = o_xbc + SSM_CONV_DIM
    o_rw = o_dt + SSM_N_HEADS
    o_g = o_rw + RWKV_SHIFT_DIM
    w_z = w[:, :o_xbc].astype(BF16)
    w_xbc = w[:, o_xbc:o_dt].astype(BF16)
    w_dt = jnp.pad(w[:, o_dt:o_rw], ((0, 0), (0, LANES - SSM_N_HEADS))).astype(BF16)
    w_rw = w[:, o_rw:o_g].astype(BF16)
    w_g = w[:, o_g:].astype(BF16)
    z = matmul_bf16(xn, w_z, ROW_TILE, 1024, "proj_z")
    xbc = matmul_bf16(xn, w_xbc, ROW_TILE, 1024, "proj_xbc")
    dtr = matmul_bf16(xn, w_dt, ROW_TILE, LANES, "proj_dt")

    ssm_args = (ssm_conv_w[l], ssm_conv_b[l], ssm_dt_bias[l], ssm_A_log[l], ssm_D[l], ssm_norm_g[l])
    conv0_p = jnp.zeros((bp, SUBLANES, SSM_CONV_DIM), F32)
    s0_p = jnp.zeros((bp, SSM_N_HEADS, SSM_HEAD_DIM, SSM_D_STATE), F32)
    ya_p, ssm_p = ssd_branch(xbc, z, dtr, conv0_p, s0_p, *ssm_args, t=PROMPT_PAD_T, q=SSM_CHUNK,
                             t_valid=t_prompt, name="ssd_prompt")
    ya_s, ssm_s = ssd_step(xbc, z, dtr, n_prow, jnp.swapaxes(state_ssm_conv[l], 0, 1), state_ssm[l], *ssm_args,
                           bt=SUBLANES, name="ssd_sample")
    rw = matmul_bf16(xn, w_rw, ROW_TILE, 896, "proj_rw")
    rwkv_args = (rwkv_mu[l], rwkv_w0[l], rwkv_w2[l], rwkv_a0[l], rwkv_a2[l], rwkv_k_k[l], rwkv_k_a[l],
                 rwkv_r_k[l], rwkv_gn_g[l], rwkv_gn_b[l])
    shift0_p = jnp.zeros((bp, SUBLANES, RWKV_SHIFT_DIM), F32)
    wkv0_p = jnp.zeros((bp, RWKV_N_HEADS, RWKV_HEAD, RWKV_HEAD), F32)
    yb_p, wkv_p = rwkv_branch(rw, shift0_p, wkv0_p, *rwkv_args, t=PROMPT_PAD_T, cq=RWKV_CHUNK, t_valid=t_prompt,
                              name="rwkv_prompt")
    yb_s, wkv_t = rwkv_step(rw, n_prow, state_rwkv_shift[l], jnp.transpose(state_rwkv_wkv[l], (1, 2, 3, 0)),
                            *rwkv_args, name="rwkv_sample")
    wkv_s = jnp.transpose(wkv_t, (3, 0, 1, 2))
    gates = matmul_bf16(xn, w_g, ROW_TILE, 1024, "proj_gates")
    h1, xn2, top_idx, top_gate = mix_and_route(
        gates, ya_p, yb_p, ya_s, yb_s, h_rows,
        w_out[l].astype(BF16), ln2_g[l], router_w[l], router_b[l], MOE_BLOCK)

    n_rows = h_rows.shape[0]
    row_id = jnp.arange(n_rows, dtype=jnp.int32)
    valid = jnp.logical_or(row_id >= n_prow, row_id % PROMPT_PAD_T < t_prompt)
    yb, slot_of_pair = moe_expert_rows(xn2, top_idx, valid, (bp * t_prompt + bs) * TOP_K, w_gate_up[l],
                                       b_gate_up[l], w_down[l], b_down[l])

    def combine(row0, out_block, out_full, out_map):
        rows = (row0[:, None] + jnp.arange(MOE_BLOCK, dtype=jnp.int32)[None, :])
        slots = jnp.transpose(slot_of_pair[rows], (0, 2, 1)).reshape(row0.shape[0], 1, TOP_K * MOE_BLOCK)
        return moe_combine(yb, h1, slots, top_gate[rows.reshape(-1)], row0, lnf_g, (out_block, out_full),
                           out_map, MOE_BLOCK)

    tiles_per_seq = seq // MOE_BLOCK
    tile_id = jnp.arange(bp * tiles_per_seq, dtype=jnp.int32)
    row0_p = (tile_id // tiles_per_seq) * PROMPT_PAD_T + N_META + (tile_id % tiles_per_seq) * MOE_BLOCK
    y_prompt = combine(row0_p, (1, MOE_BLOCK, d), (bp, seq, d),
                       lambda i: (i // tiles_per_seq, i % tiles_per_seq, 0))
    y_sample = combine(jnp.full((1,), n_prow, jnp.int32), (MOE_BLOCK, d), (bs, d), lambda i: (i, 0))

    last = [b * PROMPT_PAD_T + t_prompt - 1 for b in range(bp)]
    prompt_conv = jnp.stack([xbc[r - (SSM_CONV - 2):r + 1] for r in last])
    prompt_shift = jnp.stack([rw[r] for r in last])
    sample_conv = jnp.concatenate([state_ssm_conv[l][:, 1:], xbc[n_prow:, None]], axis=1)
    return (y_prompt, y_sample.reshape(bs, 1, d), prompt_conv[None], ssm_p[None], prompt_shift[None], wkv_p[None],
            sample_conv[None], ssm_s[None], rw[n_prow:][None], wkv_s[None])


def moe_expert_rows(xn2, top_idx, valid, n_valid_pairs, w_gate_up, b_gate_up, w_down, b_down):
    n_rows = xn2.shape[0]
    experts = jnp.arange(N_EXPERTS, dtype=jnp.int32)
    key = jnp.where(valid[:, None], top_idx[:, :TOP_K], N_EXPERTS).reshape(-1)
    n_pairs = key.shape[0]
    n_blocks = -(-n_valid_pairs // MOE_BLOCK) + N_EXPERTS
    n_slots = n_blocks * MOE_BLOCK
    counts = jnp.sum((key[:, None] == experts[None, :]).astype(jnp.int32), axis=0)
    padded = (counts + MOE_BLOCK - 1) // MOE_BLOCK * MOE_BLOCK
    ends = jnp.cumsum(padded)
    fill_e = jnp.repeat(experts, MOE_BLOCK)
    fill_r = jnp.tile(jnp.arange(MOE_BLOCK, dtype=jnp.int32), N_EXPERTS)
    fill_key = jnp.where(fill_r < jnp.repeat(padded - counts, MOE_BLOCK), fill_e, N_EXPERTS + 1)
    keys = jnp.concatenate([key, fill_key])
    iota = jnp.arange(keys.shape[0], dtype=jnp.int32)
    _, order = lax.sort((keys, iota), num_keys=1, is_stable=True)
    _, slot_of = lax.sort((order, iota), num_keys=1, is_stable=True)
    slot_tok = jnp.where(order[:n_slots] < n_pairs, order[:n_slots] // TOP_K, 0).astype(jnp.int32)
    slot_of_pair = jnp.minimum(slot_of[:n_pairs], n_slots - 1).reshape(n_rows, TOP_K)
    block_start = jnp.arange(n_blocks, dtype=jnp.int32) * MOE_BLOCK
    block_e = jnp.minimum(jnp.sum((ends[None, :] <= block_start[:, None]).astype(jnp.int32), axis=1),
                          N_EXPERTS - 1).astype(jnp.int32)
    n_used = (ends[-1:] // MOE_BLOCK).astype(jnp.int32)
    next_e = next_run_expert(block_e)

    hb = moe_up(xn2, slot_tok, block_e, next_e, n_used, w_gate_up, b_gate_up)
    yb = moe_down(hb, block_e, next_e, n_used, w_down, b_down)
    return yb, slot_of_pair
```

```python
import functools
import math

import jax
import jax.numpy as jnp
from jax import lax
from jax.experimental import pallas as pl
from jax.experimental.pallas import tpu as pltpu

F32 = jnp.float32
BF16 = jnp.bfloat16

V7X_VMEM_LIMIT_BYTES = 60000 * 1024
LANES = 128
SUBLANES = 8

D_MODEL = 2048
N_META = 16
RMS_EPS = 1e-6
SSM_D_INNER = 2048
SSM_HEAD_DIM = 64
SSM_N_HEADS = 32
SSM_N_GROUPS = 4
SSM_HPG = 8
SSM_D_STATE = 128
SSM_CONV = 4
SSM_CHUNK = 128
SSM_CONV_DIM = SSM_D_INNER + 2 * SSM_N_GROUPS * SSM_D_STATE
SSM_NORM_EPS = 1e-5
RWKV_DIM = 2048
RWKV_HEAD = 64
RWKV_N_HEADS = 32
RWKV_LORA = 64
RWKV_GN_EPS = 64e-5
RWKV_SHIFT_DIM = 3 * RWKV_DIM + 2 * RWKV_LORA
RWKV_CHUNK = 64
N_EXPERTS = 32
TOP_K = 4
D_FF = 2048
SWIGLU_LIMIT = 7.0
SWIGLU_ALPHA = 1.702
MOE_BLOCK = 128


def _cparams(*sem):
    return pltpu.CompilerParams(dimension_semantics=sem, vmem_limit_bytes=V7X_VMEM_LIMIT_BYTES)


def _split3(x):
    hi = x.astype(BF16)
    r1 = x - hi.astype(F32)
    mid = r1.astype(BF16)
    lo = (r1 - mid.astype(F32)).astype(BF16)
    return hi, mid, lo


def _dot(a, b, dims=None):
    a = a.astype(BF16)
    b = b.astype(BF16)
    if dims is None:
        return jnp.dot(a, b, preferred_element_type=F32)
    return lax.dot_general(a, b, (dims, ((), ())), preferred_element_type=F32)


def _dot_exact_lhs(x, m01):
    hi, mid, lo = _split3(x)
    m = m01.astype(BF16)
    return (jnp.dot(hi, m, preferred_element_type=F32) + jnp.dot(mid, m, preferred_element_type=F32)
            + jnp.dot(lo, m, preferred_element_type=F32))


def _dot_split_lhs(x, m01, terms):
    m = m01.astype(BF16)
    out = None
    for part in _split3(x)[:terms]:
        d = jnp.dot(part, m, preferred_element_type=F32)
        out = d if out is None else out + d
    return out


def _dot_f32(a, b, terms=6):
    ah, am, al = _split3(a)
    bh, bm, bl = _split3(b)
    lhs = jnp.concatenate([ah, ah, am, ah, al, am][:terms], axis=1)
    rhs = jnp.concatenate([bh, bm, bh, bl, bh, bm][:terms], axis=0)
    return jnp.dot(lhs, rhs, preferred_element_type=F32)


def _softplus(x):
    return jnp.maximum(x, 0.0) + jnp.log(1.0 + jnp.exp(-jnp.abs(x)))


def _sigmoid(x):
    return 1.0 / (1.0 + jnp.exp(-x))


def _silu(x):
    return x * _sigmoid(x)


def _rmsnorm_kernel(x_ref, g_ref, o_ref, *, eps):
    x = x_ref[...]
    ms = jnp.mean(x * x, axis=-1, keepdims=True)
    o_ref[...] = (x * lax.rsqrt(ms + eps) * g_ref[...]).astype(o_ref.dtype)


def rmsnorm_rows(x, g, out_dtype, tm):
    rows, d = x.shape
    return pl.pallas_call(
        functools.partial(_rmsnorm_kernel, eps=RMS_EPS),
        out_shape=jax.ShapeDtypeStruct((rows, d), out_dtype),
        grid=(rows // tm,),
        in_specs=[pl.BlockSpec((tm, d), lambda i: (i, 0)), pl.BlockSpec((1, d), lambda i: (0, 0))],
        out_specs=pl.BlockSpec((tm, d), lambda i: (i, 0)),
        compiler_params=_cparams("parallel"),
        name="rmsnorm_rows",
    )(x, g.reshape(1, d))


def _matmul_kernel(x_ref, w_ref, o_ref):
    o_ref[...] = jnp.dot(x_ref[...], w_ref[...], preferred_element_type=F32)


def matmul_bf16(x, w, tm, tn, name, col0=0, n=None):
    rows, k = x.shape
    n = w.shape[1] if n is None else n
    j0 = col0 // tn
    return pl.pallas_call(
        _matmul_kernel,
        out_shape=jax.ShapeDtypeStruct((rows, n), F32),
        grid=(n // tn, rows // tm),
        in_specs=[pl.BlockSpec((tm, k), lambda j, i: (i, 0)), pl.BlockSpec((k, tn), lambda j, i: (0, j0 + j))],
        out_specs=pl.BlockSpec((tm, tn), lambda j, i: (i, j)),
        compiler_params=_cparams("parallel", "parallel"),
        name=name,
    )(x, w)


def _ssd_kernel(xbc_ref, z_ref, dtr_ref, conv0_ref, s0_ref, convw_ref, convb_ref, dtb_ref, alog_ref,
                dskip_ref, ng_ref, expand_ref, y_ref, s_ref, buf_ref, *, q, t_valid):
    c = pl.program_id(1)
    n_pad = buf_ref.shape[0] - q

    @pl.when(c == 0)
    def _():
        buf_ref[0:n_pad, :] = conv0_ref[0]
        s_ref[0] = s0_ref[0]

    buf_ref[n_pad:n_pad + q, :] = xbc_ref[...]
    acc = convb_ref[...] + convw_ref[SSM_CONV - 1:SSM_CONV, :] * buf_ref[n_pad:n_pad + q, :]
    for k in range(SSM_CONV - 1):
        off = n_pad - (SSM_CONV - 1) + k
        acc = acc + convw_ref[k:k + 1, :] * buf_ref[off:off + q, :]
    buf_ref[0:n_pad, :] = buf_ref[q:q + n_pad, :]
    xbc = _silu(acc)
    gn = SSM_N_GROUPS * SSM_D_STATE
    xs = xbc[:, :SSM_D_INNER]
    bm = xbc[:, SSM_D_INNER:SSM_D_INNER + gn]
    cm = xbc[:, SSM_D_INNER + gn:]

    row = lax.broadcasted_iota(jnp.int32, (q, LANES), 0) + c * q
    lane = lax.broadcasted_iota(jnp.int32, (q, LANES), 1)
    live = jnp.logical_and(row < t_valid, lane < SSM_N_HEADS)
    dt = jnp.where(live, _softplus(dtr_ref[...] + dtb_ref[...]), 0.0)
    da = dt * (-jnp.exp(alog_ref[...]))

    ti = lax.broadcasted_iota(jnp.int32, (q, q), 0)
    si = lax.broadcasted_iota(jnp.int32, (q, q), 1)
    causal = si <= ti
    tril = causal.astype(BF16)
    dh, dm, dl = _split3(da)
    cs = (jnp.dot(tril, dh, preferred_element_type=F32) + jnp.dot(tril, dm, preferred_element_type=F32)
          + jnp.dot(tril, dl, preferred_element_type=F32))
    eye = (lax.broadcasted_iota(jnp.int32, (LANES, LANES), 0)
           == lax.broadcasted_iota(jnp.int32, (LANES, LANES), 1)).astype(BF16)
    ch, cmid, cl = _split3(cs)
    nt = (((1,), (1,)), ((), ()))
    cs_t = (lax.dot_general(eye, ch, nt, preferred_element_type=F32)
            + lax.dot_general(eye, cmid, nt, preferred_element_type=F32)
            + lax.dot_general(eye, cl, nt, preferred_element_type=F32))

    expand = expand_ref[...]
    dt_x = _dot_exact_lhs(dt, expand)
    cs_x = _dot_exact_lhs(cs, expand)
    xdt = xs * dt_x
    ecs = jnp.exp(cs_x)
    dte = jnp.exp(cs_x[q - 1:q, :] - cs_x)

    lane_in_pair = lax.broadcasted_iota(jnp.int32, (q, LANES), 1)
    lo_half = lane_in_pair < SSM_HEAD_DIM

    y_groups = []
    for g in range(SSM_N_GROUPS):
        b_g = bm[:, g * SSM_D_STATE:(g + 1) * SSM_D_STATE]
        c_g = cm[:, g * SSM_D_STATE:(g + 1) * SSM_D_STATE]
        cb = _dot(c_g, b_g, ((1,), (1,)))
        gsl = slice(g * SSM_HPG * SSM_HEAD_DIM, (g + 1) * SSM_HPG * SSM_HEAD_DIM)
        s_g = s_ref[0, g * SSM_HPG:(g + 1) * SSM_HPG].reshape(SSM_HPG * SSM_HEAD_DIM, SSM_D_STATE)
        y_off = _dot(c_g, s_g, ((1,), (1,))) * ecs[:, gsl]
        slabs = []
        for m in range(SSM_HPG // 2):
            lsl = slice(gsl.start + m * LANES, gsl.start + (m + 1) * LANES)
            x_pair = xdt[:, lsl]
            acc_pair = None
            for half in range(2):
                h = g * SSM_HPG + 2 * m + half
                seg = cs[:, h:h + 1] - cs_t[h:h + 1, :]
                lmat = jnp.where(causal, jnp.exp(jnp.where(causal, seg, 0.0)), 0.0)
                x_half = jnp.where(lo_half if half == 0 else jnp.logical_not(lo_half), x_pair, 0.0)
                contrib = _dot(cb * lmat, x_half)
                acc_pair = contrib if acc_pair is None else acc_pair + contrib
            slabs.append(acc_pair)
        y_groups.append(jnp.concatenate(slabs, axis=1) + y_off)

        xw = xdt[:, gsl] * dte[:, gsl]
        upd = _dot(xw, b_g, ((0,), (0,)))
        for j in range(SSM_HPG):
            h = g * SSM_HPG + j
            dec = jnp.exp(cs_t[h:h + 1, q - 1:q])
            s_ref[0, h] = s_ref[0, h] * dec + upd[j * SSM_HEAD_DIM:(j + 1) * SSM_HEAD_DIM, :]

    y = jnp.concatenate(y_groups, axis=1) + dskip_ref[...] * xs
    y = y * _silu(z_ref[...])
    outs = []
    gw = SSM_D_INNER // SSM_N_GROUPS
    for g in range(SSM_N_GROUPS):
        yg = y[:, g * gw:(g + 1) * gw]
        outs.append(yg * lax.rsqrt(jnp.mean(yg * yg, axis=-1, keepdims=True) + SSM_NORM_EPS))
    y_ref[...] = jnp.concatenate(outs, axis=1) * ng_ref[...]


def ssd_branch(xbc, z, dtr, conv0, s0, conv_w, conv_b, dt_bias, a_log, d_skip, norm_g, *, t, q, t_valid, name):
    nb = s0.shape[0]
    cps = t // q
    n_pad = conv0.shape[1]
    pad_l = lambda v: jnp.pad(v.astype(F32), (0, LANES - v.shape[0])).reshape(1, LANES)
    expand = (jnp.arange(LANES)[:, None] == (jnp.arange(SSM_D_INNER) // SSM_HEAD_DIM)[None, :]).astype(BF16)
    row2 = lambda v: v.reshape(1, -1)
    full = lambda a: pl.BlockSpec(a.shape, lambda b, c: (0,) * a.ndim)
    consts = [conv_w, row2(conv_b), pad_l(dt_bias), pad_l(a_log), row2(jnp.repeat(d_skip, SSM_HEAD_DIM)),
              row2(norm_g), expand]
    kern = functools.partial(_ssd_kernel, q=q, t_valid=t_valid)
    return pl.pallas_call(
        kern,
        out_shape=(jax.ShapeDtypeStruct((nb * t, SSM_D_INNER), F32),
                   jax.ShapeDtypeStruct(s0.shape, F32)),
        grid=(nb, cps),
        in_specs=[pl.BlockSpec((q, SSM_CONV_DIM), lambda b, c: (b * cps + c, 0)),
                  pl.BlockSpec((q, SSM_D_INNER), lambda b, c: (b * cps + c, 0)),
                  pl.BlockSpec((q, LANES), lambda b, c: (b * cps + c, 0)),
                  pl.BlockSpec((1, n_pad, SSM_CONV_DIM), lambda b, c: (b, 0, 0)),
                  pl.BlockSpec((1,) + s0.shape[1:], lambda b, c: (b, 0, 0, 0))] + [full(a) for a in consts],
        out_specs=(pl.BlockSpec((q, SSM_D_INNER), lambda b, c: (b * cps + c, 0)),
                   pl.BlockSpec((1,) + s0.shape[1:], lambda b, c: (b, 0, 0, 0))),
        scratch_shapes=[pltpu.VMEM((q + n_pad, SSM_CONV_DIM), F32)],
        compiler_params=_cparams("parallel", "arbitrary"),
        name=name,
    )(xbc, z, dtr, conv0, s0, *consts)


def _ssd_step_kernel(xbc_ref, z_ref, dtr_ref, conv_ref, s0_ref, convw_ref, convb_ref, dtb_ref, alog_ref, dskip_ref,
                     ng_ref, expand_ref, y_ref, s_ref):
    bt = xbc_ref.shape[0]
    acc = convb_ref[...] + convw_ref[SSM_CONV - 1:SSM_CONV, :] * xbc_ref[...]
    for k in range(SSM_CONV - 1):
        acc = acc + convw_ref[k:k + 1, :] * conv_ref[k]
    xbc = _silu(acc)
    gn = SSM_N_GROUPS * SSM_D_STATE
    xs = xbc[:, :SSM_D_INNER]
    bm = xbc[:, SSM_D_INNER:SSM_D_INNER + gn]
    cm = xbc[:, SSM_D_INNER + gn:]
    lane = lax.broadcasted_iota(jnp.int32, (bt, LANES), 1)
    dt = jnp.where(lane < SSM_N_HEADS, _softplus(dtr_ref[...] + dtb_ref[...]), 0.0)
    dec = jnp.exp(dt * (-jnp.exp(alog_ref[...])))
    xdt = xs * _dot_exact_lhs(dt, expand_ref[...])
    hd = SSM_HPG * SSM_HEAD_DIM
    rowid = lax.broadcasted_iota(jnp.int32, (bt, hd), 0)
    tn = ((0,), (0,))
    y_acc = [jnp.zeros((bt, hd), F32) for _ in range(SSM_N_GROUPS)]
    for bb in range(bt):
        mine = rowid == bb
        for g in range(SSM_N_GROUPS):
            b_g = bm[:, g * SSM_D_STATE:(g + 1) * SSM_D_STATE]
            c_g = cm[:, g * SSM_D_STATE:(g + 1) * SSM_D_STATE]
            xh, xm, xl = _split3(jnp.where(mine, xdt[:, g * hd:(g + 1) * hd], 0.0))
            bh, bmid, bl = _split3(b_g)
            upd = _dot(jnp.concatenate([xh, xh, xm, xh, xl, xm], axis=0),
                       jnp.concatenate([bh, bmid, bh, bl, bh, bmid], axis=0), tn)
            heads = []
            for j in range(SSM_HPG):
                h = g * SSM_HPG + j
                s_new = s0_ref[bb, h] * dec[bb:bb + 1, h:h + 1] + upd[j * SSM_HEAD_DIM:(j + 1) * SSM_HEAD_DIM, :]
                s_ref[bb, h] = s_new
                heads.append(s_new)
            y_all = _dot(c_g, jnp.concatenate(heads, axis=0), ((1,), (1,)))
            y_acc[g] = jnp.where(mine, y_all, y_acc[g])
    y = jnp.concatenate(y_acc, axis=1) + dskip_ref[...] * xs
    y = y * _silu(z_ref[...])
    outs = []
    gw = SSM_D_INNER // SSM_N_GROUPS
    for g in range(SSM_N_GROUPS):
        yg = y[:, g * gw:(g + 1) * gw]
        outs.append(yg * lax.rsqrt(jnp.mean(yg * yg, axis=-1, keepdims=True) + SSM_NORM_EPS))
    y_ref[...] = jnp.concatenate(outs, axis=1) * ng_ref[...]


def ssd_step(xbc, z, dtr, row0, conv_prev, s0, conv_w, conv_b, dt_bias, a_log, d_skip, norm_g, *, bt, name):
    nb = s0.shape[0]
    pad_l = lambda v: jnp.pad(v.astype(F32), (0, LANES - v.shape[0])).reshape(1, LANES)
    expand = (jnp.arange(LANES)[:, None] == (jnp.arange(SSM_D_INNER) // SSM_HEAD_DIM)[None, :]).astype(BF16)
    row2 = lambda v: v.reshape(1, -1)
    full = lambda a: pl.BlockSpec(a.shape, lambda i: (0,) * a.ndim)
    consts = [conv_w, row2(conv_b), pad_l(dt_bias), pad_l(a_log), row2(jnp.repeat(d_skip, SSM_HEAD_DIM)),
              row2(norm_g), expand]
    rows = lambda n: pl.BlockSpec((bt, n), lambda i: (row0 // bt + i, 0))
    state_spec = pl.BlockSpec((bt,) + s0.shape[1:], lambda i: (i, 0, 0, 0))
    return pl.pallas_call(
        _ssd_step_kernel,
        out_shape=(jax.ShapeDtypeStruct((nb, SSM_D_INNER), F32), jax.ShapeDtypeStruct(s0.shape, F32)),
        grid=(nb // bt,),
        in_specs=[rows(SSM_CONV_DIM), rows(SSM_D_INNER), rows(LANES),
                  pl.BlockSpec((SSM_CONV - 1, bt, SSM_CONV_DIM), lambda i: (0, i, 0)),
                  state_spec] + [full(a) for a in consts],
        out_specs=(pl.BlockSpec((bt, SSM_D_INNER), lambda i: (i, 0)), state_spec),
        compiler_params=_cparams("parallel"),
        name=name,
    )(xbc, z, dtr, conv_prev, s0, *consts)


RWKV_GROUP_HEADS = 4
RWKV_GROUP_LANES = RWKV_GROUP_HEADS * RWKV_HEAD


def _rep_rows(x, n):
    return jnp.concatenate([x] * n, axis=0)


def _head_mask(rows_per_head, cols_per_head, n_rows, n_cols):
    rr = lax.broadcasted_iota(jnp.int32, (n_rows, n_cols), 0) // rows_per_head
    cc = lax.broadcasted_iota(jnp.int32, (n_rows, n_cols), 1) // cols_per_head
    return rr == cc


def _rwkv_prep(cur, prev, live, mu_ref, w0_ref, w2_ref, a0_ref, a2_ref, kk_ref, ka_ref, rk_ref, hred_ref,
               hexp_ref):
    rows = cur.shape[0]
    mixed = cur + (prev - cur) * mu_ref[...]
    r = mixed[:, 0:RWKV_DIM]
    k = mixed[:, RWKV_DIM:2 * RWKV_DIM]
    v = mixed[:, 2 * RWKV_DIM:3 * RWKV_DIM]
    lora = mixed[:, 3 * RWKV_DIM:]
    lane = lax.broadcasted_iota(jnp.int32, (rows, LANES), 1)
    wl = jnp.where(lane < RWKV_LORA, jnp.tanh(lora), 0.0)
    al = jnp.where(lane >= RWKV_LORA, lora, 0.0)
    logw = -math.exp(-0.5) * _sigmoid(w0_ref[...] + _dot_f32(wl, w2_ref[...], terms=3))
    a = _sigmoid(a0_ref[...] + _dot_f32(al, a2_ref[...], terms=3))
    kk_raw = k * kk_ref[...]
    k2 = k * (1.0 + (a - 1.0) * ka_ref[...])
    sums = _dot_split_lhs(jnp.concatenate([kk_raw * kk_raw, r * k2 * rk_ref[...]], axis=0), hred_ref[...], 2)
    inv_n = 1.0 / jnp.maximum(jnp.sqrt(sums[:rows]), 1e-12)
    kk = kk_raw * _dot_split_lhs(inv_n, hexp_ref[...], 2)
    if live is not None:
        logw = jnp.where(live, logw, 0.0)
        kk = jnp.where(live, kk, 0.0)
        k2 = jnp.where(live, k2, 0.0)
        v = jnp.where(live, v, 0.0)
    return r, k2, v, kk, a, logw, sums[rows:]


def _rwkv_post(y, v, bonus_sum, gng_ref, gnb_ref, hred_ref, hexp_ref):
    rows = y.shape[0]
    inv_h = 1.0 / RWKV_HEAD
    sums = _dot_split_lhs(jnp.concatenate([y, y * y], axis=0), hred_ref[...], 3) * inv_h
    mean = sums[:rows]
    var = jnp.maximum(sums[rows:] - mean * mean, 0.0)
    per_head = jnp.concatenate([mean, lax.rsqrt(var + RWKV_GN_EPS), bonus_sum], axis=0)
    wide = _dot_split_lhs(per_head, hexp_ref[...], 2)
    return ((y - wide[:rows]) * wide[rows:2 * rows] * gng_ref[...] + gnb_ref[...]
            + wide[2 * rows:] * v)


def _state_to_blockdiag(stack, m_state):
    return jnp.where(m_state, jnp.concatenate([stack] * RWKV_GROUP_HEADS, axis=1), 0.0)


def _blockdiag_to_state(s_bd):
    stack = s_bd[:, 0:RWKV_HEAD]
    for hh in range(1, RWKV_GROUP_HEADS):
        stack = stack + s_bd[:, hh * RWKV_HEAD:(hh + 1) * RWKV_HEAD]
    return stack


def _rwkv_kernel(rw_ref, shift0_ref, wkv0_ref, mu_ref, w0_ref, w2_ref, a0_ref, a2_ref, kk_ref, ka_ref, rk_ref,
                 gng_ref, gnb_ref, hred_ref, hexp_ref,
                 y_ref, wkv_ref, buf_ref, s_ref, *, cq, t_valid):
    c = pl.program_id(1)
    n_chunks = pl.num_programs(1)
    n_pad = buf_ref.shape[0] - cq
    gl = RWKV_GROUP_LANES
    gh = RWKV_GROUP_HEADS
    n_groups = RWKV_DIM // gl
    nt = ((1,), (1,))
    tn = ((0,), (0,))

    m_state = _head_mask(RWKV_HEAD, RWKV_HEAD, gl, gl)
    m_exp = _head_mask(cq, RWKV_HEAD, gh * cq, gl)
    m_nbd = _head_mask(cq, cq, gh * cq, gh * cq)

    @pl.when(c == 0)
    def _():
        buf_ref[0:n_pad, :] = shift0_ref[0]
        for g in range(n_groups):
            s_ref[g] = _state_to_blockdiag(wkv0_ref[0, g * gl:(g + 1) * gl, :], m_state)

    buf_ref[n_pad:n_pad + cq, :] = rw_ref[...]
    cur = buf_ref[n_pad:n_pad + cq, :]
    prev = buf_ref[n_pad - 1:n_pad - 1 + cq, :]
    buf_ref[0:n_pad, :] = buf_ref[cq:cq + n_pad, :]
    row = lax.broadcasted_iota(jnp.int32, (cq, 1), 0) + c * cq
    r, k2, v, kk, a, logw, bonus_sum = _rwkv_prep(cur, prev, row < t_valid, mu_ref, w0_ref, w2_ref, a0_ref, a2_ref,
                                                  kk_ref, ka_ref, rk_ref, hred_ref, hexp_ref)

    ti = lax.broadcasted_iota(jnp.int32, (cq, cq), 0)
    si = lax.broadcasted_iota(jnp.int32, (cq, cq), 1)
    tril = (si <= ti).astype(BF16)
    lh, lm, ll = _split3(logw)
    cl = (jnp.dot(tril, lh, preferred_element_type=F32) + jnp.dot(tril, lm, preferred_element_type=F32)
          + jnp.dot(tril, ll, preferred_element_type=F32))
    p_incl = jnp.exp(cl)
    p_inv = jnp.exp(-cl)
    a_t = jnp.exp(cl - logw) * (-kk)
    r_t = p_incl * r
    b_t = p_inv * (kk * a)
    k_t = p_inv * k2
    p_end = p_incl[cq - 1:cq, :]

    tt = lax.broadcasted_iota(jnp.int32, (cq, gh * cq), 0)
    ss_i = lax.broadcasted_iota(jnp.int32, (cq, gh * cq), 1) % cq
    strict = ss_i < tt
    incl = ss_i <= tt

    def expand_rows(x):
        return jnp.where(m_exp, _rep_rows(x, gh), 0.0)

    groups = range(n_groups)
    sls = [slice(g * gl, (g + 1) * gl) for g in groups]
    lhs_ar = [jnp.concatenate([a_t[:, sl], r_t[:, sl]], axis=0) for sl in sls]
    v_exp = [expand_rows(v[:, sl]) for sl in sls]
    s_bd = [s_ref[g] for g in groups]
    hs = gh * cq
    abk = [_dot(lhs_ar[g], jnp.concatenate([expand_rows(b_t[:, sls[g]]), expand_rows(k_t[:, sls[g]])], axis=0), nt)
           for g in groups]
    ab = [m[:, :hs] for m in abk]
    ak = [m[:, hs:] for m in abk]
    w_s = [_dot(lhs_ar[g], s_bd[g], nt) for g in groups]
    x = [w_s[g][:cq] + _dot(jnp.where(strict, ak[g][:cq], 0.0), v_exp[g]) for g in groups]
    n_cat = [jnp.where(strict, ab[g][:cq], 0.0) for g in groups]
    n_steps = int(math.log2(cq))
    for step in range(n_steps):
        x = [x[g] + _dot(n_cat[g], expand_rows(x[g])) for g in groups]
        if step + 1 < n_steps:
            n_cat = [_dot(n_cat[g], jnp.where(m_nbd, _rep_rows(n_cat[g], gh), 0.0)) for g in groups]
    incl2 = jnp.concatenate([incl, incl], axis=1)
    y_groups = [w_s[g][cq:] + _dot(jnp.where(incl2, abk[g][cq:], 0.0),
                                   jnp.concatenate([expand_rows(x[g]), v_exp[g]], axis=0)) for g in groups]
    for g in groups:
        uv = jnp.concatenate([x[g], v[:, sls[g]]], axis=0)
        bk = jnp.concatenate([b_t[:, sls[g]], k_t[:, sls[g]]], axis=0)
        inc = _dot(uv, bk, tn)
        s_ref[g] = jnp.where(m_state, (s_bd[g] + inc) * p_end[:, sls[g]], 0.0)

    y = jnp.concatenate(y_groups, axis=1)
    y_ref[...] = _rwkv_post(y, v, bonus_sum, gng_ref, gnb_ref, hred_ref, hexp_ref)

    @pl.when(c == n_chunks - 1)
    def _():
        for g in range(n_groups):
            wkv_ref[0, g * gl:(g + 1) * gl, :] = _blockdiag_to_state(s_ref[g])


def _rwkv_consts(mu, w0, w2, a0, a2, k_k, k_a, r_k, gn_g, gn_b):
    row2 = lambda a: a.reshape(1, -1).astype(F32)
    zeros_l = jnp.zeros((RWKV_LORA, RWKV_DIM), F32)
    head_of_lane = jnp.arange(RWKV_DIM) // RWKV_HEAD
    hexp = (jnp.arange(LANES)[:, None] == head_of_lane[None, :]).astype(BF16)
    return [row2(mu), row2(w0), jnp.concatenate([w2, zeros_l], axis=0), row2(a0),
            jnp.concatenate([zeros_l, a2], axis=0), row2(k_k), row2(k_a), row2(r_k), row2(gn_g), row2(gn_b),
            hexp.T, hexp]


def _rwkv_step_prep_kernel(rw_ref, prev_ref, mu_ref, w0_ref, w2_ref, a0_ref, a2_ref, kk_ref, ka_ref, rk_ref,
                           gng_ref, gnb_ref, hred_ref, hexp_ref, vecs_ref, v_ref, bsum_ref):
    r, k2, v, kk, a, logw, bonus_sum = _rwkv_prep(rw_ref[...], prev_ref[...], None, mu_ref, w0_ref, w2_ref, a0_ref,
                                                  a2_ref, kk_ref, ka_ref, rk_ref, hred_ref, hexp_ref)
    for n, vec in enumerate((-kk, kk * a, jnp.exp(logw), k2, r, v)):
        vecs_ref[n] = vec.T
    v_ref[...] = v
    bsum_ref[...] = bonus_sum


def _rwkv_step_state_kernel(vecs_ref, s_ref, s_out_ref, yt_ref):
    a_t, b_t, w_t, k_t, r_t = (vecs_ref[n] for n in range(5))

    def body(i, carry):
        s_i = s_ref[0, i]
        u = jnp.sum(s_i * a_t, axis=0, keepdims=True)
        s_new = s_i * w_t + u * b_t + vecs_ref[5, pl.ds(i, 1), :] * k_t
        s_out_ref[0, i] = s_new
        yt_ref[pl.ds(i, 1), :] = jnp.sum(s_new * r_t, axis=0, keepdims=True)
        return carry

    lax.fori_loop(0, RWKV_HEAD, body, 0, unroll=4)


def _rwkv_step_post_kernel(yt_ref, v_ref, bsum_ref, gng_ref, gnb_ref, hred_ref, hexp_ref, y_ref):
    y_ref[...] = _rwkv_post(yt_ref[...].T, v_ref[...], bsum_ref[...], gng_ref, gnb_ref, hred_ref, hexp_ref)


def rwkv_step(rw, row0, prev, wkv_t, *params, name):
    nb = prev.shape[0]
    consts = _rwkv_consts(*params)
    full = lambda a: pl.BlockSpec(a.shape, lambda i: (0,) * a.ndim)
    n_vec = 6
    vecs, v_rows, bsum = pl.pallas_call(
        _rwkv_step_prep_kernel,
        out_shape=(jax.ShapeDtypeStruct((n_vec, RWKV_DIM, nb), F32), jax.ShapeDtypeStruct((nb, RWKV_DIM), F32),
                   jax.ShapeDtypeStruct((nb, LANES), F32)),
        grid=(1,),
        in_specs=[pl.BlockSpec((nb, RWKV_SHIFT_DIM), lambda i: (row0 // nb, 0)),
                  pl.BlockSpec((nb, RWKV_SHIFT_DIM), lambda i: (0, 0))] + [full(a) for a in consts],
        out_specs=(pl.BlockSpec((n_vec, RWKV_DIM, nb), lambda i: (0, 0, 0)),
                   pl.BlockSpec((nb, RWKV_DIM), lambda i: (0, 0)), pl.BlockSpec((nb, LANES), lambda i: (0, 0))),
        compiler_params=_cparams("arbitrary"),
        name=name + "_prep",
    )(rw, prev, *consts)
    state_spec = pl.BlockSpec((1, RWKV_HEAD, RWKV_HEAD, nb), lambda h: (h, 0, 0, 0))
    wkv_new, y_t = pl.pallas_call(
        _rwkv_step_state_kernel,
        out_shape=(jax.ShapeDtypeStruct(wkv_t.shape, F32), jax.ShapeDtypeStruct((RWKV_DIM, nb), F32)),
        grid=(RWKV_N_HEADS,),
        in_specs=[pl.BlockSpec((n_vec, RWKV_HEAD, nb), lambda h: (0, h, 0)), state_spec],
        out_specs=(state_spec, pl.BlockSpec((RWKV_HEAD, nb), lambda h: (h, 0))),
        compiler_params=_cparams("parallel"),
        name=name + "_state",
    )(vecs, wkv_t)
    post_consts = consts[8:]
    y = pl.pallas_call(
        _rwkv_step_post_kernel,
        out_shape=jax.ShapeDtypeStruct((nb, RWKV_DIM), F32),
        grid=(1,),
        in_specs=[pl.BlockSpec((RWKV_DIM, nb), lambda i: (0, 0)), pl.BlockSpec((nb, RWKV_DIM), lambda i: (0, 0)),
                  pl.BlockSpec((nb, LANES), lambda i: (0, 0))] + [full(a) for a in post_consts],
        out_specs=pl.BlockSpec((nb, RWKV_DIM), lambda i: (0, 0)),
        compiler_params=_cparams("arbitrary"),
        name=name + "_post",
    )(y_t, v_rows, bsum, *post_consts)
    return y, wkv_new


def rwkv_branch(rw, shift0, wkv0, *params, t, cq, t_valid, name):
    nb = wkv0.shape[0]
    cps = t // cq
    n_pad = shift0.shape[1]
    consts = _rwkv_consts(*params)
    full = lambda a: pl.BlockSpec(a.shape, lambda b, c: (0,) * a.ndim)
    wkv_rows = wkv0.reshape(nb, RWKV_DIM, RWKV_HEAD)
    kern = functools.partial(_rwkv_kernel, cq=cq, t_valid=t_valid)
    y, wkv = pl.pallas_call(
        kern,
        out_shape=(jax.ShapeDtypeStruct((nb * t, RWKV_DIM), F32),
                   jax.ShapeDtypeStruct(wkv_rows.shape, F32)),
        grid=(nb, cps),
        in_specs=[pl.BlockSpec((cq, RWKV_SHIFT_DIM), lambda b, c: (b * cps + c, 0)),
                  pl.BlockSpec((1, n_pad, RWKV_SHIFT_DIM), lambda b, c: (b, 0, 0)),
                  pl.BlockSpec((1, RWKV_DIM, RWKV_HEAD), lambda b, c: (b, 0, 0))] + [full(a) for a in consts],
        out_specs=(pl.BlockSpec((cq, RWKV_DIM), lambda b, c: (b * cps + c, 0)),
                   pl.BlockSpec((1, RWKV_DIM, RWKV_HEAD), lambda b, c: (b, 0, 0))),
        scratch_shapes=[pltpu.VMEM((cq + n_pad, RWKV_SHIFT_DIM), F32),
                        pltpu.VMEM((RWKV_DIM // RWKV_GROUP_LANES, RWKV_GROUP_LANES, RWKV_GROUP_LANES), F32)],
        compiler_params=_cparams("parallel", "arbitrary"),
        name=name,
    )(rw, shift0, wkv_rows, *consts)
    return y, wkv.reshape(wkv0.shape)


NEG_BIG = -1e30


def _mix_kernel(g_ref, yap_ref, ybp_ref, yas_ref, ybs_ref, h_ref, wout_ref, ln2_ref, rw_ref, rb_ref,
                h1_ref, xn_ref, idx_ref, gate_ref):
    i = pl.program_id(0)
    last = i == pl.num_programs(0) - 1
    ya = jnp.where(last, yas_ref[...], yap_ref[...])
    yb = jnp.where(last, ybs_ref[...], ybp_ref[...])
    g = g_ref[...]
    merged = _sigmoid(g[:, :D_MODEL]) * ya + _sigmoid(g[:, D_MODEL:]) * yb
    h1 = h_ref[...] + jnp.dot(merged.astype(BF16), wout_ref[...], preferred_element_type=F32)
    h1_ref[...] = h1
    ms = jnp.mean(h1 * h1, axis=-1, keepdims=True)
    xn = h1 * lax.rsqrt(ms + RMS_EPS) * ln2_ref[...]
    xn_ref[...] = xn
    xh, xm, xl = _split3(xn)
    logits = jnp.dot(jnp.concatenate([xh, xh, xm, xh, xl, xm], axis=1), rw_ref[...],
                     preferred_element_type=F32) + rb_ref[...]
    lane = lax.broadcasted_iota(jnp.int32, logits.shape, 1).astype(F32)
    idx_out = jnp.zeros(logits.shape, F32)
    val_out = jnp.zeros(logits.shape, F32)
    top0 = None
    for kth in range(TOP_K):
        m = jnp.max(logits, axis=-1, keepdims=True)
        sel = jnp.min(jnp.where(logits == m, lane, float(LANES)), axis=-1, keepdims=True)
        if kth == 0:
            top0 = m
        idx_out = jnp.where(lane == kth, sel, idx_out)
        val_out = jnp.where(lane == kth, jnp.exp(m - top0), val_out)
        logits = jnp.where(lane == sel, -jnp.inf, logits)
    idx_ref[...] = idx_out.astype(jnp.int32)
    gate_ref[...] = val_out / jnp.sum(val_out, axis=-1, keepdims=True)


def mix_and_route(gates, ya_p, yb_p, ya_s, yb_s, h_rows, w_out_bf, ln2_g, router_w, router_b, tm):
    rows = h_rows.shape[0]
    n_tiles = rows // tm
    last_p = ya_p.shape[0] // tm - 1
    wh, wm, wl = _split3(jnp.pad(router_w, ((0, 0), (0, LANES - N_EXPERTS))))
    rw_pad = jnp.concatenate([wh, wm, wh, wl, wh, wm], axis=0)
    rb_pad = jnp.pad(router_b, (0, LANES - N_EXPERTS), constant_values=NEG_BIG).reshape(1, LANES)
    row_spec = lambda n: pl.BlockSpec((tm, n), lambda i: (i, 0))
    prompt_spec = pl.BlockSpec((tm, D_MODEL), lambda i: (jnp.minimum(i, last_p), 0))
    fixed = lambda a: pl.BlockSpec(a.shape, lambda i: (0,) * a.ndim)
    return pl.pallas_call(
        _mix_kernel,
        out_shape=(jax.ShapeDtypeStruct((rows, D_MODEL), F32), jax.ShapeDtypeStruct((rows, D_MODEL), F32),
                   jax.ShapeDtypeStruct((rows, LANES), jnp.int32), jax.ShapeDtypeStruct((rows, LANES), F32)),
        grid=(n_tiles,),
        in_specs=[row_spec(2 * D_MODEL), prompt_spec, prompt_spec, fixed(ya_s), fixed(yb_s), row_spec(D_MODEL),
                  fixed(w_out_bf), pl.BlockSpec((1, D_MODEL), lambda i: (0, 0)), fixed(rw_pad), fixed(rb_pad)],
        out_specs=(row_spec(D_MODEL), row_spec(D_MODEL), row_spec(LANES), row_spec(LANES)),
        compiler_params=_cparams("parallel"),
        name="mix_and_route",
    )(gates, ya_p, yb_p, ya_s, yb_s, h_rows, w_out_bf, ln2_g.reshape(1, D_MODEL), rw_pad, rb_pad)


def next_run_expert(block_e):
    n = block_e.shape[0]
    idx = jnp.arange(n, dtype=jnp.int32)
    change = jnp.concatenate([jnp.ones((1,), bool), block_e[1:] != block_e[:-1]])
    later_change = jnp.concatenate([jnp.where(change, idx, n)[1:], jnp.full((1,), n, jnp.int32)])
    next_pos = lax.cummin(later_change, axis=0, reverse=True)
    onehot = next_pos[:, None] == idx[None, :]
    return jnp.where(next_pos < n, jnp.sum(jnp.where(onehot, block_e[None, :], 0), axis=1), -1).astype(jnp.int32)


def _stream_expert_weights(be_ref, nxt_ref, copies, cast):
    i = pl.program_id(0)
    first = jnp.logical_or(i == 0, be_ref[i] != be_ref[jnp.maximum(i, 1) - 1])

    @pl.when(i == 0)
    def _():
        for cp in copies(be_ref[0]):
            cp.start()

    @pl.when(first)
    def _():
        for cp in copies(be_ref[i]):
            cp.wait()
        cast()

    @pl.when(jnp.logical_and(first, nxt_ref[i] >= 0))
    def _():
        for cp in copies(nxt_ref[i]):
            cp.start()


def _moe_up_kernel(be_ref, nxt_ref, nb_ref, tok_ref, tok_next_ref, x_hbm, w_hbm, b_ref, o_ref, w_f32, w_bf, sem,
                   rows_ref, rsem):
    i = pl.program_id(0)
    n_used = nb_ref[0]
    slot = lax.rem(i, 2)

    def row_copies(idx_ref, buf):
        return [pltpu.make_async_copy(x_hbm.at[pl.ds(idx_ref[0, 0, r], 1), :],
                                      rows_ref.at[buf, pl.ds(r, 1), :], rsem.at[buf]) for r in range(MOE_BLOCK)]

    @pl.when(jnp.logical_and(i == 0, n_used > 0))
    def _():
        for cp in row_copies(tok_ref, 0):
            cp.start()

    @pl.when(i + 1 < n_used)
    def _():
        for cp in row_copies(tok_next_ref, 1 - slot):
            cp.start()

    def copies(e):
        return [pltpu.make_async_copy(w_hbm.at[e, :, pl.ds(half * D_FF, D_FF)], w_f32.at[half], sem.at[half])
                for half in range(2)]

    def cast():
        w_bf[...] = w_f32[...].astype(BF16)

    _stream_expert_weights(be_ref, nxt_ref, copies, cast)

    @pl.when(i < n_used)
    def _():
        for cp in row_copies(tok_ref, slot):
            cp.wait()
        x = rows_ref[slot].astype(BF16)
        g = jnp.dot(x, w_bf[0], preferred_element_type=F32) + b_ref[0, :, :D_FF]
        u = jnp.dot(x, w_bf[1], preferred_element_type=F32) + b_ref[0, :, D_FF:]
        g = jnp.minimum(g, SWIGLU_LIMIT)
        u = jnp.clip(u, -SWIGLU_LIMIT, SWIGLU_LIMIT)
        o_ref[...] = ((u + 1.0) * (g * _sigmoid(g * SWIGLU_ALPHA))).astype(o_ref.dtype)

    @pl.when(i >= n_used)
    def _():
        o_ref[...] = jnp.zeros_like(o_ref)


def moe_up(x, slot_tok, block_e, next_e, n_used, w_gate_up, b_gate_up):
    n_blocks = slot_tok.shape[0] // MOE_BLOCK
    tok3 = slot_tok.reshape(n_blocks, 1, MOE_BLOCK)
    grid_spec = pltpu.PrefetchScalarGridSpec(
        num_scalar_prefetch=3,
        grid=(n_blocks,),
        in_specs=[pl.BlockSpec((1, 1, MOE_BLOCK), lambda i, be, nx, nb: (i, 0, 0), memory_space=pltpu.SMEM),
                  pl.BlockSpec((1, 1, MOE_BLOCK), lambda i, be, nx, nb: (jnp.minimum(i + 1, n_blocks - 1), 0, 0),
                               memory_space=pltpu.SMEM),
                  pl.BlockSpec(memory_space=pl.ANY),
                  pl.BlockSpec(memory_space=pl.ANY),
                  pl.BlockSpec((1, 1, 2 * D_FF), lambda i, be, nx, nb: (be[i], 0, 0))],
        out_specs=pl.BlockSpec((MOE_BLOCK, D_FF), lambda i, be, nx, nb: (i, 0)),
        scratch_shapes=[pltpu.VMEM((2, D_MODEL, D_FF), F32), pltpu.VMEM((2, D_MODEL, D_FF), BF16),
                        pltpu.SemaphoreType.DMA((2,)),
                        pltpu.VMEM((2, MOE_BLOCK, D_MODEL), x.dtype), pltpu.SemaphoreType.DMA((2,))],
    )
    return pl.pallas_call(
        _moe_up_kernel,
        out_shape=jax.ShapeDtypeStruct((n_blocks * MOE_BLOCK, D_FF), BF16),
        grid_spec=grid_spec,
        compiler_params=_cparams("arbitrary"),
        name="moe_up",
    )(block_e, next_e, n_used, tok3, tok3, x, w_gate_up, b_gate_up.reshape(N_EXPERTS, 1, 2 * D_FF))


def _moe_down_kernel(be_ref, nxt_ref, nb_ref, h_ref, w_hbm, bd_ref, o_ref, w_f32, w_bf, sem):
    i = pl.program_id(0)

    def copies(e):
        return [pltpu.make_async_copy(w_hbm.at[e], w_f32, sem)]

    def cast():
        w_bf[...] = w_f32[...].astype(BF16)

    _stream_expert_weights(be_ref, nxt_ref, copies, cast)

    @pl.when(i < nb_ref[0])
    def _():
        o_ref[...] = jnp.dot(h_ref[...], w_bf[...], preferred_element_type=F32) + bd_ref[0]

    @pl.when(i >= nb_ref[0])
    def _():
        o_ref[...] = jnp.zeros_like(o_ref)


def moe_down(hb, block_e, next_e, n_used, w_down, b_down):
    n_blocks = hb.shape[0] // MOE_BLOCK
    grid_spec = pltpu.PrefetchScalarGridSpec(
        num_scalar_prefetch=3,
        grid=(n_blocks,),
        in_specs=[pl.BlockSpec((MOE_BLOCK, D_FF), lambda i, be, nx, nb: (i, 0)),
                  pl.BlockSpec(memory_space=pl.ANY),
                  pl.BlockSpec((1, 1, D_MODEL), lambda i, be, nx, nb: (be[i], 0, 0))],
        out_specs=pl.BlockSpec((MOE_BLOCK, D_MODEL), lambda i, be, nx, nb: (i, 0)),
        scratch_shapes=[pltpu.VMEM((D_FF, D_MODEL), F32), pltpu.VMEM((D_FF, D_MODEL), BF16),
                        pltpu.SemaphoreType.DMA(())],
    )
    return pl.pallas_call(
        _moe_down_kernel,
        out_shape=jax.ShapeDtypeStruct((n_blocks * MOE_BLOCK, D_MODEL), F32),
        grid_spec=grid_spec,
        compiler_params=_cparams("arbitrary"),
        name="moe_down",
    )(block_e, next_e, n_used, hb, w_down, b_down.reshape(N_EXPERTS, 1, D_MODEL))


def _combine_kernel(slot_ref, slot_next_ref, row0_ref, gate_ref, lnf_ref, yb_hbm, h_hbm, o_ref, rows_ref, h_vmem,
                    sem, hsem):
    i = pl.program_id(0)
    n_tiles = pl.num_programs(0)
    tm = o_ref.shape[-2]
    slot = lax.rem(i, 2)

    def copies(idx_ref, tile, buf):
        row0 = pl.multiple_of(row0_ref[tile], SUBLANES)
        cps = [pltpu.make_async_copy(h_hbm.at[pl.ds(row0, tm), :], h_vmem.at[buf], hsem.at[buf])]
        for n in range(TOP_K * tm):
            cps.append(pltpu.make_async_copy(yb_hbm.at[pl.ds(idx_ref[0, 0, n], 1), :],
                                             rows_ref.at[buf, n // tm, pl.ds(n % tm, 1), :], sem.at[buf]))
        return cps

    @pl.when(i == 0)
    def _():
        for cp in copies(slot_ref, 0, 0):
            cp.start()

    @pl.when(i + 1 < n_tiles)
    def _():
        for cp in copies(slot_next_ref, jnp.minimum(i + 1, n_tiles - 1), 1 - slot):
            cp.start()

    for cp in copies(slot_ref, i, slot):
        cp.wait()
    gate = gate_ref[...]
    acc = h_vmem[slot]
    for k in range(TOP_K):
        acc = acc + rows_ref[slot, k] * gate[:, k:k + 1]
    ms = jnp.mean(acc * acc, axis=-1, keepdims=True)
    o_ref[...] = (acc * lax.rsqrt(ms + RMS_EPS) * lnf_ref[...]).reshape(o_ref.shape)


def moe_combine(yb, h1, slot_of_pair, gates, row0, lnf_g, out_shape, out_index_map, tm):
    n_tiles = row0.shape[0]
    grid_spec = pltpu.PrefetchScalarGridSpec(
        num_scalar_prefetch=0,
        grid=(n_tiles,),
        in_specs=[pl.BlockSpec((1, 1, TOP_K * tm), lambda i: (i, 0, 0), memory_space=pltpu.SMEM),
                  pl.BlockSpec((1, 1, TOP_K * tm), lambda i: (jnp.minimum(i + 1, n_tiles - 1), 0, 0),
                               memory_space=pltpu.SMEM),
                  pl.BlockSpec(memory_space=pltpu.SMEM),
                  pl.BlockSpec((tm, LANES), lambda i: (i, 0)),
                  pl.BlockSpec((1, D_MODEL), lambda i: (0, 0)),
                  pl.BlockSpec(memory_space=pl.ANY),
                  pl.BlockSpec(memory_space=pl.ANY)],
        out_specs=pl.BlockSpec(out_shape[0], out_index_map),
        scratch_shapes=[pltpu.VMEM((2, TOP_K, tm, D_MODEL), F32), pltpu.VMEM((2, tm, D_MODEL), F32),
                        pltpu.SemaphoreType.DMA((2,)), pltpu.SemaphoreType.DMA((2,))],
    )
    return pl.pallas_call(
        _combine_kernel,
        out_shape=jax.ShapeDtypeStruct(out_shape[1], F32),
        grid_spec=grid_spec,
        compiler_params=_cparams("arbitrary"),
        name="moe_combine",
    )(slot_of_pair, slot_of_pair, row0, gates, lnf_g.reshape(1, D_MODEL), yb, h1)


PROMPT_PAD_T = 2176
ROW_TILE = 1104
PROJ_ROW_TILE = 2208


def kernel(x_prompt, x_sample, state_ssm_conv, state_ssm, state_rwkv_shift, state_rwkv_wkv, meta_tokens, ln1_g, w_in, ssm_conv_w, ssm_conv_b, ssm_dt_bias, ssm_A_log, ssm_D, ssm_norm_g, rwkv_mu, rwkv_w0, rwkv_w2, rwkv_a0, rwkv_a2, rwkv_k_k, rwkv_k_a, rwkv_r_k, rwkv_gn_g, rwkv_gn_b, w_out, ln2_g, router_w, router_b, w_gate_up, b_gate_up, w_down, b_down, lnf_g):
    bp, seq, d = x_prompt.shape
    bs = x_sample.shape[0]
    t_prompt = N_META + seq
    tail = jnp.zeros((PROMPT_PAD_T - t_prompt, d), F32)
    pieces = []
    for b in range(bp):
        pieces += [meta_tokens, x_prompt[b], tail]
    h_rows = jnp.concatenate(pieces + [x_sample.reshape(bs, d)], axis=0)
    n_prow = bp * PROMPT_PAD_T

    l = 0
    xn = rmsnorm_rows(h_rows, ln1_g[l], BF16, ROW_TILE)
    w = w_in[l]
    o_xbc = SSM_D_INNER
    o_dt = o_xbc + SSM_CONV_DIM
    o_rw = o_dt + SSM_N_HEADS
    o_g = o_rw + RWKV_SHIFT_DIM
    w_bf = w.astype(BF16)
    w_rw = w_bf[:, o_rw:o_g]
    w_g = w_bf[:, o_g:]
    z = matmul_bf16(xn, w_bf, PROJ_ROW_TILE, 1024, "proj_z", col0=0, n=SSM_D_INNER)
    xbc = matmul_bf16(xn, w_bf, PROJ_ROW_TILE, 1024, "proj_xbc", col0=o_xbc, n=SSM_CONV_DIM)
    dtr = matmul_bf16(xn, w_bf, PROJ_ROW_TILE, LANES, "proj_dt", col0=o_dt, n=LANES)

    ssm_args = (ssm_conv_w[l], ssm_conv_b[l], ssm_dt_bias[l], ssm_A_log[l], ssm_D[l], ssm_norm_g[l])
    conv0_p = jnp.zeros((bp, SUBLANES, SSM_CONV_DIM), F32)
    s0_p = jnp.zeros((bp, SSM_N_HEADS, SSM_HEAD_DIM, SSM_D_STATE), F32)
    ya_p, ssm_p = ssd_branch(xbc, z, dtr, conv0_p, s0_p, *ssm_args, t=PROMPT_PAD_T, q=SSM_CHUNK,
                             t_valid=t_prompt, name="ssd_prompt")
    ya_s, ssm_s = ssd_step(xbc, z, dtr, n_prow, jnp.swapaxes(state_ssm_conv[l], 0, 1), state_ssm[l], *ssm_args,
                           bt=SUBLANES, name="ssd_sample")
    rw = matmul_bf16(xn, w_rw, PROJ_ROW_TILE, 896, "proj_rw")
    rwkv_args = (rwkv_mu[l], rwkv_w0[l], rwkv_w2[l], rwkv_a0[l], rwkv_a2[l], rwkv_k_k[l], rwkv_k_a[l],
                 rwkv_r_k[l], rwkv_gn_g[l], rwkv_gn_b[l])
    shift0_p = jnp.zeros((bp, SUBLANES, RWKV_SHIFT_DIM), F32)
    wkv0_p = jnp.zeros((bp, RWKV_N_HEADS, RWKV_HEAD, RWKV_HEAD), F32)
    yb_p, wkv_p = rwkv_branch(rw, shift0_p, wkv0_p, *rwkv_args, t=PROMPT_PAD_T, cq=RWKV_CHUNK, t_valid=t_prompt,
                              name="rwkv_prompt")
    yb_s, wkv_t = rwkv_step(rw, n_prow, state_rwkv_shift[l], jnp.transpose(state_rwkv_wkv[l], (1, 2, 3, 0)),
                            *rwkv_args, name="rwkv_sample")
    wkv_s = jnp.transpose(wkv_t, (3, 0, 1, 2))
    gates = matmul_bf16(xn, w_g, PROJ_ROW_TILE, 1024, "proj_gates")
    h1, xn2, top_idx, top_gate = mix_and_route(
        gates, ya_p, yb_p, ya_s, yb_s, h_rows,
        w_out[l].astype(BF16), ln2_g[l], router_w[l], router_b[l], MOE_BLOCK)

    n_rows = h_rows.shape[0]
    row_id = jnp.arange(n_rows, dtype=jnp.int32)
    valid = jnp.logical_or(row_id >= n_prow, row_id % PROMPT_PAD_T < t_prompt)
    yb, slot_of_pair = moe_expert_rows(xn2, top_idx, valid, (bp * t_prompt + bs) * TOP_K, w_gate_up[l],
                                       b_gate_up[l], w_down[l], b_down[l])

    def combine(row0, out_block, out_full, out_map):
        rows = (row0[:, None] + jnp.arange(MOE_BLOCK, dtype=jnp.int32)[None, :])
        slots = jnp.transpose(slot_of_pair[rows], (0, 2, 1)).reshape(row0.shape[0], 1, TOP_K * MOE_BLOCK)
        return moe_combine(yb, h1, slots, top_gate[rows.reshape(-1)], row0, lnf_g, (out_block, out_full),
                           out_map, MOE_BLOCK)

    tiles_per_seq = seq // MOE_BLOCK
    tile_id = jnp.arange(bp * tiles_per_seq, dtype=jnp.int32)
    row0_p = (tile_id // tiles_per_seq) * PROMPT_PAD_T + N_META + (tile_id % tiles_per_seq) * MOE_BLOCK
    y_prompt = combine(row0_p, (1, MOE_BLOCK, d), (bp, seq, d),
                       lambda i: (i // tiles_per_seq, i % tiles_per_seq, 0))
    y_sample = combine(jnp.full((1,), n_prow, jnp.int32), (MOE_BLOCK, d), (bs, d), lambda i: (i, 0))

    last = [b * PROMPT_PAD_T + t_prompt - 1 for b in range(bp)]
    prompt_conv = jnp.stack([xbc[r - (SSM_CONV - 2):r + 1] for r in last])
    prompt_shift = jnp.stack([rw[r] for r in last])
    sample_conv = jnp.concatenate([state_ssm_conv[l][:, 1:], xbc[n_prow:, None]], axis=1)
    return (y_prompt, y_sample.reshape(bs, 1, d), prompt_conv[None], ssm_p[None], prompt_shift[None], wkv_p[None],
            sample_conv[None], ssm_s[None], rw[n_prow:][None], wkv_s[None])


def moe_expert_rows(xn2, top_idx, valid, n_valid_pairs, w_gate_up, b_gate_up, w_down, b_down):
    n_rows = xn2.shape[0]
    experts = jnp.arange(N_EXPERTS, dtype=jnp.int32)
    key = jnp.where(valid[:, None], top_idx[:, :TOP_K], N_EXPERTS).reshape(-1)
    n_pairs = key.shape[0]
    n_blocks = -(-n_valid_pairs // MOE_BLOCK) + N_EXPERTS
    n_slots = n_blocks * MOE_BLOCK
    counts = jnp.sum((key[:, None] == experts[None, :]).astype(jnp.int32), axis=0)
    padded = (counts + MOE_BLOCK - 1) // MOE_BLOCK * MOE_BLOCK
    ends = jnp.cumsum(padded)
    fill_e = jnp.repeat(experts, MOE_BLOCK)
    fill_r = jnp.tile(jnp.arange(MOE_BLOCK, dtype=jnp.int32), N_EXPERTS)
    fill_key = jnp.where(fill_r < jnp.repeat(padded - counts, MOE_BLOCK), fill_e, N_EXPERTS + 1)
    keys = jnp.concatenate([key, fill_key])
    iota = jnp.arange(keys.shape[0], dtype=jnp.int32)
    _, order = lax.sort((keys, iota), num_keys=1, is_stable=True)
    _, slot_of = lax.sort((order, iota), num_keys=1, is_stable=True)
    slot_tok = jnp.where(order[:n_slots] < n_pairs, order[:n_slots] // TOP_K, 0).astype(jnp.int32)
    slot_of_pair = jnp.minimum(slot_of[:n_pairs], n_slots - 1).reshape(n_rows, TOP_K)
    block_start = jnp.arange(n_blocks, dtype=jnp.int32) * MOE_BLOCK
    block_e = jnp.minimum(jnp.sum((ends[None, :] <= block_start[:, None]).astype(jnp.int32), axis=1),
                          N_EXPERTS - 1).astype(jnp.int32)
    n_used = (ends[-1:] // MOE_BLOCK).astype(jnp.int32)
    next_e = next_run_expert(block_e)

    hb = moe_up(xn2, slot_tok, block_e, next_e, n_used, w_gate_up, b_gate_up)
    yb = moe_down(hb, block_e, next_e, n_used, w_down, b_down)
    return yb, slot_of_pair
```

```python
import functools
import math

import jax
import jax.numpy as jnp
from jax import lax
from jax.experimental import pallas as pl
from jax.experimental.pallas import tpu as pltpu

F32 = jnp.float32
BF16 = jnp.bfloat16

V7X_VMEM_LIMIT_BYTES = 60000 * 1024
LANES = 128
SUBLANES = 8

D_MODEL = 2048
N_META = 16
RMS_EPS = 1e-6
SSM_D_INNER = 2048
SSM_HEAD_DIM = 64
SSM_N_HEADS = 32
SSM_N_GROUPS = 4
SSM_HPG = 8
SSM_D_STATE = 128
SSM_CONV = 4
SSM_CHUNK = 128
SSM_CONV_DIM = SSM_D_INNER + 2 * SSM_N_GROUPS * SSM_D_STATE
SSM_NORM_EPS = 1e-5
RWKV_DIM = 2048
RWKV_HEAD = 64
RWKV_N_HEADS = 32
RWKV_LORA = 64
RWKV_GN_EPS = 64e-5
RWKV_SHIFT_DIM = 3 * RWKV_DIM + 2 * RWKV_LORA
RWKV_CHUNK = 64
N_EXPERTS = 32
TOP_K = 4
D_FF = 2048
SWIGLU_LIMIT = 7.0
SWIGLU_ALPHA = 1.702
MOE_BLOCK = 128


def _cparams(*sem):
    return pltpu.CompilerParams(dimension_semantics=sem, vmem_limit_bytes=V7X_VMEM_LIMIT_BYTES)


def _split3(x):
    hi = x.astype(BF16)
    r1 = x - hi.astype(F32)
    mid = r1.astype(BF16)
    lo = (r1 - mid.astype(F32)).astype(BF16)
    return hi, mid, lo


def _dot(a, b, dims=None):
    a = a.astype(BF16)
    b = b.astype(BF16)
    if dims is None:
        return jnp.dot(a, b, preferred_element_type=F32)
    return lax.dot_general(a, b, (dims, ((), ())), preferred_element_type=F32)


def _dot_exact_lhs(x, m01):
    hi, mid, lo = _split3(x)
    m = m01.astype(BF16)
    return (jnp.dot(hi, m, preferred_element_type=F32) + jnp.dot(mid, m, preferred_element_type=F32)
            + jnp.dot(lo, m, preferred_element_type=F32))


def _dot_split_lhs(x, m01, terms):
    m = m01.astype(BF16)
    out = None
    for part in _split3(x)[:terms]:
        d = jnp.dot(part, m, preferred_element_type=F32)
        out = d if out is None else out + d
    return out


def _dot_f32(a, b, terms=6):
    ah, am, al = _split3(a)
    bh, bm, bl = _split3(b)
    lhs = jnp.concatenate([ah, ah, am, ah, al, am][:terms], axis=1)
    rhs = jnp.concatenate([bh, bm, bh, bl, bh, bm][:terms], axis=0)
    return jnp.dot(lhs, rhs, preferred_element_type=F32)


def _softplus(x):
    return jnp.maximum(x, 0.0) + jnp.log(1.0 + jnp.exp(-jnp.abs(x)))


def _sigmoid(x):
    return 1.0 / (1.0 + jnp.exp(-x))


def _silu(x):
    return x * _sigmoid(x)


def _rmsnorm_kernel(x_ref, g_ref, o_ref, *, eps):
    x = x_ref[...]
    ms = jnp.mean(x * x, axis=-1, keepdims=True)
    o_ref[...] = (x * lax.rsqrt(ms + eps) * g_ref[...]).astype(o_ref.dtype)


def rmsnorm_rows(x, g, out_dtype, tm):
    rows, d = x.shape
    return pl.pallas_call(
        functools.partial(_rmsnorm_kernel, eps=RMS_EPS),
        out_shape=jax.ShapeDtypeStruct((rows, d), out_dtype),
        grid=(rows // tm,),
        in_specs=[pl.BlockSpec((tm, d), lambda i: (i, 0)), pl.BlockSpec((1, d), lambda i: (0, 0))],
        out_specs=pl.BlockSpec((tm, d), lambda i: (i, 0)),
        compiler_params=_cparams("parallel"),
        name="rmsnorm_rows",
    )(x, g.reshape(1, d))


def _matmul_kernel(x_ref, w_ref, o_ref):
    o_ref[...] = jnp.dot(x_ref[...], w_ref[...], preferred_element_type=F32)


def matmul_bf16(x, w, tm, tn, name):
    rows, k = x.shape
    n = w.shape[1]
    return pl.pallas_call(
        _matmul_kernel,
        out_shape=jax.ShapeDtypeStruct((rows, n), F32),
        grid=(n // tn, rows // tm),
        in_specs=[pl.BlockSpec((tm, k), lambda j, i: (i, 0)), pl.BlockSpec((k, tn), lambda j, i: (0, j))],
        out_specs=pl.BlockSpec((tm, tn), lambda j, i: (i, j)),
        compiler_params=_cparams("parallel", "parallel"),
        name=name,
    )(x, w)


def _ssd_kernel(xbc_ref, z_ref, dtr_ref, conv0_ref, s0_ref, convw_ref, convb_ref, dtb_ref, alog_ref,
                dskip_ref, ng_ref, expand_ref, y_ref, s_ref, buf_ref, *, q, t_valid):
    c = pl.program_id(1)
    n_pad = buf_ref.shape[0] - q

    @pl.when(c == 0)
    def _():
        buf_ref[0:n_pad, :] = conv0_ref[0]
        s_ref[0] = s0_ref[0]

    buf_ref[n_pad:n_pad + q, :] = xbc_ref[...]
    acc = convb_ref[...] + convw_ref[SSM_CONV - 1:SSM_CONV, :] * buf_ref[n_pad:n_pad + q, :]
    for k in range(SSM_CONV - 1):
        off = n_pad - (SSM_CONV - 1) + k
        acc = acc + convw_ref[k:k + 1, :] * buf_ref[off:off + q, :]
    buf_ref[0:n_pad, :] = buf_ref[q:q + n_pad, :]
    xbc = _silu(acc)
    gn = SSM_N_GROUPS * SSM_D_STATE
    xs = xbc[:, :SSM_D_INNER]
    bm = xbc[:, SSM_D_INNER:SSM_D_INNER + gn]
    cm = xbc[:, SSM_D_INNER + gn:]

    row = lax.broadcasted_iota(jnp.int32, (q, LANES), 0) + c * q
    lane = lax.broadcasted_iota(jnp.int32, (q, LANES), 1)
    live = jnp.logical_and(row < t_valid, lane < SSM_N_HEADS)
    dt = jnp.where(live, _softplus(dtr_ref[...] + dtb_ref[...]), 0.0)
    da = dt * (-jnp.exp(alog_ref[...]))

    ti = lax.broadcasted_iota(jnp.int32, (q, q), 0)
    si = lax.broadcasted_iota(jnp.int32, (q, q), 1)
    causal = si <= ti
    tril = causal.astype(BF16)
    dh, dm, dl = _split3(da)
    cs = (jnp.dot(tril, dh, preferred_element_type=F32) + jnp.dot(tril, dm, preferred_element_type=F32)
          + jnp.dot(tril, dl, preferred_element_type=F32))
    eye = (lax.broadcasted_iota(jnp.int32, (LANES, LANES), 0)
           == lax.broadcasted_iota(jnp.int32, (LANES, LANES), 1)).astype(BF16)
    ch, cmid, cl = _split3(cs)
    nt = (((1,), (1,)), ((), ()))
    cs_t = (lax.dot_general(eye, ch, nt, preferred_element_type=F32)
            + lax.dot_general(eye, cmid, nt, preferred_element_type=F32)
            + lax.dot_general(eye, cl, nt, preferred_element_type=F32))

    expand = expand_ref[...]
    dt_x = _dot_exact_lhs(dt, expand)
    cs_x = _dot_exact_lhs(cs, expand)
    xdt = xs * dt_x
    ecs = jnp.exp(cs_x)
    dte = jnp.exp(cs_x[q - 1:q, :] - cs_x)

    lane_in_pair = lax.broadcasted_iota(jnp.int32, (q, LANES), 1)
    lo_half = lane_in_pair < SSM_HEAD_DIM

    y_groups = []
    for g in range(SSM_N_GROUPS):
        b_g = bm[:, g * SSM_D_STATE:(g + 1) * SSM_D_STATE]
        c_g = cm[:, g * SSM_D_STATE:(g + 1) * SSM_D_STATE]
        cb = _dot(c_g, b_g, ((1,), (1,)))
        gsl = slice(g * SSM_HPG * SSM_HEAD_DIM, (g + 1) * SSM_HPG * SSM_HEAD_DIM)
        s_g = s_ref[0, g * SSM_HPG:(g + 1) * SSM_HPG].reshape(SSM_HPG * SSM_HEAD_DIM, SSM_D_STATE)
        y_off = _dot(c_g, s_g, ((1,), (1,))) * ecs[:, gsl]
        slabs = []
        for m in range(SSM_HPG // 2):
            lsl = slice(gsl.start + m * LANES, gsl.start + (m + 1) * LANES)
            x_pair = xdt[:, lsl]
            acc_pair = None
            for half in range(2):
                h = g * SSM_HPG + 2 * m + half
                seg = cs[:, h:h + 1] - cs_t[h:h + 1, :]
                lmat = jnp.where(causal, jnp.exp(jnp.where(causal, seg, 0.0)), 0.0)
                x_half = jnp.where(lo_half if half == 0 else jnp.logical_not(lo_half), x_pair, 0.0)
                contrib = _dot(cb * lmat, x_half)
                acc_pair = contrib if acc_pair is None else acc_pair + contrib
            slabs.append(acc_pair)
        y_groups.append(jnp.concatenate(slabs, axis=1) + y_off)

        xw = xdt[:, gsl] * dte[:, gsl]
        upd = _dot(xw, b_g, ((0,), (0,)))
        for j in range(SSM_HPG):
            h = g * SSM_HPG + j
            dec = jnp.exp(cs_t[h:h + 1, q - 1:q])
            s_ref[0, h] = s_ref[0, h] * dec + upd[j * SSM_HEAD_DIM:(j + 1) * SSM_HEAD_DIM, :]

    y = jnp.concatenate(y_groups, axis=1) + dskip_ref[...] * xs
    y = y * _silu(z_ref[...])
    outs = []
    gw = SSM_D_INNER // SSM_N_GROUPS
    for g in range(SSM_N_GROUPS):
        yg = y[:, g * gw:(g + 1) * gw]
        outs.append(yg * lax.rsqrt(jnp.mean(yg * yg, axis=-1, keepdims=True) + SSM_NORM_EPS))
    y_ref[...] = jnp.concatenate(outs, axis=1) * ng_ref[...]


def ssd_branch(xbc, z, dtr, conv0, s0, conv_w, conv_b, dt_bias, a_log, d_skip, norm_g, *, t, q, t_valid, name):
    nb = s0.shape[0]
    cps = t // q
    n_pad = conv0.shape[1]
    pad_l = lambda v: jnp.pad(v.astype(F32), (0, LANES - v.shape[0])).reshape(1, LANES)
    expand = (jnp.arange(LANES)[:, None] == (jnp.arange(SSM_D_INNER) // SSM_HEAD_DIM)[None, :]).astype(BF16)
    row2 = lambda v: v.reshape(1, -1)
    full = lambda a: pl.BlockSpec(a.shape, lambda b, c: (0,) * a.ndim)
    consts = [conv_w, row2(conv_b), pad_l(dt_bias), pad_l(a_log), row2(jnp.repeat(d_skip, SSM_HEAD_DIM)),
              row2(norm_g), expand]
    kern = functools.partial(_ssd_kernel, q=q, t_valid=t_valid)
    return pl.pallas_call(
        kern,
        out_shape=(jax.ShapeDtypeStruct((nb * t, SSM_D_INNER), F32),
                   jax.ShapeDtypeStruct(s0.shape, F32)),
        grid=(nb, cps),
        in_specs=[pl.BlockSpec((q, SSM_CONV_DIM), lambda b, c: (b * cps + c, 0)),
                  pl.BlockSpec((q, SSM_D_INNER), lambda b, c: (b * cps + c, 0)),
                  pl.BlockSpec((q, LANES), lambda b, c: (b * cps + c, 0)),
                  pl.BlockSpec((1, n_pad, SSM_CONV_DIM), lambda b, c: (b, 0, 0)),
                  pl.BlockSpec((1,) + s0.shape[1:], lambda b, c: (b, 0, 0, 0))] + [full(a) for a in consts],
        out_specs=(pl.BlockSpec((q, SSM_D_INNER), lambda b, c: (b * cps + c, 0)),
                   pl.BlockSpec((1,) + s0.shape[1:], lambda b, c: (b, 0, 0, 0))),
        scratch_shapes=[pltpu.VMEM((q + n_pad, SSM_CONV_DIM), F32)],
        compiler_params=_cparams("parallel", "arbitrary"),
        name=name,
    )(xbc, z, dtr, conv0, s0, *consts)


def _ssd_step_kernel(xbc_ref, z_ref, dtr_ref, conv_ref, s0_ref, convw_ref, convb_ref, dtb_ref, alog_ref, dskip_ref,
                     ng_ref, expand_ref, y_ref, s_ref):
    bt = xbc_ref.shape[0]
    acc = convb_ref[...] + convw_ref[SSM_CONV - 1:SSM_CONV, :] * xbc_ref[...]
    for k in range(SSM_CONV - 1):
        acc = acc + convw_ref[k:k + 1, :] * conv_ref[k]
    xbc = _silu(acc)
    gn = SSM_N_GROUPS * SSM_D_STATE
    xs = xbc[:, :SSM_D_INNER]
    bm = xbc[:, SSM_D_INNER:SSM_D_INNER + gn]
    cm = xbc[:, SSM_D_INNER + gn:]
    lane = lax.broadcasted_iota(jnp.int32, (bt, LANES), 1)
    dt = jnp.where(lane < SSM_N_HEADS, _softplus(dtr_ref[...] + dtb_ref[...]), 0.0)
    dec = jnp.exp(dt * (-jnp.exp(alog_ref[...])))
    xdt = xs * _dot_exact_lhs(dt, expand_ref[...])
    hd = SSM_HPG * SSM_HEAD_DIM
    rowid = lax.broadcasted_iota(jnp.int32, (bt, hd), 0)
    tn = ((0,), (0,))
    y_acc = [jnp.zeros((bt, hd), F32) for _ in range(SSM_N_GROUPS)]
    for bb in range(bt):
        mine = rowid == bb
        for g in range(SSM_N_GROUPS):
            b_g = bm[:, g * SSM_D_STATE:(g + 1) * SSM_D_STATE]
            c_g = cm[:, g * SSM_D_STATE:(g + 1) * SSM_D_STATE]
            xh, xm, xl = _split3(jnp.where(mine, xdt[:, g * hd:(g + 1) * hd], 0.0))
            bh, bmid, bl = _split3(b_g)
            upd = _dot(jnp.concatenate([xh, xh, xm, xh, xl, xm], axis=0),
                       jnp.concatenate([bh, bmid, bh, bl, bh, bmid], axis=0), tn)
            heads = []
            for j in range(SSM_HPG):
                h = g * SSM_HPG + j
                s_new = s0_ref[bb, h] * dec[bb:bb + 1, h:h + 1] + upd[j * SSM_HEAD_DIM:(j + 1) * SSM_HEAD_DIM, :]
                s_ref[bb, h] = s_new
                heads.append(s_new)
            y_all = _dot(c_g, jnp.concatenate(heads, axis=0), ((1,), (1,)))
            y_acc[g] = jnp.where(mine, y_all, y_acc[g])
    y = jnp.concatenate(y_acc, axis=1) + dskip_ref[...] * xs
    y = y * _silu(z_ref[...])
    outs = []
    gw = SSM_D_INNER // SSM_N_GROUPS
    for g in range(SSM_N_GROUPS):
        yg = y[:, g * gw:(g + 1) * gw]
        outs.append(yg * lax.rsqrt(jnp.mean(yg * yg, axis=-1, keepdims=True) + SSM_NORM_EPS))
    y_ref[...] = jnp.concatenate(outs, axis=1) * ng_ref[...]


def ssd_step(xbc, z, dtr, row0, conv_prev, s0, conv_w, conv_b, dt_bias, a_log, d_skip, norm_g, *, bt, name):
    nb = s0.shape[0]
    pad_l = lambda v: jnp.pad(v.astype(F32), (0, LANES - v.shape[0])).reshape(1, LANES)
    expand = (jnp.arange(LANES)[:, None] == (jnp.arange(SSM_D_INNER) // SSM_HEAD_DIM)[None, :]).astype(BF16)
    row2 = lambda v: v.reshape(1, -1)
    full = lambda a: pl.BlockSpec(a.shape, lambda i: (0,) * a.ndim)
    consts = [conv_w, row2(conv_b), pad_l(dt_bias), pad_l(a_log), row2(jnp.repeat(d_skip, SSM_HEAD_DIM)),
              row2(norm_g), expand]
    rows = lambda n: pl.BlockSpec((bt, n), lambda i: (row0 // bt + i, 0))
    state_spec = pl.BlockSpec((bt,) + s0.shape[1:], lambda i: (i, 0, 0, 0))
    return pl.pallas_call(
        _ssd_step_kernel,
        out_shape=(jax.ShapeDtypeStruct((nb, SSM_D_INNER), F32), jax.ShapeDtypeStruct(s0.shape, F32)),
        grid=(nb // bt,),
        in_specs=[rows(SSM_CONV_DIM), rows(SSM_D_INNER), rows(LANES),
                  pl.BlockSpec((SSM_CONV - 1, bt, SSM_CONV_DIM), lambda i: (0, i, 0)),
                  state_spec] + [full(a) for a in consts],
        out_specs=(pl.BlockSpec((bt, SSM_D_INNER), lambda i: (i, 0)), state_spec),
        compiler_params=_cparams("parallel"),
        name=name,
    )(xbc, z, dtr, conv_prev, s0, *consts)


RWKV_GROUP_HEADS = 4
RWKV_GROUP_LANES = RWKV_GROUP_HEADS * RWKV_HEAD


def _rep_rows(x, n):
    return jnp.concatenate([x] * n, axis=0)


def _head_mask(rows_per_head, cols_per_head, n_rows, n_cols):
    rr = lax.broadcasted_iota(jnp.int32, (n_rows, n_cols), 0) // rows_per_head
    cc = lax.broadcasted_iota(jnp.int32, (n_rows, n_cols), 1) // cols_per_head
    return rr == cc


def _rwkv_prep(cur, prev, live, mu_ref, w0_ref, w2_ref, a0_ref, a2_ref, kk_ref, ka_ref, rk_ref, hred_ref,
               hexp_ref):
    rows = cur.shape[0]
    mixed = cur + (prev - cur) * mu_ref[...]
    r = mixed[:, 0:RWKV_DIM]
    k = mixed[:, RWKV_DIM:2 * RWKV_DIM]
    v = mixed[:, 2 * RWKV_DIM:3 * RWKV_DIM]
    lora = mixed[:, 3 * RWKV_DIM:]
    lane = lax.broadcasted_iota(jnp.int32, (rows, LANES), 1)
    wl = jnp.where(lane < RWKV_LORA, jnp.tanh(lora), 0.0)
    al = jnp.where(lane >= RWKV_LORA, lora, 0.0)
    logw = -math.exp(-0.5) * _sigmoid(w0_ref[...] + _dot_f32(wl, w2_ref[...], terms=3))
    a = _sigmoid(a0_ref[...] + _dot_f32(al, a2_ref[...], terms=3))
    kk_raw = k * kk_ref[...]
    k2 = k * (1.0 + (a - 1.0) * ka_ref[...])
    sums = _dot_split_lhs(jnp.concatenate([kk_raw * kk_raw, r * k2 * rk_ref[...]], axis=0), hred_ref[...], 2)
    inv_n = 1.0 / jnp.maximum(jnp.sqrt(sums[:rows]), 1e-12)
    kk = kk_raw * _dot_split_lhs(inv_n, hexp_ref[...], 2)
    if live is not None:
        logw = jnp.where(live, logw, 0.0)
        kk = jnp.where(live, kk, 0.0)
        k2 = jnp.where(live, k2, 0.0)
        v = jnp.where(live, v, 0.0)
    return r, k2, v, kk, a, logw, sums[rows:]


def _rwkv_post(y, v, bonus_sum, gng_ref, gnb_ref, hred_ref, hexp_ref):
    rows = y.shape[0]
    inv_h = 1.0 / RWKV_HEAD
    sums = _dot_split_lhs(jnp.concatenate([y, y * y], axis=0), hred_ref[...], 3) * inv_h
    mean = sums[:rows]
    var = jnp.maximum(sums[rows:] - mean * mean, 0.0)
    per_head = jnp.concatenate([mean, lax.rsqrt(var + RWKV_GN_EPS), bonus_sum], axis=0)
    wide = _dot_split_lhs(per_head, hexp_ref[...], 2)
    return ((y - wide[:rows]) * wide[rows:2 * rows] * gng_ref[...] + gnb_ref[...]
            + wide[2 * rows:] * v)


def _state_to_blockdiag(stack, m_state):
    return jnp.where(m_state, jnp.concatenate([stack] * RWKV_GROUP_HEADS, axis=1), 0.0)


def _blockdiag_to_state(s_bd):
    stack = s_bd[:, 0:RWKV_HEAD]
    for hh in range(1, RWKV_GROUP_HEADS):
        stack = stack + s_bd[:, hh * RWKV_HEAD:(hh + 1) * RWKV_HEAD]
    return stack


def _rwkv_kernel(rw_ref, shift0_ref, wkv0_ref, mu_ref, w0_ref, w2_ref, a0_ref, a2_ref, kk_ref, ka_ref, rk_ref,
                 gng_ref, gnb_ref, hred_ref, hexp_ref,
                 y_ref, wkv_ref, buf_ref, s_ref, *, cq, t_valid):
    c = pl.program_id(1)
    n_chunks = pl.num_programs(1)
    n_pad = buf_ref.shape[0] - cq
    gl = RWKV_GROUP_LANES
    gh = RWKV_GROUP_HEADS
    n_groups = RWKV_DIM // gl
    nt = ((1,), (1,))
    tn = ((0,), (0,))

    m_state = _head_mask(RWKV_HEAD, RWKV_HEAD, gl, gl)
    m_exp = _head_mask(cq, RWKV_HEAD, gh * cq, gl)
    m_nbd = _head_mask(cq, cq, gh * cq, gh * cq)

    @pl.when(c == 0)
    def _():
        buf_ref[0:n_pad, :] = shift0_ref[0]
        for g in range(n_groups):
            s_ref[g] = _state_to_blockdiag(wkv0_ref[0, g * gl:(g + 1) * gl, :], m_state)

    buf_ref[n_pad:n_pad + cq, :] = rw_ref[...]
    cur = buf_ref[n_pad:n_pad + cq, :]
    prev = buf_ref[n_pad - 1:n_pad - 1 + cq, :]
    buf_ref[0:n_pad, :] = buf_ref[cq:cq + n_pad, :]
    row = lax.broadcasted_iota(jnp.int32, (cq, 1), 0) + c * cq
    r, k2, v, kk, a, logw, bonus_sum = _rwkv_prep(cur, prev, row < t_valid, mu_ref, w0_ref, w2_ref, a0_ref, a2_ref,
                                                  kk_ref, ka_ref, rk_ref, hred_ref, hexp_ref)

    ti = lax.broadcasted_iota(jnp.int32, (cq, cq), 0)
    si = lax.broadcasted_iota(jnp.int32, (cq, cq), 1)
    tril = (si <= ti).astype(BF16)
    lh, lm, ll = _split3(logw)
    cl = (jnp.dot(tril, lh, preferred_element_type=F32) + jnp.dot(tril, lm, preferred_element_type=F32)
          + jnp.dot(tril, ll, preferred_element_type=F32))
    p_incl = jnp.exp(cl)
    p_inv = jnp.exp(-cl)
    a_t = jnp.exp(cl - logw) * (-kk)
    r_t = p_incl * r
    b_t = p_inv * (kk * a)
    k_t = p_inv * k2
    p_end = p_incl[cq - 1:cq, :]

    tt = lax.broadcasted_iota(jnp.int32, (cq, gh * cq), 0)
    ss_i = lax.broadcasted_iota(jnp.int32, (cq, gh * cq), 1) % cq
    strict = ss_i < tt
    incl = ss_i <= tt

    def expand_rows(x):
        return jnp.where(m_exp, _rep_rows(x, gh), 0.0)

    groups = range(n_groups)
    sls = [slice(g * gl, (g + 1) * gl) for g in groups]
    lhs_ar = [jnp.concatenate([a_t[:, sl], r_t[:, sl]], axis=0) for sl in sls]
    v_exp = [expand_rows(v[:, sl]) for sl in sls]
    s_bd = [s_ref[g] for g in groups]
    hs = gh * cq
    abk = [_dot(lhs_ar[g], jnp.concatenate([expand_rows(b_t[:, sls[g]]), expand_rows(k_t[:, sls[g]])], axis=0), nt)
           for g in groups]
    ab = [m[:, :hs] for m in abk]
    ak = [m[:, hs:] for m in abk]
    w_s = [_dot(lhs_ar[g], s_bd[g], nt) for g in groups]
    x = [w_s[g][:cq] + _dot(jnp.where(strict, ak[g][:cq], 0.0), v_exp[g]) for g in groups]
    n_cat = [jnp.where(strict, ab[g][:cq], 0.0) for g in groups]
    n_steps = int(math.log2(cq))
    for step in range(n_steps):
        x = [x[g] + _dot(n_cat[g], expand_rows(x[g])) for g in groups]
        if step + 1 < n_steps:
            n_cat = [_dot(n_cat[g], jnp.where(m_nbd, _rep_rows(n_cat[g], gh), 0.0)) for g in groups]
    incl2 = jnp.concatenate([incl, incl], axis=1)
    y_groups = [w_s[g][cq:] + _dot(jnp.where(incl2, abk[g][cq:], 0.0),
                                   jnp.concatenate([expand_rows(x[g]), v_exp[g]], axis=0)) for g in groups]
    for g in groups:
        uv = jnp.concatenate([x[g], v[:, sls[g]]], axis=0)
        bk = jnp.concatenate([b_t[:, sls[g]], k_t[:, sls[g]]], axis=0)
        inc = _dot(uv, bk, tn)
        s_ref[g] = jnp.where(m_state, (s_bd[g] + inc) * p_end[:, sls[g]], 0.0)

    y = jnp.concatenate(y_groups, axis=1)
    y_ref[...] = _rwkv_post(y, v, bonus_sum, gng_ref, gnb_ref, hred_ref, hexp_ref)

    @pl.when(c == n_chunks - 1)
    def _():
        for g in range(n_groups):
            wkv_ref[0, g * gl:(g + 1) * gl, :] = _blockdiag_to_state(s_ref[g])


def _rwkv_consts(mu, w0, w2, a0, a2, k_k, k_a, r_k, gn_g, gn_b):
    row2 = lambda a: a.reshape(1, -1).astype(F32)
    zeros_l = jnp.zeros((RWKV_LORA, RWKV_DIM), F32)
    head_of_lane = jnp.arange(RWKV_DIM) // RWKV_HEAD
    hexp = (jnp.arange(LANES)[:, None] == head_of_lane[None, :]).astype(BF16)
    return [row2(mu), row2(w0), jnp.concatenate([w2, zeros_l], axis=0), row2(a0),
            jnp.concatenate([zeros_l, a2], axis=0), row2(k_k), row2(k_a), row2(r_k), row2(gn_g), row2(gn_b),
            hexp.T, hexp]


def _rwkv_step_prep_kernel(rw_ref, prev_ref, mu_ref, w0_ref, w2_ref, a0_ref, a2_ref, kk_ref, ka_ref, rk_ref,
                           gng_ref, gnb_ref, hred_ref, hexp_ref, vecs_ref, v_ref, bsum_ref):
    r, k2, v, kk, a, logw, bonus_sum = _rwkv_prep(rw_ref[...], prev_ref[...], None, mu_ref, w0_ref, w2_ref, a0_ref,
                                                  a2_ref, kk_ref, ka_ref, rk_ref, hred_ref, hexp_ref)
    for n, vec in enumerate((-kk, kk * a, jnp.exp(logw), k2, r, v)):
        vecs_ref[n] = vec.T
    v_ref[...] = v
    bsum_ref[...] = bonus_sum


def _rwkv_step_state_kernel(vecs_ref, s_ref, s_out_ref, yt_ref):
    a_t, b_t, w_t, k_t, r_t = (vecs_ref[n] for n in range(5))

    def body(i, carry):
        s_i = s_ref[0, i]
        u = jnp.sum(s_i * a_t, axis=0, keepdims=True)
        s_new = s_i * w_t + u * b_t + vecs_ref[5, pl.ds(i, 1), :] * k_t
        s_out_ref[0, i] = s_new
        yt_ref[pl.ds(i, 1), :] = jnp.sum(s_new * r_t, axis=0, keepdims=True)
        return carry

    lax.fori_loop(0, RWKV_HEAD, body, 0, unroll=4)


def _rwkv_step_post_kernel(yt_ref, v_ref, bsum_ref, gng_ref, gnb_ref, hred_ref, hexp_ref, y_ref):
    y_ref[...] = _rwkv_post(yt_ref[...].T, v_ref[...], bsum_ref[...], gng_ref, gnb_ref, hred_ref, hexp_ref)


def rwkv_step(rw, row0, prev, wkv_t, *params, name):
    nb = prev.shape[0]
    consts = _rwkv_consts(*params)
    full = lambda a: pl.BlockSpec(a.shape, lambda i: (0,) * a.ndim)
    n_vec = 6
    vecs, v_rows, bsum = pl.pallas_call(
        _rwkv_step_prep_kernel,
        out_shape=(jax.ShapeDtypeStruct((n_vec, RWKV_DIM, nb), F32), jax.ShapeDtypeStruct((nb, RWKV_DIM), F32),
                   jax.ShapeDtypeStruct((nb, LANES), F32)),
        grid=(1,),
        in_specs=[pl.BlockSpec((nb, RWKV_SHIFT_DIM), lambda i: (row0 // nb, 0)),
                  pl.BlockSpec((nb, RWKV_SHIFT_DIM), lambda i: (0, 0))] + [full(a) for a in consts],
        out_specs=(pl.BlockSpec((n_vec, RWKV_DIM, nb), lambda i: (0, 0, 0)),
                   pl.BlockSpec((nb, RWKV_DIM), lambda i: (0, 0)), pl.BlockSpec((nb, LANES), lambda i: (0, 0))),
        compiler_params=_cparams("arbitrary"),
        name=name + "_prep",
    )(rw, prev, *consts)
    state_spec = pl.BlockSpec((1, RWKV_HEAD, RWKV_HEAD, nb), lambda h: (h, 0, 0, 0))
    wkv_new, y_t = pl.pallas_call(
        _rwkv_step_state_kernel,
        out_shape=(jax.ShapeDtypeStruct(wkv_t.shape, F32), jax.ShapeDtypeStruct((RWKV_DIM, nb), F32)),
        grid=(RWKV_N_HEADS,),
        in_specs=[pl.BlockSpec((n_vec, RWKV_HEAD, nb), lambda h: (0, h, 0)), state_spec],
        out_specs=(state_spec, pl.BlockSpec((RWKV_HEAD, nb), lambda h: (h, 0))),
        compiler_params=_cparams("parallel"),
        name=name + "_state",
    )(vecs, wkv_t)
    post_consts = consts[8:]
    y = pl.pallas_call(
        _rwkv_step_post_kernel,
        out_shape=jax.ShapeDtypeStruct((nb, RWKV_DIM), F32),
        grid=(1,),
        in_specs=[pl.BlockSpec((RWKV_DIM, nb), lambda i: (0, 0)), pl.BlockSpec((nb, RWKV_DIM), lambda i: (0, 0)),
                  pl.BlockSpec((nb, LANES), lambda i: (0, 0))] + [full(a) for a in post_consts],
        out_specs=pl.BlockSpec((nb, RWKV_DIM), lambda i: (0, 0)),
        compiler_params=_cparams("arbitrary"),
        name=name + "_post",
    )(y_t, v_rows, bsum, *post_consts)
    return y, wkv_new


def rwkv_branch(rw, shift0, wkv0, *params, t, cq, t_valid, name):
    nb = wkv0.shape[0]
    cps = t // cq
    n_pad = shift0.shape[1]
    consts = _rwkv_consts(*params)
    full = lambda a: pl.BlockSpec(a.shape, lambda b, c: (0,) * a.ndim)
    wkv_rows = wkv0.reshape(nb, RWKV_DIM, RWKV_HEAD)
    kern = functools.partial(_rwkv_kernel, cq=cq, t_valid=t_valid)
    y, wkv = pl.pallas_call(
        kern,
        out_shape=(jax.ShapeDtypeStruct((nb * t, RWKV_DIM), F32),
                   jax.ShapeDtypeStruct(wkv_rows.shape, F32)),
        grid=(nb, cps),
        in_specs=[pl.BlockSpec((cq, RWKV_SHIFT_DIM), lambda b, c: (b * cps + c, 0)),
                  pl.BlockSpec((1, n_pad, RWKV_SHIFT_DIM), lambda b, c: (b, 0, 0)),
                  pl.BlockSpec((1, RWKV_DIM, RWKV_HEAD), lambda b, c: (b, 0, 0))] + [full(a) for a in consts],
        out_specs=(pl.BlockSpec((cq, RWKV_DIM), lambda b, c: (b * cps + c, 0)),
                   pl.BlockSpec((1, RWKV_DIM, RWKV_HEAD), lambda b, c: (b, 0, 0))),
        scratch_shapes=[pltpu.VMEM((cq + n_pad, RWKV_SHIFT_DIM), F32),
                        pltpu.VMEM((RWKV_DIM // RWKV_GROUP_LANES, RWKV_GROUP_LANES, RWKV_GROUP_LANES), F32)],
        compiler_params=_cparams("parallel", "arbitrary"),
        name=name,
    )(rw, shift0, wkv_rows, *consts)
    return y, wkv.reshape(wkv0.shape)


NEG_BIG = -1e30


def _mix_kernel(g_ref, yap_ref, ybp_ref, yas_ref, ybs_ref, h_ref, wout_ref, ln2_ref, rw_ref, rb_ref,
                h1_ref, xn_ref, idx_ref, gate_ref):
    i = pl.program_id(0)
    last = i == pl.num_programs(0) - 1
    ya = jnp.where(last, yas_ref[...], yap_ref[...])
    yb = jnp.where(last, ybs_ref[...], ybp_ref[...])
    g = g_ref[...]
    merged = _sigmoid(g[:, :D_MODEL]) * ya + _sigmoid(g[:, D_MODEL:]) * yb
    h1 = h_ref[...] + jnp.dot(merged.astype(BF16), wout_ref[...], preferred_element_type=F32)
    h1_ref[...] = h1
    ms = jnp.mean(h1 * h1, axis=-1, keepdims=True)
    xn = h1 * lax.rsqrt(ms + RMS_EPS) * ln2_ref[...]
    xn_ref[...] = xn
    xh, xm, xl = _split3(xn)
    logits = jnp.dot(jnp.concatenate([xh, xh, xm, xh, xl, xm], axis=1), rw_ref[...],
                     preferred_element_type=F32) + rb_ref[...]
    lane = lax.broadcasted_iota(jnp.int32, logits.shape, 1).astype(F32)
    idx_out = jnp.zeros(logits.shape, F32)
    val_out = jnp.zeros(logits.shape, F32)
    top0 = None
    for kth in range(TOP_K):
        m = jnp.max(logits, axis=-1, keepdims=True)
        sel = jnp.min(jnp.where(logits == m, lane, float(LANES)), axis=-1, keepdims=True)
        if kth == 0:
            top0 = m
        idx_out = jnp.where(lane == kth, sel, idx_out)
        val_out = jnp.where(lane == kth, jnp.exp(m - top0), val_out)
        logits = jnp.where(lane == sel, -jnp.inf, logits)
    idx_ref[...] = idx_out.astype(jnp.int32)
    gate_ref[...] = val_out / jnp.sum(val_out, axis=-1, keepdims=True)


def mix_and_route(gates, ya_p, yb_p, ya_s, yb_s, h_rows, w_out_bf, ln2_g, router_w, router_b, tm):
    rows = h_rows.shape[0]
    n_tiles = rows // tm
    last_p = ya_p.shape[0] // tm - 1
    wh, wm, wl = _split3(jnp.pad(router_w, ((0, 0), (0, LANES - N_EXPERTS))))
    rw_pad = jnp.concatenate([wh, wm, wh, wl, wh, wm], axis=0)
    rb_pad = jnp.pad(router_b, (0, LANES - N_EXPERTS), constant_values=NEG_BIG).reshape(1, LANES)
    row_spec = lambda n: pl.BlockSpec((tm, n), lambda i: (i, 0))
    prompt_spec = pl.BlockSpec((tm, D_MODEL), lambda i: (jnp.minimum(i, last_p), 0))
    fixed = lambda a: pl.BlockSpec(a.shape, lambda i: (0,) * a.ndim)
    return pl.pallas_call(
        _mix_kernel,
        out_shape=(jax.ShapeDtypeStruct((rows, D_MODEL), F32), jax.ShapeDtypeStruct((rows, D_MODEL), F32),
                   jax.ShapeDtypeStruct((rows, LANES), jnp.int32), jax.ShapeDtypeStruct((rows, LANES), F32)),
        grid=(n_tiles,),
        in_specs=[row_spec(2 * D_MODEL), prompt_spec, prompt_spec, fixed(ya_s), fixed(yb_s), row_spec(D_MODEL),
                  fixed(w_out_bf), pl.BlockSpec((1, D_MODEL), lambda i: (0, 0)), fixed(rw_pad), fixed(rb_pad)],
        out_specs=(row_spec(D_MODEL), row_spec(D_MODEL), row_spec(LANES), row_spec(LANES)),
        compiler_params=_cparams("parallel"),
        name="mix_and_route",
    )(gates, ya_p, yb_p, ya_s, yb_s, h_rows, w_out_bf, ln2_g.reshape(1, D_MODEL), rw_pad, rb_pad)


def next_run_expert(block_e):
    n = block_e.shape[0]
    idx = jnp.arange(n, dtype=jnp.int32)
    change = jnp.concatenate([jnp.ones((1,), bool), block_e[1:] != block_e[:-1]])
    later_change = jnp.concatenate([jnp.where(change, idx, n)[1:], jnp.full((1,), n, jnp.int32)])
    next_pos = lax.cummin(later_change, axis=0, reverse=True)
    onehot = next_pos[:, None] == idx[None, :]
    return jnp.where(next_pos < n, jnp.sum(jnp.where(onehot, block_e[None, :], 0), axis=1), -1).astype(jnp.int32)


WEIGHT_DMA_PRIORITY = 1


def _stream_expert_weights(be_ref, nxt_ref, copies, cast):
    i = pl.program_id(0)
    first = jnp.logical_or(i == 0, be_ref[i] != be_ref[jnp.maximum(i, 1) - 1])

    @pl.when(i == 0)
    def _():
        for cp in copies(be_ref[0]):
            cp.start(priority=WEIGHT_DMA_PRIORITY)

    @pl.when(first)
    def _():
        for cp in copies(be_ref[i]):
            cp.wait()
        cast()

    @pl.when(jnp.logical_and(first, nxt_ref[i] >= 0))
    def _():
        for cp in copies(nxt_ref[i]):
            cp.start(priority=WEIGHT_DMA_PRIORITY)


def _moe_up_kernel(be_ref, nxt_ref, nb_ref, tok_ref, tok_next_ref, x_hbm, w_hbm, b_ref, o_ref, w_f32, w_bf, sem,
                   rows_ref, rsem):
    i = pl.program_id(0)
    n_used = nb_ref[0]
    slot = lax.rem(i, 2)

    def row_copies(idx_ref, buf):
        return [pltpu.make_async_copy(x_hbm.at[pl.ds(idx_ref[0, 0, r], 1), :],
                                      rows_ref.at[buf, pl.ds(r, 1), :], rsem.at[buf]) for r in range(MOE_BLOCK)]

    @pl.when(jnp.logical_and(i == 0, n_used > 0))
    def _():
        for cp in row_copies(tok_ref, 0):
            cp.start()

    @pl.when(i + 1 < n_used)
    def _():
        for cp in row_copies(tok_next_ref, 1 - slot):
            cp.start()

    def copies(e):
        return [pltpu.make_async_copy(w_hbm.at[e, :, pl.ds(half * D_FF, D_FF)], w_f32.at[half], sem.at[half])
                for half in range(2)]

    def cast():
        w_bf[...] = w_f32[...].astype(BF16)

    _stream_expert_weights(be_ref, nxt_ref, copies, cast)

    @pl.when(i < n_used)
    def _():
        for cp in row_copies(tok_ref, slot):
            cp.wait()
        x = rows_ref[slot].astype(BF16)
        g = jnp.dot(x, w_bf[0], preferred_element_type=F32) + b_ref[0, :, :D_FF]
        u = jnp.dot(x, w_bf[1], preferred_element_type=F32) + b_ref[0, :, D_FF:]
        g = jnp.minimum(g, SWIGLU_LIMIT)
        u = jnp.clip(u, -SWIGLU_LIMIT, SWIGLU_LIMIT)
        o_ref[...] = ((u + 1.0) * (g * _sigmoid(g * SWIGLU_ALPHA))).astype(o_ref.dtype)

    @pl.when(i >= n_used)
    def _():
        o_ref[...] = jnp.zeros_like(o_ref)


def moe_up(x, slot_tok, block_e, next_e, n_used, w_gate_up, b_gate_up):
    n_blocks = slot_tok.shape[0] // MOE_BLOCK
    tok3 = slot_tok.reshape(n_blocks, 1, MOE_BLOCK)
    grid_spec = pltpu.PrefetchScalarGridSpec(
        num_scalar_prefetch=3,
        grid=(n_blocks,),
        in_specs=[pl.BlockSpec((1, 1, MOE_BLOCK), lambda i, be, nx, nb: (i, 0, 0), memory_space=pltpu.SMEM),
                  pl.BlockSpec((1, 1, MOE_BLOCK), lambda i, be, nx, nb: (jnp.minimum(i + 1, n_blocks - 1), 0, 0),
                               memory_space=pltpu.SMEM),
                  pl.BlockSpec(memory_space=pl.ANY),
                  pl.BlockSpec(memory_space=pl.ANY),
                  pl.BlockSpec((1, 1, 2 * D_FF), lambda i, be, nx, nb: (be[i], 0, 0))],
        out_specs=pl.BlockSpec((MOE_BLOCK, D_FF), lambda i, be, nx, nb: (i, 0)),
        scratch_shapes=[pltpu.VMEM((2, D_MODEL, D_FF), F32), pltpu.VMEM((2, D_MODEL, D_FF), BF16),
                        pltpu.SemaphoreType.DMA((2,)),
                        pltpu.VMEM((2, MOE_BLOCK, D_MODEL), x.dtype), pltpu.SemaphoreType.DMA((2,))],
    )
    return pl.pallas_call(
        _moe_up_kernel,
        out_shape=jax.ShapeDtypeStruct((n_blocks * MOE_BLOCK, D_FF), BF16),
        grid_spec=grid_spec,
        compiler_params=_cparams("arbitrary"),
        name="moe_up",
    )(block_e, next_e, n_used, tok3, tok3, x, w_gate_up, b_gate_up.reshape(N_EXPERTS, 1, 2 * D_FF))


def _moe_down_kernel(be_ref, nxt_ref, nb_ref, h_ref, w_hbm, bd_ref, o_ref, w_f32, w_bf, sem):
    i = pl.program_id(0)

    def copies(e):
        return [pltpu.make_async_copy(w_hbm.at[e], w_f32, sem)]

    def cast():
        w_bf[...] = w_f32[...].astype(BF16)

    _stream_expert_weights(be_ref, nxt_ref, copies, cast)

    @pl.when(i < nb_ref[0])
    def _():
        o_ref[...] = jnp.dot(h_ref[...], w_bf[...], preferred_element_type=F32) + bd_ref[0]

    @pl.when(i >= nb_ref[0])
    def _():
        o_ref[...] = jnp.zeros_like(o_ref)


def moe_down(hb, block_e, next_e, n_used, w_down, b_down):
    n_blocks = hb.shape[0] // MOE_BLOCK
    grid_spec = pltpu.PrefetchScalarGridSpec(
        num_scalar_prefetch=3,
        grid=(n_blocks,),
        in_specs=[pl.BlockSpec((MOE_BLOCK, D_FF), lambda i, be, nx, nb: (i, 0)),
                  pl.BlockSpec(memory_space=pl.ANY),
                  pl.BlockSpec((1, 1, D_MODEL), lambda i, be, nx, nb: (be[i], 0, 0))],
        out_specs=pl.BlockSpec((MOE_BLOCK, D_MODEL), lambda i, be, nx, nb: (i, 0)),
        scratch_shapes=[pltpu.VMEM((D_FF, D_MODEL), F32), pltpu.VMEM((D_FF, D_MODEL), BF16),
                        pltpu.SemaphoreType.DMA(())],
    )
    return pl.pallas_call(
        _moe_down_kernel,
        out_shape=jax.ShapeDtypeStruct((n_blocks * MOE_BLOCK, D_MODEL), F32),
        grid_spec=grid_spec,
        compiler_params=_cparams("arbitrary"),
        name="moe_down",
    )(block_e, next_e, n_used, hb, w_down, b_down.reshape(N_EXPERTS, 1, D_MODEL))


def _combine_kernel(slot_ref, slot_next_ref, row0_ref, gate_ref, lnf_ref, yb_hbm, h_hbm, o_ref, rows_ref, h_vmem,
                    sem, hsem):
    i = pl.program_id(0)
    n_tiles = pl.num_programs(0)
    tm = o_ref.shape[-2]
    slot = lax.rem(i, 2)

    def copies(idx_ref, tile, buf):
        row0 = pl.multiple_of(row0_ref[tile], SUBLANES)
        cps = [pltpu.make_async_copy(h_hbm.at[pl.ds(row0, tm), :], h_vmem.at[buf], hsem.at[buf])]
        for n in range(TOP_K * tm):
            cps.append(pltpu.make_async_copy(yb_hbm.at[pl.ds(idx_ref[0, 0, n], 1), :],
                                             rows_ref.at[buf, n // tm, pl.ds(n % tm, 1), :], sem.at[buf]))
        return cps

    @pl.when(i == 0)
    def _():
        for n, cp in enumerate(copies(slot_ref, 0, 0)):
            cp.start(priority=n % 2)

    @pl.when(i + 1 < n_tiles)
    def _():
        for n, cp in enumerate(copies(slot_next_ref, jnp.minimum(i + 1, n_tiles - 1), 1 - slot)):
            cp.start(priority=n % 2)

    for cp in copies(slot_ref, i, slot):
        cp.wait()
    gate = gate_ref[...]
    acc = h_vmem[slot]
    for k in range(TOP_K):
        acc = acc + rows_ref[slot, k] * gate[:, k:k + 1]
    ms = jnp.mean(acc * acc, axis=-1, keepdims=True)
    o_ref[...] = (acc * lax.rsqrt(ms + RMS_EPS) * lnf_ref[...]).reshape(o_ref.shape)


def moe_combine(yb, h1, slot_of_pair, gates, row0, lnf_g, out_shape, out_index_map, tm):
    n_tiles = row0.shape[0]
    grid_spec = pltpu.PrefetchScalarGridSpec(
        num_scalar_prefetch=0,
        grid=(n_tiles,),
        in_specs=[pl.BlockSpec((1, 1, TOP_K * tm), lambda i: (i, 0, 0), memory_space=pltpu.SMEM),
                  pl.BlockSpec((1, 1, TOP_K * tm), lambda i: (jnp.minimum(i + 1, n_tiles - 1), 0, 0),
                               memory_space=pltpu.SMEM),
                  pl.BlockSpec(memory_space=pltpu.SMEM),
                  pl.BlockSpec((tm, LANES), lambda i: (i, 0)),
                  pl.BlockSpec((1, D_MODEL), lambda i: (0, 0)),
                  pl.BlockSpec(memory_space=pl.ANY),
                  pl.BlockSpec(memory_space=pl.ANY)],
        out_specs=pl.BlockSpec(out_shape[0], out_index_map),
        scratch_shapes=[pltpu.VMEM((2, TOP_K, tm, D_MODEL), F32), pltpu.VMEM((2, tm, D_MODEL), F32),
                        pltpu.SemaphoreType.DMA((2,)), pltpu.SemaphoreType.DMA((2,))],
    )
    return pl.pallas_call(
        _combine_kernel,
        out_shape=jax.ShapeDtypeStruct(out_shape[1], F32),
        grid_spec=grid_spec,
        compiler_params=_cparams("arbitrary"),
        name="moe_combine",
    )(slot_of_pair, slot_of_pair, row0, gates, lnf_g.reshape(1, D_MODEL), yb, h1)


PROMPT_PAD_T = 2176
ROW_TILE = 1104


def kernel(x_prompt, x_sample, state_ssm_conv, state_ssm, state_rwkv_shift, state_rwkv_wkv, meta_tokens, ln1_g, w_in, ssm_conv_w, ssm_conv_b, ssm_dt_bias, ssm_A_log, ssm_D, ssm_norm_g, rwkv_mu, rwkv_w0, rwkv_w2, rwkv_a0, rwkv_a2, rwkv_k_k, rwkv_k_a, rwkv_r_k, rwkv_gn_g, rwkv_gn_b, w_out, ln2_g, router_w, router_b, w_gate_up, b_gate_up, w_down, b_down, lnf_g):
    bp, seq, d = x_prompt.shape
    bs = x_sample.shape[0]
    t_prompt = N_META + seq
    tail = jnp.zeros((PROMPT_PAD_T - t_prompt, d), F32)
    pieces = []
    for b in range(bp):
        pieces += [meta_tokens, x_prompt[b], tail]
    h_rows = jnp.concatenate(pieces + [x_sample.reshape(bs, d)], axis=0)
    n_prow = bp * PROMPT_PAD_T

    l = 0
    xn = rmsnorm_rows(h_rows, ln1_g[l], BF16, ROW_TILE)
    w = w_in[l]
    o_xbc = SSM_D_INNER
    o_dt = o_xbc + SSM_CONV_DIM
    o_rw = o_dt + SSM_N_HEADS
    o_g = o_rw + RWKV_SHIFT_DIM
    w_z = w[:, :o_xbc].astype(BF16)
    w_xbc = w[:, o_xbc:o_dt].astype(BF16)
    w_dt = jnp.pad(w[:, o_dt:o_rw], ((0, 0), (0, LANES - SSM_N_HEADS))).astype(BF16)
    w_rw = w[:, o_rw:o_g].astype(BF16)
    w_g = w[:, o_g:].astype(BF16)
    z = matmul_bf16(xn, w_z, ROW_TILE, 1024, "proj_z")
    xbc = matmul_bf16(xn, w_xbc, ROW_TILE, 1024, "proj_xbc")
    dtr = matmul_bf16(xn, w_dt, ROW_TILE, LANES, "proj_dt")

    ssm_args = (ssm_conv_w[l], ssm_conv_b[l], ssm_dt_bias[l], ssm_A_log[l], ssm_D[l], ssm_norm_g[l])
    conv0_p = jnp.zeros((bp, SUBLANES, SSM_CONV_DIM), F32)
    s0_p = jnp.zeros((bp, SSM_N_HEADS, SSM_HEAD_DIM, SSM_D_STATE), F32)
    ya_p, ssm_p = ssd_branch(xbc, z, dtr, conv0_p, s0_p, *ssm_args, t=PROMPT_PAD_T, q=SSM_CHUNK,
                             t_valid=t_prompt, name="ssd_prompt")
    ya_s, ssm_s = ssd_step(xbc, z, dtr, n_prow, jnp.swapaxes(state_ssm_conv[l], 0, 1), state_ssm[l], *ssm_args,
                           bt=SUBLANES, name="ssd_sample")
    rw = matmul_bf16(xn, w_rw, ROW_TILE, 896, "proj_rw")
    rwkv_args = (rwkv_mu[l], rwkv_w0[l], rwkv_w2[l], rwkv_a0[l], rwkv_a2[l], rwkv_k_k[l], rwkv_k_a[l],
                 rwkv_r_k[l], rwkv_gn_g[l], rwkv_gn_b[l])
    shift0_p = jnp.zeros((bp, SUBLANES, RWKV_SHIFT_DIM), F32)
    wkv0_p = jnp.zeros((bp, RWKV_N_HEADS, RWKV_HEAD, RWKV_HEAD), F32)
    yb_p, wkv_p = rwkv_branch(rw, shift0_p, wkv0_p, *rwkv_args, t=PROMPT_PAD_T, cq=RWKV_CHUNK, t_valid=t_prompt,
                              name="rwkv_prompt")
    yb_s, wkv_t = rwkv_step(rw, n_prow, state_rwkv_shift[l], jnp.transpose(state_rwkv_wkv[l], (1, 2, 3, 0)),
                            *rwkv_args, name="rwkv_sample")
    wkv_s = jnp.transpose(wkv_t, (3, 0, 1, 2))
    gates = matmul_bf16(xn, w_g, ROW_TILE, 1024, "proj_gates")
    h1, xn2, top_idx, top_gate = mix_and_route(
        gates, ya_p, yb_p, ya_s, yb_s, h_rows,
        w_out[l].astype(BF16), ln2_g[l], router_w[l], router_b[l], MOE_BLOCK)

    n_rows = h_rows.shape[0]
    row_id = jnp.arange(n_rows, dtype=jnp.int32)
    valid = jnp.logical_or(row_id >= n_prow, row_id % PROMPT_PAD_T < t_prompt)
    yb, slot_of_pair = moe_expert_rows(xn2, top_idx, valid, (bp * t_prompt + bs) * TOP_K, w_gate_up[l],
                                       b_gate_up[l], w_down[l], b_down[l])

    def combine(row0, out_block, out_full, out_map):
        rows = (row0[:, None] + jnp.arange(MOE_BLOCK, dtype=jnp.int32)[None, :])
        slots = jnp.transpose(slot_of_pair[rows], (0, 2, 1)).reshape(row0.shape[0], 1, TOP_K * MOE_BLOCK)
        return moe_combine(yb, h1, slots, top_gate[rows.reshape(-1)], row0, lnf_g, (out_block, out_full),
                           out_map, MOE_BLOCK)

    tiles_per_seq = seq // MOE_BLOCK
    tile_id = jnp.arange(bp * tiles_per_seq, dtype=jnp.int32)
    row0_p = (tile_id // tiles_per_seq) * PROMPT_PAD_T + N_META + (tile_id % tiles_per_seq) * MOE_BLOCK
    y_prompt = combine(row0_p, (1, MOE_BLOCK, d), (bp, seq, d),
                       lambda i: (i // tiles_per_seq, i % tiles_per_seq, 0))
    y_sample = combine(jnp.full((1,), n_prow, jnp.int32), (MOE_BLOCK, d), (bs, d), lambda i: (i, 0))

    last = [b * PROMPT_PAD_T + t_prompt - 1 for b in range(bp)]
    prompt_conv = jnp.stack([xbc[r - (SSM_CONV - 2):r + 1] for r in last])
    prompt_shift = jnp.stack([rw[r] for r in last])
    sample_conv = jnp.concatenate([state_ssm_conv[l][:, 1:], xbc[n_prow:, None]], axis=1)
    return (y_prompt, y_sample.reshape(bs, 1, d), prompt_conv[None], ssm_p[None], prompt_shift[None], wkv_p[None],
            sample_conv[None], ssm_s[None], rw[n_prow:][None], wkv_s[None])


def moe_expert_rows(xn2, top_idx, valid, n_valid_pairs, w_gate_up, b_gate_up, w_down, b_down):
    n_rows = xn2.shape[0]
    experts = jnp.arange(N_EXPERTS, dtype=jnp.int32)
    key = jnp.where(valid[:, None], top_idx[:, :TOP_K], N_EXPERTS).reshape(-1)
    n_pairs = key.shape[0]
    n_blocks = -(-n_valid_pairs // MOE_BLOCK) + N_EXPERTS
    n_slots = n_blocks * MOE_BLOCK
    counts = jnp.sum((key[:, None] == experts[None, :]).astype(jnp.int32), axis=0)
    padded = (counts + MOE_BLOCK - 1) // MOE_BLOCK * MOE_BLOCK
    ends = jnp.cumsum(padded)
    fill_e = jnp.repeat(experts, MOE_BLOCK)
    fill_r = jnp.tile(jnp.arange(MOE_BLOCK, dtype=jnp.int32), N_EXPERTS)
    fill_key = jnp.where(fill_r < jnp.repeat(padded - counts, MOE_BLOCK), fill_e, N_EXPERTS + 1)
    keys = jnp.concatenate([key, fill_key])
    iota = jnp.arange(keys.shape[0], dtype=jnp.int32)
    _, order = lax.sort((keys, iota), num_keys=1, is_stable=True)
    _, slot_of = lax.sort((order, iota), num_keys=1, is_stable=True)
    slot_tok = jnp.where(order[:n_slots] < n_pairs, order[:n_slots] // TOP_K, 0).astype(jnp.int32)
    slot_of_pair = jnp.minimum(slot_of[:n_pairs], n_slots - 1).reshape(n_rows, TOP_K)
    block_start = jnp.arange(n_blocks, dtype=jnp.int32) * MOE_BLOCK
    block_e = jnp.minimum(jnp.sum((ends[None, :] <= block_start[:, None]).astype(jnp.int32), axis=1),
                          N_EXPERTS - 1).astype(jnp.int32)
    n_used = (ends[-1:] // MOE_BLOCK).astype(jnp.int32)
    next_e = next_run_expert(block_e)

    hb = moe_up(xn2, slot_tok, block_e, next_e, n_used, w_gate_up, b_gate_up)
    yb = moe_down(hb, block_e, next_e, n_used, w_down, b_down)
    return yb, slot_of_pair
```
